```python
import jax, jax.numpy as jnp
from jax import lax
import numpy as np

D_MODEL = 2048
BATCH = 4
SEQ = 8192
DEPTH = 1

GRID_W = 64
ROPE_THETA = 10000.0
Q_BLOCK = 128
HEAD_DIM = 128
GQA_Q_HEADS = 8
GQA_KV_HEADS = 2
MLA_HEADS = 8
MLA_Q_RANK = 512
MLA_KV_RANK = 512
MLA_NOPE_DIM = 128
MLA_ROPE_DIM = 64
MLA_V_DIM = 128
N_BRANCHES = 2
D_FF = ((8 * D_MODEL + 3 * 256 - 1) // (3 * 256)) * 256
DEEPNORM_ALPHA = (2.0 * DEPTH) ** 0.25
DEEPNORM_BETA = (8.0 * DEPTH) ** -0.25
LN_EPS = 1e-5
RMS_EPS = 1e-6

GQA_Q_COLS = GQA_Q_HEADS * HEAD_DIM
GQA_KV_COLS = GQA_KV_HEADS * HEAD_DIM
GATE_COLS = N_BRANCHES * D_MODEL
IN_SIZES = [GQA_Q_COLS, GQA_KV_COLS, GQA_KV_COLS, MLA_Q_RANK, MLA_KV_RANK, MLA_ROPE_DIM, GATE_COLS]
IN_SPLITS = list(np.cumsum(IN_SIZES)[:-1].tolist())
IN_COLS = sum(IN_SIZES)

kernel_name = "hybrid_gqa_mla_gated_deepnorm_encoder"


def layer_norm(x):
    xf = x.astype(jnp.float32)
    mu = jnp.mean(xf, axis=-1, keepdims=True)
    var = jnp.mean(jnp.square(xf - mu), axis=-1, keepdims=True)
    return ((xf - mu) * lax.rsqrt(var + LN_EPS)).astype(x.dtype)


def layer_norm_affine(x, g, b):
    xf = x.astype(jnp.float32)
    mu = jnp.mean(xf, axis=-1, keepdims=True)
    var = jnp.mean(jnp.square(xf - mu), axis=-1, keepdims=True)
    y = (xf - mu) * lax.rsqrt(var + LN_EPS) * g.astype(jnp.float32) + b.astype(jnp.float32)
    return y.astype(x.dtype)


def rms_norm(x, g):
    xf = x.astype(jnp.float32)
    y = xf * lax.rsqrt(jnp.mean(jnp.square(xf), axis=-1, keepdims=True) + RMS_EPS) * g.astype(jnp.float32)
    return y.astype(x.dtype)


def modulate(h, shift, scale):
    return h * (1.0 + scale) + shift


def axial_rope_tables(seq, dim):
    rows = seq // GRID_W
    quarter = dim // 4
    inv_freq = ROPE_THETA ** (-jnp.arange(quarter, dtype=jnp.float32) / quarter)
    row_ang = jnp.arange(rows, dtype=jnp.float32)[:, None] * inv_freq
    col_ang = jnp.arange(GRID_W, dtype=jnp.float32)[:, None] * inv_freq
    ang = jnp.concatenate([
        jnp.broadcast_to(row_ang[:, None, :], (rows, GRID_W, quarter)),
        jnp.broadcast_to(col_ang[None, :, :], (rows, GRID_W, quarter)),
    ], axis=-1).reshape(seq, 2 * quarter)
    return jnp.cos(ang), jnp.sin(ang)


def apply_rope(x, cos, sin):
    xf = x.astype(jnp.float32).reshape(*x.shape[:-1], x.shape[-1] // 2, 2)
    x0, x1 = xf[..., 0], xf[..., 1]
    c = cos[None, :, None, :]
    s = sin[None, :, None, :]
    out = jnp.stack([x0 * c - x1 * s, x0 * s + x1 * c], axis=-1).reshape(x.shape)
    return out.astype(x.dtype)


def gqa_attention(q, k, v):
    b, s, hq, d = q.shape
    hkv = k.shape[2]
    g = hq // hkv
    nb = s // Q_BLOCK
    qb = q.reshape(b, nb, Q_BLOCK, hkv, g, d).transpose(1, 0, 2, 3, 4, 5)
    scale = d ** -0.5

    def block(q_blk):
        sc = jnp.einsum('bqkgd,bskd->bkgqs', q_blk, k, preferred_element_type=jnp.float32) * scale
        p = jax.nn.softmax(sc, axis=-1).astype(v.dtype)
        return jnp.einsum('bkgqs,bskd->bqkgd', p, v)

    out = lax.map(block, qb)
    return out.transpose(1, 0, 2, 3, 4, 5).reshape(b, s, hq * d)


def mla_attention(q_nope, q_rope, k_nope, k_rope, v):
    b, s, h, dn = q_nope.shape
    dr = q_rope.shape[-1]
    dv = v.shape[-1]
    nb = s // Q_BLOCK
    qn = q_nope.reshape(b, nb, Q_BLOCK, h, dn).transpose(1, 0, 2, 3, 4)
    qr = q_rope.reshape(b, nb, Q_BLOCK, h, dr).transpose(1, 0, 2, 3, 4)
    scale = (dn + dr) ** -0.5

    def block(args):
        qn_blk, qr_blk = args
        sc = (jnp.einsum('bqhd,bshd->bhqs', qn_blk, k_nope, preferred_element_type=jnp.float32)
              + jnp.einsum('bqhr,bsr->bhqs', qr_blk, k_rope, preferred_element_type=jnp.float32)) * scale
        p = jax.nn.softmax(sc, axis=-1).astype(v.dtype)
        return jnp.einsum('bhqs,bshd->bqhd', p, v)

    out = lax.map(block, (qn, qr))
    return out.transpose(1, 0, 2, 3, 4).reshape(b, s, h * dv)


def _normal(k, shape, fan_in, gain=1.0):
    return jax.random.normal(k, shape, jnp.float32) * (gain * fan_in ** -0.5)


def _gain(k, shape):
    return 1.0 + 0.02 * jax.random.normal(k, shape, jnp.float32)


def setup_inputs(seed: int = 0) -> dict:
    key = jax.random.key(seed)
    ks = jax.random.split(key, 24)
    L, D = DEPTH, D_MODEL
    x = jax.random.normal(ks[0], (BATCH, SEQ, D), jnp.float32)
    c = jax.random.normal(ks[1], (BATCH, D), jnp.float32)
    w_ada = _normal(ks[2], (L, D, 6 * D), D, 0.5)
    b_ada = 0.02 * jax.random.normal(ks[3], (L, 6 * D), jnp.float32)
    col_scale = jnp.concatenate([
        jnp.ones((GQA_Q_COLS + GQA_KV_COLS,), jnp.float32),
        jnp.full((GQA_KV_COLS,), DEEPNORM_BETA, jnp.float32),
        jnp.ones((MLA_Q_RANK + MLA_KV_RANK + MLA_ROPE_DIM + GATE_COLS,), jnp.float32),
    ])
    w_in = _normal(ks[4], (L, D, IN_COLS), D) * col_scale
    b_gates = 0.01 * jax.random.normal(ks[5], (L, GATE_COLS), jnp.float32)
    gqa_q_gain = _gain(ks[6], (L, HEAD_DIM))
    gqa_k_gain = _gain(ks[7], (L, HEAD_DIM))
    mla_q_gain = _gain(ks[8], (L, MLA_Q_RANK))
    mla_kv_gain = _gain(ks[9], (L, MLA_KV_RANK))
    w_mla_uq = _normal(ks[10], (L, MLA_Q_RANK, MLA_HEADS * (MLA_NOPE_DIM + MLA_ROPE_DIM)), MLA_Q_RANK)
    w_uk = _normal(ks[11], (L, MLA_KV_RANK, MLA_HEADS, MLA_NOPE_DIM), MLA_KV_RANK)
    w_uv = _normal(ks[12], (L, MLA_KV_RANK, MLA_HEADS, MLA_V_DIM), MLA_KV_RANK, DEEPNORM_BETA)
    w_mla_ukv = jnp.concatenate([w_uk, w_uv], axis=-1).reshape(L, MLA_KV_RANK, MLA_HEADS * (MLA_NOPE_DIM + MLA_V_DIM))
    w_branch_gqa = _normal(ks[13], (L, GQA_Q_COLS, D), GQA_Q_COLS, DEEPNORM_BETA)
    w_branch_mla = _normal(ks[14], (L, MLA_HEADS * MLA_V_DIM, D), MLA_HEADS * MLA_V_DIM, DEEPNORM_BETA)
    w_out = _normal(ks[15], (L, D, D), D, DEEPNORM_BETA)
    ln1_g = _gain(ks[16], (L, D))
    ln1_b = 0.02 * jax.random.normal(ks[17], (L, D), jnp.float32)
    w_ffn_gate = _normal(ks[18], (L, D, D_FF), D, DEEPNORM_BETA)
    w_ffn_up = _normal(ks[19], (L, D, D_FF), D, DEEPNORM_BETA)
    w_ffn_down = _normal(ks[20], (L, D_FF, D), D_FF, DEEPNORM_BETA)
    ln2_g = _gain(ks[21], (L, D))
    ln2_b = 0.02 * jax.random.normal(ks[22], (L, D), jnp.float32)
    return {
        "x": x, "c": c, "w_ada": w_ada, "b_ada": b_ada, "w_in": w_in, "b_gates": b_gates,
        "gqa_q_gain": gqa_q_gain, "gqa_k_gain": gqa_k_gain, "mla_q_gain": mla_q_gain,
        "mla_kv_gain": mla_kv_gain, "w_mla_uq": w_mla_uq, "w_mla_ukv": w_mla_ukv,
        "w_branch_gqa": w_branch_gqa, "w_branch_mla": w_branch_mla, "w_out": w_out,
        "ln1_g": ln1_g, "ln1_b": ln1_b, "w_ffn_gate": w_ffn_gate, "w_ffn_up": w_ffn_up,
        "w_ffn_down": w_ffn_down, "ln2_g": ln2_g, "ln2_b": ln2_b,
    }


def reference(x, c, w_ada, b_ada, w_in, b_gates, gqa_q_gain, gqa_k_gain, mla_q_gain, mla_kv_gain,
              w_mla_uq, w_mla_ukv, w_branch_gqa, w_branch_mla, w_out, ln1_g, ln1_b,
              w_ffn_gate, w_ffn_up, w_ffn_down, ln2_g, ln2_b):
    b, s, _ = x.shape
    cos_g, sin_g = axial_rope_tables(s, HEAD_DIM)
    cos_m, sin_m = axial_rope_tables(s, MLA_ROPE_DIM)
    c_act = jax.nn.silu(c)
    for l in range(DEPTH):
        mod = (c_act @ w_ada[l] + b_ada[l])[:, None, :]
        shift1, scale1, gate1, shift2, scale2, gate2 = jnp.split(mod, 6, axis=-1)

        h = modulate(layer_norm(x), shift1, scale1)
        proj = h @ w_in[l]
        q_g, k_g, v_g, q_lat, kv_lat, k_r, gate_logits = jnp.split(proj, IN_SPLITS, axis=-1)

        q_g = apply_rope(rms_norm(q_g.reshape(b, s, GQA_Q_HEADS, HEAD_DIM), gqa_q_gain[l]), cos_g, sin_g)
        k_g = apply_rope(rms_norm(k_g.reshape(b, s, GQA_KV_HEADS, HEAD_DIM), gqa_k_gain[l]), cos_g, sin_g)
        v_g = v_g.reshape(b, s, GQA_KV_HEADS, HEAD_DIM)
        y_gqa = gqa_attention(q_g, k_g, v_g)

        q_m = (rms_norm(q_lat, mla_q_gain[l]) @ w_mla_uq[l]).reshape(b, s, MLA_HEADS, MLA_NOPE_DIM + MLA_ROPE_DIM)
        q_nope = q_m[..., :MLA_NOPE_DIM]
        q_rope = apply_rope(q_m[..., MLA_NOPE_DIM:], cos_m, sin_m)
        kv = (rms_norm(kv_lat, mla_kv_gain[l]) @ w_mla_ukv[l]).reshape(b, s, MLA_HEADS, MLA_NOPE_DIM + MLA_V_DIM)
        k_nope = kv[..., :MLA_NOPE_DIM]
        v_m = kv[..., MLA_NOPE_DIM:]
        k_rope = apply_rope(k_r[:, :, None, :], cos_m, sin_m)[:, :, 0, :]
        y_mla = mla_attention(q_nope, q_rope, k_nope, k_rope, v_m)

        g_gqa, g_mla = jnp.split(jax.nn.sigmoid(gate_logits + b_gates[l]), N_BRANCHES, axis=-1)
        merged = g_gqa * (y_gqa @ w_branch_gqa[l]) + g_mla * (y_mla @ w_branch_mla[l])
        x = layer_norm_affine(DEEPNORM_ALPHA * x + gate1 * (merged @ w_out[l]), ln1_g[l], ln1_b[l])

        h = modulate(layer_norm(x), shift2, scale2)
        f = (jax.nn.silu(h @ w_ffn_gate[l]) * (h @ w_ffn_up[l])) @ w_ffn_down[l]
        x = layer_norm_affine(DEEPNORM_ALPHA * x + gate2 * f, ln2_g[l], ln2_b[l])
    return x
```

```python
import functools

import jax
import jax.numpy as jnp
import numpy as np
from jax import lax
from jax.experimental import pallas as pl
from jax.experimental.pallas import tpu as pltpu

D_MODEL = 2048
GRID_W = 64
ROPE_THETA = 10000.0
HEAD_DIM = 128
GQA_Q_HEADS = 8
GQA_KV_HEADS = 2
MLA_HEADS = 8
MLA_Q_RANK = 512
MLA_KV_RANK = 512
MLA_NOPE_DIM = 128
MLA_ROPE_DIM = 64
MLA_V_DIM = 128
D_FF = 5632
DEPTH = 1
DEEPNORM_ALPHA = (2.0 * DEPTH) ** 0.25
LN_EPS = 1e-5
RMS_EPS = 1e-6

GQA_Q_COLS = GQA_Q_HEADS * HEAD_DIM
GQA_KV_COLS = GQA_KV_HEADS * HEAD_DIM
MLA_QK_PAD = 256
LANES = 128

VMEM_LIMIT = 56 * 1024 * 1024

TM_IN = 256
TK_ATT = 256
TQ_GQA = 128
TQ_MLA = 512
TM_MERGE = 512
TN_MERGE = 512
TM_FFN = 512
TF_FFN = 512

BF16 = jnp.bfloat16
F32 = jnp.float32


def _const_spec(shape):
    nd = len(shape)
    return pl.BlockSpec(shape, lambda *_: (0,) * nd, pipeline_mode=pl.Buffered(1))


def _ln(x):
    mu = jnp.mean(x, axis=-1, keepdims=True)
    xc = x - mu
    var = jnp.mean(xc * xc, axis=-1, keepdims=True)
    return xc * lax.rsqrt(var + LN_EPS)


def _rms(x, gain):
    ms = jnp.mean(x * x, axis=-1, keepdims=True)
    return x * lax.rsqrt(ms + RMS_EPS) * gain


def _rope(x, c, s):
    return x * c + pltpu.roll(x, LANES // 2, 1) * s


def _adaln_kernel(c_ref, w_ref, b_ref, o_ref):
    c = c_ref[...]
    act = (c * jax.nn.sigmoid(c)).astype(BF16)
    o_ref[...] = jnp.dot(act, w_ref[...].astype(BF16), preferred_element_type=F32) + b_ref[...]


def _adaln(c_pad, w_ada, b_ada):
    n = w_ada.shape[1]
    tn = 1024
    return pl.pallas_call(
        _adaln_kernel,
        grid=(n // tn,),
        in_specs=[
            pl.BlockSpec((8, D_MODEL), lambda j: (0, 0)),
            pl.BlockSpec((D_MODEL, tn), lambda j: (0, j)),
            pl.BlockSpec((1, tn), lambda j: (0, j)),
        ],
        out_specs=pl.BlockSpec((8, tn), lambda j: (0, j)),
        out_shape=jax.ShapeDtypeStruct((8, n), F32),
        compiler_params=pltpu.CompilerParams(
            dimension_semantics=("arbitrary",), vmem_limit_bytes=VMEM_LIMIT),
        name="adaln",
    )(c_pad, w_ada, b_ada)


def _inproj_kernel(x_ref, mod_ref, cg_ref, sg_ref, cm_ref, sm_ref, w_ref,
                   gq_ref, gk_ref, gql_ref, gkvl_ref, wuq_ref, wuk_ref, wuv_ref,
                   qg_ref, kg_ref, vgt_ref, qm_ref, km_ref, vmt_ref, *, mla_scale):
    x = x_ref[...]
    shift = mod_ref[0, 0:1, :]
    scale = mod_ref[0, 1:2, :]
    h = (_ln(x) * (1.0 + scale) + shift).astype(BF16)
    proj = jnp.dot(h, w_ref[...], preferred_element_type=F32)

    cg = cg_ref[...]
    sg = sg_ref[...]
    cm = cm_ref[...]
    sm = sm_ref[...]

    for hd in range(GQA_Q_HEADS):
        q = proj[:, hd * HEAD_DIM:(hd + 1) * HEAD_DIM]
        q = _rope(_rms(q, gq_ref[...]), cg, sg)
        qg_ref[:, hd * HEAD_DIM:(hd + 1) * HEAD_DIM] = q.astype(BF16)
    off = GQA_Q_COLS
    for hd in range(GQA_KV_HEADS):
        k = proj[:, off + hd * HEAD_DIM: off + (hd + 1) * HEAD_DIM]
        k = _rope(_rms(k, gk_ref[...]), cg, sg)
        kg_ref[:, hd * HEAD_DIM:(hd + 1) * HEAD_DIM] = k.astype(BF16)
    off += GQA_KV_COLS
    for hd in range(GQA_KV_HEADS):
        v = proj[:, off + hd * HEAD_DIM: off + (hd + 1) * HEAD_DIM]
        vgt_ref[0, hd, 0] = v.T.astype(BF16)
    off += GQA_KV_COLS

    q_lat = _rms(proj[:, off: off + MLA_Q_RANK], gql_ref[...]).astype(BF16)
    off += MLA_Q_RANK
    kv_lat = _rms(proj[:, off: off + MLA_KV_RANK], gkvl_ref[...]).astype(BF16)
    off += MLA_KV_RANK
    k_rope = _rope(proj[:, off: off + LANES], cm, sm).astype(BF16)

    q_m = jnp.dot(q_lat, wuq_ref[...], preferred_element_type=F32) * mla_scale
    k_n = jnp.dot(kv_lat, wuk_ref[...], preferred_element_type=F32)
    v_m = jnp.dot(kv_lat, wuv_ref[...], preferred_element_type=F32)
    for hd in range(MLA_HEADS):
        b0 = hd * MLA_QK_PAD
        qm_ref[:, b0: b0 + LANES] = q_m[:, b0: b0 + LANES].astype(BF16)
        qm_ref[:, b0 + LANES: b0 + 2 * LANES] = _rope(
            q_m[:, b0 + LANES: b0 + 2 * LANES], cm, sm).astype(BF16)
        km_ref[:, b0: b0 + LANES] = k_n[:, hd * LANES:(hd + 1) * LANES].astype(BF16)
        km_ref[:, b0 + LANES: b0 + 2 * LANES] = k_rope
        vmt_ref[0, hd, 0] = v_m[:, hd * MLA_V_DIM:(hd + 1) * MLA_V_DIM].T.astype(BF16)


def _inproj(x2, mod, cg, sg, cm, sm, w_attn, gq, gk, gql, gkvl, wuq, wuk, wuv, batch, seq):
    t = x2.shape[0]
    tm = TM_IN
    nt = seq // tm
    n_attn = w_attn.shape[1]
    kern = functools.partial(
        _inproj_kernel, mla_scale=float((MLA_NOPE_DIM + MLA_ROPE_DIM) ** -0.5))
    tok = lambda i: (i, 0)
    pos = lambda i: (i % nt, 0)
    in_specs = [
        pl.BlockSpec((tm, D_MODEL), tok),
        pl.BlockSpec((1, 6, D_MODEL), lambda i: (i // nt, 0, 0)),
        pl.BlockSpec((tm, LANES), pos),
        pl.BlockSpec((tm, LANES), pos),
        pl.BlockSpec((tm, LANES), pos),
        pl.BlockSpec((tm, LANES), pos),
        _const_spec((D_MODEL, n_attn)),
        _const_spec((1, HEAD_DIM)),
        _const_spec((1, HEAD_DIM)),
        _const_spec((1, MLA_Q_RANK)),
        _const_spec((1, MLA_KV_RANK)),
        _const_spec(wuq.shape),
        _const_spec(wuk.shape),
        _const_spec(wuv.shape),
    ]
    out_shape = [
        jax.ShapeDtypeStruct((t, GQA_Q_COLS), BF16),
        jax.ShapeDtypeStruct((t, GQA_KV_COLS), BF16),
        jax.ShapeDtypeStruct((batch, GQA_KV_HEADS, seq // TK_ATT, HEAD_DIM, TK_ATT), BF16),
        jax.ShapeDtypeStruct((t, MLA_HEADS * MLA_QK_PAD), BF16),
        jax.ShapeDtypeStruct((t, MLA_HEADS * MLA_QK_PAD), BF16),
        jax.ShapeDtypeStruct((batch, MLA_HEADS, seq // TK_ATT, MLA_V_DIM, TK_ATT), BF16),
    ]
    vt_map = lambda i: (i // nt, 0, i % nt, 0, 0)
    out_specs = [
        pl.BlockSpec((tm, GQA_Q_COLS), tok),
        pl.BlockSpec((tm, GQA_KV_COLS), tok),
        pl.BlockSpec((1, GQA_KV_HEADS, 1, HEAD_DIM, TK_ATT), vt_map),
        pl.BlockSpec((tm, MLA_HEADS * MLA_QK_PAD), tok),
        pl.BlockSpec((tm, MLA_HEADS * MLA_QK_PAD), tok),
        pl.BlockSpec((1, MLA_HEADS, 1, MLA_V_DIM, TK_ATT), vt_map),
    ]
    return pl.pallas_call(
        kern,
        grid=(t // tm,),
        in_specs=in_specs,
        out_specs=out_specs,
        out_shape=out_shape,
        compiler_params=pltpu.CompilerParams(
            dimension_semantics=("arbitrary",), vmem_limit_bytes=VMEM_LIMIT),
        name="inproj",
    )(x2, mod, cg, sg, cm, sm, w_attn, gq, gk, gql, gkvl, wuq, wuk, wuv)


def _attn_kernel(q_ref, k_ref, vt_ref, o_ref, qt_scr, m_scr, l_scr, acc_scr,
                 *, groups, dq, dv, tq, tk, nk):
    for g in range(groups):
        qg = q_ref[0, :, g * dq:(g + 1) * dq].astype(F32)
        qt_scr[:, g * tq:(g + 1) * tq] = qg.T.astype(BF16)
    m_scr[...] = jnp.full(m_scr.shape, -jnp.inf, F32)
    l_scr[...] = jnp.zeros(l_scr.shape, F32)
    acc_scr[...] = jnp.zeros(acc_scr.shape, F32)

    def body(c, carry):
        start = pl.multiple_of(c * tk, tk)
        kc = k_ref[0, pl.ds(start, tk), :]
        st = jnp.dot(kc, qt_scr[...], preferred_element_type=F32)
        m_prev = m_scr[...]
        m_new = jnp.maximum(m_prev, jnp.max(st, axis=0, keepdims=True))
        alpha = jnp.exp(m_prev - m_new)
        p = jnp.exp(st - m_new)
        l_scr[...] = alpha * l_scr[...] + jnp.sum(p, axis=0, keepdims=True)
        pv = jnp.dot(vt_ref[0, 0, c], p.astype(BF16), preferred_element_type=F32)
        acc_scr[...] = alpha * acc_scr[...] + pv
        m_scr[...] = m_new
        return carry

    lax.fori_loop(0, nk, body, 0)

    out_t = acc_scr[...] / l_scr[...]
    for g in range(groups):
        o_ref[0, :, g * dv:(g + 1) * dv] = out_t[:, g * tq:(g + 1) * tq].T.astype(o_ref.dtype)


def _attention(q, k, vt, *, q_heads, kv_heads, dq, dv, tq):
    batch, seq, _ = q.shape
    groups = q_heads // kv_heads
    tk = TK_ATT
    nk = seq // tk
    mq = groups * tq
    kern = functools.partial(_attn_kernel, groups=groups, dq=dq, dv=dv, tq=tq, tk=tk, nk=nk)
    return pl.pallas_call(
        kern,
        grid=(batch, kv_heads, seq // tq),
        in_specs=[
            pl.BlockSpec((1, tq, groups * dq), lambda b, h, i: (b, i, h)),
            pl.BlockSpec((1, seq, dq), lambda b, h, i: (b, 0, h)),
            pl.BlockSpec((1, 1, nk, dv, tk), lambda b, h, i: (b, h, 0, 0, 0)),
        ],
        out_specs=pl.BlockSpec((1, tq, groups * dv), lambda b, h, i: (b, i, h)),
        out_shape=jax.ShapeDtypeStruct((batch, seq, q_heads * dv), BF16),
        scratch_shapes=[
            pltpu.VMEM((dq, mq), BF16),
            pltpu.VMEM((1, mq), F32),
            pltpu.VMEM((1, mq), F32),
            pltpu.VMEM((dv, mq), F32),
        ],
        compiler_params=pltpu.CompilerParams(
            dimension_semantics=("arbitrary", "arbitrary", "arbitrary"),
            vmem_limit_bytes=VMEM_LIMIT),
        name=f"attn_g{groups}",
    )(q, k, vt)


def _merge_kernel(x_ref, mod_ref, yg_ref, ym_ref, wga_ref, wgb_ref, bga_ref, bgb_ref,
                  wbg_ref, wbm_ref, wo_ref, g_ref, b_ref, o_ref, h_scr, acc_scr):
    j = pl.program_id(1)

    @pl.when(j == 0)
    def _():
        shift = mod_ref[0, 0:1, :]
        scale = mod_ref[0, 1:2, :]
        h_scr[...] = (_ln(x_ref[...]) * (1.0 + scale) + shift).astype(BF16)
        acc_scr[...] = jnp.zeros(acc_scr.shape, F32)

    h = h_scr[...]
    la = jnp.dot(h, wga_ref[...], preferred_element_type=F32) + bga_ref[...]
    lb = jnp.dot(h, wgb_ref[...], preferred_element_type=F32) + bgb_ref[...]
    a = jnp.dot(yg_ref[...], wbg_ref[...], preferred_element_type=F32)
    b = jnp.dot(ym_ref[...], wbm_ref[...], preferred_element_type=F32)
    merged = jax.nn.sigmoid(la) * a + jax.nn.sigmoid(lb) * b
    acc_scr[...] += jnp.dot(merged.astype(BF16), wo_ref[...], preferred_element_type=F32)

    @pl.when(j == pl.num_programs(1) - 1)
    def _():
        gate = mod_ref[0, 2:3, :]
        r = DEEPNORM_ALPHA * x_ref[...] + gate * acc_scr[...]
        o_ref[...] = _ln(r) * g_ref[...] + b_ref[...]


def _merge(x2, mod, yg, ym, w_gate, b_gate, wbg, wbm, wo, ln_g, ln_b, seq):
    t = x2.shape[0]
    tm, tn = TM_MERGE, TN_MERGE
    nt = seq // tm
    nj = D_MODEL // tn
    tok = lambda i, j: (i, 0)
    return pl.pallas_call(
        _merge_kernel,
        grid=(t // tm, nj),
        in_specs=[
            pl.BlockSpec((tm, D_MODEL), tok),
            pl.BlockSpec((1, 6, D_MODEL), lambda i, j: (i // nt, 0, 0)),
            pl.BlockSpec((tm, yg.shape[1]), tok),
            pl.BlockSpec((tm, ym.shape[1]), tok),
            pl.BlockSpec((D_MODEL, tn), lambda i, j: (0, j)),
            pl.BlockSpec((D_MODEL, tn), lambda i, j: (0, j + nj)),
            pl.BlockSpec((1, tn), lambda i, j: (0, j)),
            pl.BlockSpec((1, tn), lambda i, j: (0, j + nj)),
            pl.BlockSpec((wbg.shape[0], tn), lambda i, j: (0, j)),
            pl.BlockSpec((wbm.shape[0], tn), lambda i, j: (0, j)),
            pl.BlockSpec((tn, D_MODEL), lambda i, j: (j, 0)),
            pl.BlockSpec((1, D_MODEL), lambda i, j: (0, 0)),
            pl.BlockSpec((1, D_MODEL), lambda i, j: (0, 0)),
        ],
        out_specs=pl.BlockSpec((tm, D_MODEL), tok),
        out_shape=jax.ShapeDtypeStruct((t, D_MODEL), F32),
        scratch_shapes=[pltpu.VMEM((tm, D_MODEL), BF16), pltpu.VMEM((tm, D_MODEL), F32)],
        compiler_params=pltpu.CompilerParams(
            dimension_semantics=("arbitrary", "arbitrary"), vmem_limit_bytes=VMEM_LIMIT),
        name="merge",
    )(x2, mod, yg, ym, w_gate, w_gate, b_gate, b_gate, wbg, wbm, wo, ln_g, ln_b)


def _ffn_kernel(x_ref, mod_ref, wg_ref, wu_ref, wd_ref, g_ref, b_ref, o_ref, h_scr, acc_scr):
    j = pl.program_id(1)

    @pl.when(j == 0)
    def _():
        shift = mod_ref[0, 3:4, :]
        scale = mod_ref[0, 4:5, :]
        h_scr[...] = (_ln(x_ref[...]) * (1.0 + scale) + shift).astype(BF16)
        acc_scr[...] = jnp.zeros(acc_scr.shape, F32)

    h = h_scr[...]
    gt = jnp.dot(h, wg_ref[...], preferred_element_type=F32)
    up = jnp.dot(h, wu_ref[...], preferred_element_type=F32)
    a = (gt * jax.nn.sigmoid(gt) * up).astype(BF16)
    acc_scr[...] += jnp.dot(a, wd_ref[...], preferred_element_type=F32)

    @pl.when(j == pl.num_programs(1) - 1)
    def _():
        gate = mod_ref[0, 5:6, :]
        r = DEEPNORM_ALPHA * x_ref[...] + gate * acc_scr[...]
        o_ref[...] = _ln(r) * g_ref[...] + b_ref[...]


def _ffn(x1, mod, wg, wu, wd, ln_g, ln_b, seq):
    t = x1.shape[0]
    tm, tf = TM_FFN, TF_FFN
    nt = seq // tm
    tok = lambda i, j: (i, 0)
    return pl.pallas_call(
        _ffn_kernel,
        grid=(t // tm, D_FF // tf),
        in_specs=[
            pl.BlockSpec((tm, D_MODEL), tok),
            pl.BlockSpec((1, 6, D_MODEL), lambda i, j: (i // nt, 0, 0)),
            pl.BlockSpec((D_MODEL, tf), lambda i, j: (0, j)),
            pl.BlockSpec((D_MODEL, tf), lambda i, j: (0, j)),
            pl.BlockSpec((tf, D_MODEL), lambda i, j: (j, 0)),
            pl.BlockSpec((1, D_MODEL), lambda i, j: (0, 0)),
            pl.BlockSpec((1, D_MODEL), lambda i, j: (0, 0)),
        ],
        out_specs=pl.BlockSpec((tm, D_MODEL), tok),
        out_shape=jax.ShapeDtypeStruct((t, D_MODEL), F32),
        scratch_shapes=[pltpu.VMEM((tm, D_MODEL), BF16), pltpu.VMEM((tm, D_MODEL), F32)],
        compiler_params=pltpu.CompilerParams(
            dimension_semantics=("arbitrary", "arbitrary"), vmem_limit_bytes=VMEM_LIMIT),
        name="ffn",
    )(x1, mod, wg, wu, wd, ln_g, ln_b)


def _rope_tables(seq, dim):
    rows = seq // GRID_W
    quarter = dim // 4
    inv_freq = ROPE_THETA ** (-jnp.arange(quarter, dtype=F32) / quarter)
    row_ang = jnp.arange(rows, dtype=F32)[:, None] * inv_freq
    col_ang = jnp.arange(GRID_W, dtype=F32)[:, None] * inv_freq
    ang = jnp.concatenate([
        jnp.broadcast_to(row_ang[:, None, :], (rows, GRID_W, quarter)),
        jnp.broadcast_to(col_ang[None, :, :], (rows, GRID_W, quarter)),
    ], axis=-1).reshape(seq, 2 * quarter)
    half = dim // 2
    pad = jnp.zeros((seq, LANES // 2 - half), F32)
    cos, sin = jnp.cos(ang), jnp.sin(ang)
    c = jnp.concatenate([cos, pad, cos, pad], axis=-1)
    s = jnp.concatenate([-sin, pad, sin, pad], axis=-1)
    return c, s


def _deinterleave(n):
    return np.concatenate([np.arange(0, n, 2), np.arange(1, n, 2)])


def kernel(x, c, w_ada, b_ada, w_in, b_gates, gqa_q_gain, gqa_k_gain, mla_q_gain, mla_kv_gain,
           w_mla_uq, w_mla_ukv, w_branch_gqa, w_branch_mla, w_out, ln1_g, ln1_b,
           w_ffn_gate, w_ffn_up, w_ffn_down, ln2_g, ln2_b):
    batch, seq, d = x.shape
    assert d == D_MODEL and w_ada.shape[0] == DEPTH
    t = batch * seq
    x2 = x.reshape(t, d)

    cg, sg = _rope_tables(seq, HEAD_DIM)
    cm, sm = _rope_tables(seq, MLA_ROPE_DIM)
    perm_head = _deinterleave(HEAD_DIM)
    zeros32 = lambda rows: jnp.zeros((rows, LANES // 2 - MLA_ROPE_DIM // 2), BF16)

    c_pad = jnp.zeros((8, d), F32).at[:batch].set(c)

    for l in range(DEPTH):
        mod = _adaln(c_pad, w_ada[l], b_ada[l][None, :])[:batch].reshape(batch, 6, d)

        wl = w_in[l]
        o0 = 0
        wq = wl[:, o0:o0 + GQA_Q_COLS].reshape(d, GQA_Q_HEADS, HEAD_DIM)[:, :, perm_head]
        o0 += GQA_Q_COLS
        wk = wl[:, o0:o0 + GQA_KV_COLS].reshape(d, GQA_KV_HEADS, HEAD_DIM)[:, :, perm_head]
        o0 += GQA_KV_COLS
        wv = wl[:, o0:o0 + GQA_KV_COLS]
        o0 += GQA_KV_COLS
        wql = wl[:, o0:o0 + MLA_Q_RANK]
        o0 += MLA_Q_RANK
        wkvl = wl[:, o0:o0 + MLA_KV_RANK]
        o0 += MLA_KV_RANK
        wkr = wl[:, o0:o0 + MLA_ROPE_DIM].astype(BF16)
        o0 += MLA_ROPE_DIM
        w_gate = wl[:, o0:].astype(BF16)
        wkr_pad = jnp.concatenate(
            [wkr[:, 0::2], zeros32(d), wkr[:, 1::2], zeros32(d)], axis=-1)
        w_attn = jnp.concatenate([
            wq.reshape(d, GQA_Q_COLS).astype(BF16), wk.reshape(d, GQA_KV_COLS).astype(BF16),
            wv.astype(BF16), wql.astype(BF16), wkvl.astype(BF16), wkr_pad], axis=-1)

        uq = w_mla_uq[l].astype(BF16).reshape(MLA_Q_RANK, MLA_HEADS, MLA_NOPE_DIM + MLA_ROPE_DIM)
        uq_r = uq[:, :, MLA_NOPE_DIM:]
        z = jnp.zeros((MLA_Q_RANK, MLA_HEADS, LANES // 2 - MLA_ROPE_DIM // 2), BF16)
        wuq = jnp.concatenate(
            [uq[:, :, :MLA_NOPE_DIM], uq_r[:, :, 0::2], z, uq_r[:, :, 1::2], z],
            axis=-1).reshape(MLA_Q_RANK, MLA_HEADS * MLA_QK_PAD)
        ukv = w_mla_ukv[l].astype(BF16).reshape(MLA_KV_RANK, MLA_HEADS, MLA_NOPE_DIM + MLA_V_DIM)
        wuk = ukv[:, :, :MLA_NOPE_DIM].reshape(MLA_KV_RANK, MLA_HEADS * MLA_NOPE_DIM)
        wuv = ukv[:, :, MLA_NOPE_DIM:].reshape(MLA_KV_RANK, MLA_HEADS * MLA_V_DIM)

        gq = (gqa_q_gain[l][perm_head] * (HEAD_DIM ** -0.5))[None, :]
        gk = gqa_k_gain[l][perm_head][None, :]

        qg, kg, vgt, qm, km, vmt = _inproj(
            x2, mod, cg, sg, cm, sm, w_attn, gq, gk, mla_q_gain[l][None, :],
            mla_kv_gain[l][None, :], wuq, wuk, wuv, batch, seq)

        y_gqa = _attention(
            qg.reshape(batch, seq, GQA_Q_COLS), kg.reshape(batch, seq, GQA_KV_COLS), vgt,
            q_heads=GQA_Q_HEADS, kv_heads=GQA_KV_HEADS, dq=HEAD_DIM, dv=HEAD_DIM, tq=TQ_GQA)
        y_mla = _attention(
            qm.reshape(batch, seq, MLA_HEADS * MLA_QK_PAD),
            km.reshape(batch, seq, MLA_HEADS * MLA_QK_PAD), vmt,
            q_heads=MLA_HEADS, kv_heads=MLA_HEADS, dq=MLA_QK_PAD, dv=MLA_V_DIM, tq=TQ_MLA)

        x2 = _merge(
            x2, mod, y_gqa.reshape(t, GQA_Q_COLS), y_mla.reshape(t, MLA_HEADS * MLA_V_DIM),
            w_gate, b_gates[l][None, :], w_branch_gqa[l].astype(BF16),
            w_branch_mla[l].astype(BF16), w_out[l].astype(BF16),
            ln1_g[l][None, :], ln1_b[l][None, :], seq)

        x2 = _ffn(
            x2, mod, w_ffn_gate[l].astype(BF16), w_ffn_up[l].astype(BF16),
            w_ffn_down[l].astype(BF16), ln2_g[l][None, :], ln2_b[l][None, :], seq)

    return x2.reshape(batch, seq, d)
```

```python
import functools

import jax
import jax.numpy as jnp
import numpy as np
from jax import lax
from jax.experimental import pallas as pl
from jax.experimental.pallas import tpu as pltpu

D_MODEL = 2048
GRID_W = 64
ROPE_THETA = 10000.0
HEAD_DIM = 128
GQA_Q_HEADS = 8
GQA_KV_HEADS = 2
MLA_HEADS = 8
MLA_Q_RANK = 512
MLA_KV_RANK = 512
MLA_NOPE_DIM = 128
MLA_ROPE_DIM = 64
MLA_V_DIM = 128
D_FF = 5632
DEPTH = 1
DEEPNORM_ALPHA = (2.0 * DEPTH) ** 0.25
LN_EPS = 1e-5
RMS_EPS = 1e-6

GQA_Q_COLS = GQA_Q_HEADS * HEAD_DIM
GQA_KV_COLS = GQA_KV_HEADS * HEAD_DIM
MLA_QK_PAD = 256
LANES = 128
LOG2E = 1.4426950408889634

VMEM_LIMIT = 56 * 1024 * 1024

TM_IN = 256
TV_CHUNK = TM_IN
TK_ATT = 512
ATT_STEPS = 4
SCORE_ROWS = 256
EXP_ROWS = 64
TQ_GQA = 128
TQ_MLA = 512
TM_MERGE = 512
TN_MERGE = 512
TM_FFN = 512
TF_FFN = 512

BF16 = jnp.bfloat16
F32 = jnp.float32


def _const_spec(shape):
    nd = len(shape)
    return pl.BlockSpec(shape, lambda *_: (0,) * nd, pipeline_mode=pl.Buffered(1))


def _ln(x):
    mu = jnp.mean(x, axis=-1, keepdims=True)
    xc = x - mu
    var = jnp.mean(xc * xc, axis=-1, keepdims=True)
    return xc * lax.rsqrt(var + LN_EPS)


def _rms(x, gain):
    ms = jnp.mean(x * x, axis=-1, keepdims=True)
    return x * lax.rsqrt(ms + RMS_EPS) * gain


def _rope(x, c, s):
    return x * c + pltpu.roll(x, LANES // 2, 1) * s


def _adaln_kernel(c_ref, w_ref, b_ref, o_ref):
    c = c_ref[...]
    act = (c * jax.nn.sigmoid(c)).astype(BF16)
    o_ref[...] = jnp.dot(act, w_ref[...].astype(BF16), preferred_element_type=F32) + b_ref[...]


def _adaln(c_pad, w_ada, b_ada):
    n = w_ada.shape[1]
    tn = 1024
    return pl.pallas_call(
        _adaln_kernel,
        grid=(n // tn,),
        in_specs=[
            pl.BlockSpec((8, D_MODEL), lambda j: (0, 0)),
            pl.BlockSpec((D_MODEL, tn), lambda j: (0, j)),
            pl.BlockSpec((1, tn), lambda j: (0, j)),
        ],
        out_specs=pl.BlockSpec((8, tn), lambda j: (0, j)),
        out_shape=jax.ShapeDtypeStruct((8, n), F32),
        compiler_params=pltpu.CompilerParams(
            dimension_semantics=("arbitrary",), vmem_limit_bytes=VMEM_LIMIT),
        name="adaln",
    )(c_pad, w_ada, b_ada)


def _inproj_kernel(x_ref, mod_ref, cg_ref, sg_ref, cm_ref, sm_ref, w_ref,
                   gq_ref, gk_ref, gql_ref, gkvl_ref, wuq_ref, wuk_ref, wuv_ref,
                   qg_ref, kg_ref, vgt_ref, qm_ref, km_ref, vmt_ref, *, mla_scale):
    x = x_ref[...]
    shift = mod_ref[0, 0:1, :]
    scale = mod_ref[0, 1:2, :]
    h = (_ln(x) * (1.0 + scale) + shift).astype(BF16)
    proj = jnp.dot(h, w_ref[...], preferred_element_type=F32)

    cg = cg_ref[...]
    sg = sg_ref[...]
    cm = cm_ref[...]
    sm = sm_ref[...]

    for hd in range(GQA_Q_HEADS):
        q = proj[:, hd * HEAD_DIM:(hd + 1) * HEAD_DIM]
        q = _rope(_rms(q, gq_ref[...]), cg, sg)
        qg_ref[:, hd * HEAD_DIM:(hd + 1) * HEAD_DIM] = q.astype(BF16)
    off = GQA_Q_COLS
    for hd in range(GQA_KV_HEADS):
        k = proj[:, off + hd * HEAD_DIM: off + (hd + 1) * HEAD_DIM]
        k = _rope(_rms(k, gk_ref[...]), cg, sg)
        kg_ref[:, hd * HEAD_DIM:(hd + 1) * HEAD_DIM] = k.astype(BF16)
    off += GQA_KV_COLS
    for hd in range(GQA_KV_HEADS):
        v = proj[:, off + hd * HEAD_DIM: off + (hd + 1) * HEAD_DIM]
        vgt_ref[0, hd, 0] = v.T.astype(BF16)
    off += GQA_KV_COLS

    q_lat = _rms(proj[:, off: off + MLA_Q_RANK], gql_ref[...]).astype(BF16)
    off += MLA_Q_RANK
    kv_lat = _rms(proj[:, off: off + MLA_KV_RANK], gkvl_ref[...]).astype(BF16)
    off += MLA_KV_RANK
    k_rope = _rope(proj[:, off: off + LANES], cm, sm).astype(BF16)

    q_m = jnp.dot(q_lat, wuq_ref[...], preferred_element_type=F32) * mla_scale
    k_n = jnp.dot(kv_lat, wuk_ref[...], preferred_element_type=F32)
    v_m = jnp.dot(kv_lat, wuv_ref[...], preferred_element_type=F32)
    for hd in range(MLA_HEADS):
        b0 = hd * MLA_QK_PAD
        qm_ref[:, b0: b0 + LANES] = q_m[:, b0: b0 + LANES].astype(BF16)
        qm_ref[:, b0 + LANES: b0 + 2 * LANES] = _rope(
            q_m[:, b0 + LANES: b0 + 2 * LANES], cm, sm).astype(BF16)
        km_ref[:, b0: b0 + LANES] = k_n[:, hd * LANES:(hd + 1) * LANES].astype(BF16)
        km_ref[:, b0 + LANES: b0 + 2 * LANES] = k_rope
        vmt_ref[0, hd, 0] = v_m[:, hd * MLA_V_DIM:(hd + 1) * MLA_V_DIM].T.astype(BF16)


def _inproj(x2, mod, cg, sg, cm, sm, w_attn, gq, gk, gql, gkvl, wuq, wuk, wuv, batch, seq):
    t = x2.shape[0]
    tm = TM_IN
    nt = seq // tm
    n_attn = w_attn.shape[1]
    kern = functools.partial(
        _inproj_kernel, mla_scale=float(LOG2E * (MLA_NOPE_DIM + MLA_ROPE_DIM) ** -0.5))
    tok = lambda i: (i, 0)
    pos = lambda i: (i % nt, 0)
    in_specs = [
        pl.BlockSpec((tm, D_MODEL), tok),
        pl.BlockSpec((1, 6, D_MODEL), lambda i: (i // nt, 0, 0)),
        pl.BlockSpec((tm, LANES), pos),
        pl.BlockSpec((tm, LANES), pos),
        pl.BlockSpec((tm, LANES), pos),
        pl.BlockSpec((tm, LANES), pos),
        _const_spec((D_MODEL, n_attn)),
        _const_spec((1, HEAD_DIM)),
        _const_spec((1, HEAD_DIM)),
        _const_spec((1, MLA_Q_RANK)),
        _const_spec((1, MLA_KV_RANK)),
        _const_spec(wuq.shape),
        _const_spec(wuk.shape),
        _const_spec(wuv.shape),
    ]
    out_shape = [
        jax.ShapeDtypeStruct((t, GQA_Q_COLS), BF16),
        jax.ShapeDtypeStruct((t, GQA_KV_COLS), BF16),
        jax.ShapeDtypeStruct((batch, GQA_KV_HEADS, seq // TV_CHUNK, HEAD_DIM, TV_CHUNK), BF16),
        jax.ShapeDtypeStruct((t, MLA_HEADS * MLA_QK_PAD), BF16),
        jax.ShapeDtypeStruct((t, MLA_HEADS * MLA_QK_PAD), BF16),
        jax.ShapeDtypeStruct((batch, MLA_HEADS, seq // TV_CHUNK, MLA_V_DIM, TV_CHUNK), BF16),
    ]
    vt_map = lambda i: (i // nt, 0, i % nt, 0, 0)
    out_specs = [
        pl.BlockSpec((tm, GQA_Q_COLS), tok),
        pl.BlockSpec((tm, GQA_KV_COLS), tok),
        pl.BlockSpec((1, GQA_KV_HEADS, 1, HEAD_DIM, TV_CHUNK), vt_map),
        pl.BlockSpec((tm, MLA_HEADS * MLA_QK_PAD), tok),
        pl.BlockSpec((tm, MLA_HEADS * MLA_QK_PAD), tok),
        pl.BlockSpec((1, MLA_HEADS, 1, MLA_V_DIM, TV_CHUNK), vt_map),
    ]
    return pl.pallas_call(
        kern,
        grid=(t // tm,),
        in_specs=in_specs,
        out_specs=out_specs,
        out_shape=out_shape,
        compiler_params=pltpu.CompilerParams(
            dimension_semantics=("arbitrary",), vmem_limit_bytes=VMEM_LIMIT),
        name="inproj",
    )(x2, mod, cg, sg, cm, sm, w_attn, gq, gk, gql, gkvl, wuq, wuk, wuv)


def _attn_kernel(q_ref, k_ref, vt_ref, o_ref, qt_scr, st_scr, pb_scr, acc_scr,
                 *, groups, dq, dv, tq, tk, nk, tv, steps):
    mq = groups * tq
    nv = tk // tv
    for g in range(groups):
        qg = q_ref[0, :, g * dq:(g + 1) * dq].astype(F32)
        qt_scr[:, g * tq:(g + 1) * tq] = qg.T.astype(BF16)

    def scores(c, st_ref):
        cmax = None
        for r in range(tk // SCORE_ROWS):
            start = pl.multiple_of(c * tk + r * SCORE_ROWS, SCORE_ROWS)
            st = jnp.dot(k_ref[0, pl.ds(start, SCORE_ROWS), :], qt_scr[...],
                         preferred_element_type=F32)
            st_ref[r * SCORE_ROWS:(r + 1) * SCORE_ROWS, :] = st
            bmax = jnp.max(st, axis=0, keepdims=True)
            cmax = bmax if cmax is None else jnp.maximum(cmax, bmax)
        return cmax

    def values(c, pb_ref, alpha):
        vt = jnp.concatenate([vt_ref[0, 0, c * nv + j] for j in range(nv)], axis=1)
        acc_scr[...] = alpha * acc_scr[...] + jnp.dot(
            vt, pb_ref[...], preferred_element_type=F32)

    def softmax(st_ref, pb_ref, cmax, m_prev, l_prev):
        m_new = jnp.maximum(m_prev, cmax)
        alpha = jnp.exp2(m_prev - m_new)
        l_new = alpha * l_prev
        for r in range(tk // EXP_ROWS):
            rows = slice(r * EXP_ROWS, (r + 1) * EXP_ROWS)
            p = jnp.exp2(st_ref[rows, :] - m_new)
            pb_ref[rows, :] = p.astype(BF16)
            l_new = l_new + jnp.sum(p, axis=0, keepdims=True)
        return m_new, alpha, l_new

    def step(c, cur, nxt, carry, first=False, last=False):
        cmax, m_prev, l_prev, alpha_prev = carry
        cmax_next = cmax if last else scores(c + 1, st_scr.at[nxt])
        if not first:
            values(c - 1, pb_scr.at[nxt], alpha_prev)
        m_new, alpha, l_new = softmax(st_scr.at[cur], pb_scr.at[cur], cmax, m_prev, l_prev)
        return cmax_next, m_new, l_new, alpha

    acc_scr[...] = jnp.zeros(acc_scr.shape, F32)
    carry = (scores(0, st_scr.at[0]), jnp.full((1, mq), -jnp.inf, F32), jnp.zeros((1, mq), F32),
             jnp.ones((1, mq), F32))
    carry = step(0, 0, 1, carry, first=True)

    n_loop = (nk - 2) // steps

    def body(i, carry):
        c0 = steps * i + 1
        for s in range(steps):
            carry = step(c0 + s, (1 + s) % 2, s % 2, carry)
        return carry

    if n_loop > 0:
        carry = lax.fori_loop(0, n_loop, body, carry)
    for c in range(n_loop * steps + 1, nk - 1):
        carry = step(c, c % 2, (c + 1) % 2, carry)
    _, _, l_fin, alpha = step(nk - 1, (nk - 1) % 2, nk % 2, carry, last=True)
    values(nk - 1, pb_scr.at[(nk - 1) % 2], alpha)

    out_t = acc_scr[...] / l_fin
    for g in range(groups):
        o_ref[0, :, g * dv:(g + 1) * dv] = out_t[:, g * tq:(g + 1) * tq].T.astype(o_ref.dtype)


def _attention(q, k, vt, *, q_heads, kv_heads, dq, dv, tq):
    batch, seq, _ = q.shape
    groups = q_heads // kv_heads
    tk = TK_ATT
    tv = TV_CHUNK
    nk = seq // tk
    mq = groups * tq
    kern = functools.partial(_attn_kernel, groups=groups, dq=dq, dv=dv, tq=tq, tk=tk, nk=nk,
                             tv=tv, steps=ATT_STEPS)
    return pl.pallas_call(
        kern,
        grid=(batch, kv_heads, seq // tq),
        in_specs=[
            pl.BlockSpec((1, tq, groups * dq), lambda b, h, i: (b, i, h)),
            pl.BlockSpec((1, seq, dq), lambda b, h, i: (b, 0, h)),
            pl.BlockSpec((1, 1, seq // tv, dv, tv), lambda b, h, i: (b, h, 0, 0, 0)),
        ],
        out_specs=pl.BlockSpec((1, tq, groups * dv), lambda b, h, i: (b, i, h)),
        out_shape=jax.ShapeDtypeStruct((batch, seq, q_heads * dv), BF16),
        scratch_shapes=[
            pltpu.VMEM((dq, mq), BF16),
            pltpu.VMEM((2, tk, mq), F32),
            pltpu.VMEM((2, tk, mq), BF16),
            pltpu.VMEM((dv, mq), F32),
        ],
        compiler_params=pltpu.CompilerParams(
            dimension_semantics=("arbitrary", "arbitrary", "arbitrary"),
            vmem_limit_bytes=VMEM_LIMIT),
        name=f"attn_g{groups}",
    )(q, k, vt)


def _merge_kernel(x_ref, mod_ref, yg_ref, ym_ref, wga_ref, wgb_ref, bga_ref, bgb_ref,
                  wbg_ref, wbm_ref, wo_ref, g_ref, b_ref, o_ref, h_scr, acc_scr):
    j = pl.program_id(1)

    @pl.when(j == 0)
    def _():
        shift = mod_ref[0, 0:1, :]
        scale = mod_ref[0, 1:2, :]
        h_scr[...] = (_ln(x_ref[...]) * (1.0 + scale) + shift).astype(BF16)
        acc_scr[...] = jnp.zeros(acc_scr.shape, F32)

    h = h_scr[...]
    la = jnp.dot(h, wga_ref[...], preferred_element_type=F32) + bga_ref[...]
    lb = jnp.dot(h, wgb_ref[...], preferred_element_type=F32) + bgb_ref[...]
    a = jnp.dot(yg_ref[...], wbg_ref[...], preferred_element_type=F32)
    b = jnp.dot(ym_ref[...], wbm_ref[...], preferred_element_type=F32)
    merged = jax.nn.sigmoid(la) * a + jax.nn.sigmoid(lb) * b
    acc_scr[...] += jnp.dot(merged.astype(BF16), wo_ref[...], preferred_element_type=F32)

    @pl.when(j == pl.num_programs(1) - 1)
    def _():
        gate = mod_ref[0, 2:3, :]
        r = DEEPNORM_ALPHA * x_ref[...] + gate * acc_scr[...]
        o_ref[...] = _ln(r) * g_ref[...] + b_ref[...]


def _merge(x2, mod, yg, ym, w_gate, b_gate, wbg, wbm, wo, ln_g, ln_b, seq):
    t = x2.shape[0]
    tm, tn = TM_MERGE, TN_MERGE
    nt = seq // tm
    nj = D_MODEL // tn
    tok = lambda i, j: (i, 0)
    return pl.pallas_call(
        _merge_kernel,
        grid=(t // tm, nj),
        in_specs=[
            pl.BlockSpec((tm, D_MODEL), tok),
            pl.BlockSpec((1, 6, D_MODEL), lambda i, j: (i // nt, 0, 0)),
            pl.BlockSpec((tm, yg.shape[1]), tok),
            pl.BlockSpec((tm, ym.shape[1]), tok),
            pl.BlockSpec((D_MODEL, tn), lambda i, j: (0, j)),
            pl.BlockSpec((D_MODEL, tn), lambda i, j: (0, j + nj)),
            pl.BlockSpec((1, tn), lambda i, j: (0, j)),
            pl.BlockSpec((1, tn), lambda i, j: (0, j + nj)),
            pl.BlockSpec((wbg.shape[0], tn), lambda i, j: (0, j)),
            pl.BlockSpec((wbm.shape[0], tn), lambda i, j: (0, j)),
            pl.BlockSpec((tn, D_MODEL), lambda i, j: (j, 0)),
            pl.BlockSpec((1, D_MODEL), lambda i, j: (0, 0)),
            pl.BlockSpec((1, D_MODEL), lambda i, j: (0, 0)),
        ],
        out_specs=pl.BlockSpec((tm, D_MODEL), tok),
        out_shape=jax.ShapeDtypeStruct((t, D_MODEL), F32),
        scratch_shapes=[pltpu.VMEM((tm, D_MODEL), BF16), pltpu.VMEM((tm, D_MODEL), F32)],
        compiler_params=pltpu.CompilerParams(
            dimension_semantics=("arbitrary", "arbitrary"), vmem_limit_bytes=VMEM_LIMIT),
        name="merge",
    )(x2, mod, yg, ym, w_gate, w_gate, b_gate, b_gate, wbg, wbm, wo, ln_g, ln_b)


def _ffn_kernel(x_ref, mod_ref, wg_ref, wu_ref, wd_ref, g_ref, b_ref, o_ref, h_scr, acc_scr):
    j = pl.program_id(1)

    @pl.when(j == 0)
    def _():
        shift = mod_ref[0, 3:4, :]
        scale = mod_ref[0, 4:5, :]
        h_scr[...] = (_ln(x_ref[...]) * (1.0 + scale) + shift).astype(BF16)
        acc_scr[...] = jnp.zeros(acc_scr.shape, F32)

    h = h_scr[...]
    gt = jnp.dot(h, wg_ref[...], preferred_element_type=F32)
    up = jnp.dot(h, wu_ref[...], preferred_element_type=F32)
    a = (gt * jax.nn.sigmoid(gt) * up).astype(BF16)
    acc_scr[...] += jnp.dot(a, wd_ref[...], preferred_element_type=F32)

    @pl.when(j == pl.num_programs(1) - 1)
    def _():
        gate = mod_ref[0, 5:6, :]
        r = DEEPNORM_ALPHA * x_ref[...] + gate * acc_scr[...]
        o_ref[...] = _ln(r) * g_ref[...] + b_ref[...]


def _ffn(x1, mod, wg, wu, wd, ln_g, ln_b, seq):
    t = x1.shape[0]
    tm, tf = TM_FFN, TF_FFN
    nt = seq // tm
    tok = lambda i, j: (i, 0)
    return pl.pallas_call(
        _ffn_kernel,
        grid=(t // tm, D_FF // tf),
        in_specs=[
            pl.BlockSpec((tm, D_MODEL), tok),
            pl.BlockSpec((1, 6, D_MODEL), lambda i, j: (i // nt, 0, 0)),
            pl.BlockSpec((D_MODEL, tf), lambda i, j: (0, j)),
            pl.BlockSpec((D_MODEL, tf), lambda i, j: (0, j)),
            pl.BlockSpec((tf, D_MODEL), lambda i, j: (j, 0)),
            pl.BlockSpec((1, D_MODEL), lambda i, j: (0, 0)),
            pl.BlockSpec((1, D_MODEL), lambda i, j: (0, 0)),
        ],
        out_specs=pl.BlockSpec((tm, D_MODEL), tok),
        out_shape=jax.ShapeDtypeStruct((t, D_MODEL), F32),
        scratch_shapes=[pltpu.VMEM((tm, D_MODEL), BF16), pltpu.VMEM((tm, D_MODEL), F32)],
        compiler_params=pltpu.CompilerParams(
            dimension_semantics=("arbitrary", "arbitrary"), vmem_limit_bytes=VMEM_LIMIT),
        name="ffn",
    )(x1, mod, wg, wu, wd, ln_g, ln_b)


def _rope_tables(seq, dim):
    rows = seq // GRID_W
    quarter = dim // 4
    inv_freq = ROPE_THETA ** (-jnp.arange(quarter, dtype=F32) / quarter)
    row_ang = jnp.arange(rows, dtype=F32)[:, None] * inv_freq
    col_ang = jnp.arange(GRID_W, dtype=F32)[:, None] * inv_freq
    ang = jnp.concatenate([
        jnp.broadcast_to(row_ang[:, None, :], (rows, GRID_W, quarter)),
        jnp.broadcast_to(col_ang[None, :, :], (rows, GRID_W, quarter)),
    ], axis=-1).reshape(seq, 2 * quarter)
    half = dim // 2
    pad = jnp.zeros((seq, LANES // 2 - half), F32)
    cos, sin = jnp.cos(ang), jnp.sin(ang)
    c = jnp.concatenate([cos, pad, cos, pad], axis=-1)
    s = jnp.concatenate([-sin, pad, sin, pad], axis=-1)
    return c, s


def _deinterleave(n):
    return np.concatenate([np.arange(0, n, 2), np.arange(1, n, 2)])


def kernel(x, c, w_ada, b_ada, w_in, b_gates, gqa_q_gain, gqa_k_gain, mla_q_gain, mla_kv_gain,
           w_mla_uq, w_mla_ukv, w_branch_gqa, w_branch_mla, w_out, ln1_g, ln1_b,
           w_ffn_gate, w_ffn_up, w_ffn_down, ln2_g, ln2_b):
    batch, seq, d = x.shape
    assert d == D_MODEL and w_ada.shape[0] == DEPTH
    t = batch * seq
    x2 = x.reshape(t, d)

    cg, sg = _rope_tables(seq, HEAD_DIM)
    cm, sm = _rope_tables(seq, MLA_ROPE_DIM)
    perm_head = _deinterleave(HEAD_DIM)
    zeros32 = lambda rows: jnp.zeros((rows, LANES // 2 - MLA_ROPE_DIM // 2), BF16)

    c_pad = jnp.zeros((8, d), F32).at[:batch].set(c)

    for l in range(DEPTH):
        mod = _adaln(c_pad, w_ada[l], b_ada[l][None, :])[:batch].reshape(batch, 6, d)

        wl = w_in[l]
        o0 = 0
        wq = wl[:, o0:o0 + GQA_Q_COLS].reshape(d, GQA_Q_HEADS, HEAD_DIM)[:, :, perm_head]
        o0 += GQA_Q_COLS
        wk = wl[:, o0:o0 + GQA_KV_COLS].reshape(d, GQA_KV_HEADS, HEAD_DIM)[:, :, perm_head]
        o0 += GQA_KV_COLS
        wv = wl[:, o0:o0 + GQA_KV_COLS]
        o0 += GQA_KV_COLS
        wql = wl[:, o0:o0 + MLA_Q_RANK]
        o0 += MLA_Q_RANK
        wkvl = wl[:, o0:o0 + MLA_KV_RANK]
        o0 += MLA_KV_RANK
        wkr = wl[:, o0:o0 + MLA_ROPE_DIM].astype(BF16)
        o0 += MLA_ROPE_DIM
        w_gate = wl[:, o0:].astype(BF16)
        wkr_pad = jnp.concatenate(
            [wkr[:, 0::2], zeros32(d), wkr[:, 1::2], zeros32(d)], axis=-1)
        w_attn = jnp.concatenate([
            wq.reshape(d, GQA_Q_COLS).astype(BF16), wk.reshape(d, GQA_KV_COLS).astype(BF16),
            wv.astype(BF16), wql.astype(BF16), wkvl.astype(BF16), wkr_pad], axis=-1)

        uq = w_mla_uq[l].astype(BF16).reshape(MLA_Q_RANK, MLA_HEADS, MLA_NOPE_DIM + MLA_ROPE_DIM)
        uq_r = uq[:, :, MLA_NOPE_DIM:]
        z = jnp.zeros((MLA_Q_RANK, MLA_HEADS, LANES // 2 - MLA_ROPE_DIM // 2), BF16)
        wuq = jnp.concatenate(
            [uq[:, :, :MLA_NOPE_DIM], uq_r[:, :, 0::2], z, uq_r[:, :, 1::2], z],
            axis=-1).reshape(MLA_Q_RANK, MLA_HEADS * MLA_QK_PAD)
        ukv = w_mla_ukv[l].astype(BF16).reshape(MLA_KV_RANK, MLA_HEADS, MLA_NOPE_DIM + MLA_V_DIM)
        wuk = ukv[:, :, :MLA_NOPE_DIM].reshape(MLA_KV_RANK, MLA_HEADS * MLA_NOPE_DIM)
        wuv = ukv[:, :, MLA_NOPE_DIM:].reshape(MLA_KV_RANK, MLA_HEADS * MLA_V_DIM)

        gq = (gqa_q_gain[l][perm_head] * (LOG2E * HEAD_DIM ** -0.5))[None, :]
        gk = gqa_k_gain[l][perm_head][None, :]

        qg, kg, vgt, qm, km, vmt = _inproj(
            x2, mod, cg, sg, cm, sm, w_attn, gq, gk, mla_q_gain[l][None, :],
            mla_kv_gain[l][None, :], wuq, wuk, wuv, batch, seq)

        y_gqa = _attention(
            qg.reshape(batch, seq, GQA_Q_COLS), kg.reshape(batch, seq, GQA_KV_COLS), vgt,
            q_heads=GQA_Q_HEADS, kv_heads=GQA_KV_HEADS, dq=HEAD_DIM, dv=HEAD_DIM, tq=TQ_GQA)
        y_mla = _attention(
            qm.reshape(batch, seq, MLA_HEADS * MLA_QK_PAD),
            km.reshape(batch, seq, MLA_HEADS * MLA_QK_PAD), vmt,
            q_heads=MLA_HEADS, kv_heads=MLA_HEADS, dq=MLA_QK_PAD, dv=MLA_V_DIM, tq=TQ_MLA)

        x2 = _merge(
            x2, mod, y_gqa.reshape(t, GQA_Q_COLS), y_mla.reshape(t, MLA_HEADS * MLA_V_DIM),
            w_gate, b_gates[l][None, :], w_branch_gqa[l].astype(BF16),
            w_branch_mla[l].astype(BF16), w_out[l].astype(BF16),
            ln1_g[l][None, :], ln1_b[l][None, :], seq)

        x2 = _ffn(
            x2, mod, w_ffn_gate[l].astype(BF16), w_ffn_up[l].astype(BF16),
            w_ffn_down[l].astype(BF16), ln2_g[l][None, :], ln2_b[l][None, :], seq)

    return x2.reshape(batch, seq, d)
```

```python
import functools

import jax
import jax.numpy as jnp
import numpy as np
from jax import lax
from jax.experimental import pallas as pl
from jax.experimental.pallas import tpu as pltpu

D_MODEL = 2048
GRID_W = 64
ROPE_THETA = 10000.0
HEAD_DIM = 128
GQA_Q_HEADS = 8
GQA_KV_HEADS = 2
MLA_HEADS = 8
MLA_Q_RANK = 512
MLA_KV_RANK = 512
MLA_NOPE_DIM = 128
MLA_ROPE_DIM = 64
MLA_V_DIM = 128
D_FF = 5632
DEPTH = 1
DEEPNORM_ALPHA = (2.0 * DEPTH) ** 0.25
LN_EPS = 1e-5
RMS_EPS = 1e-6

GQA_Q_COLS = GQA_Q_HEADS * HEAD_DIM
GQA_KV_COLS = GQA_KV_HEADS * HEAD_DIM
MLA_QK_PAD = 256
LANES = 128
LOG2E = 1.4426950408889634

VMEM_LIMIT = 56 * 1024 * 1024

TM_IN = 256
TV_CHUNK = TM_IN
TK_ATT = 256
ATT_STEPS = 6
SCORE_ROWS = 256
SUM_ROWS = 16
EXP_ROWS = 64
TQ_GQA = 256
TQ_MLA = 1024
TM_MERGE = 512
TN_MERGE = 512
TM_FFN = 512
TF_FFN = 512

BF16 = jnp.bfloat16
F32 = jnp.float32


def _const_spec(shape):
    nd = len(shape)
    return pl.BlockSpec(shape, lambda *_: (0,) * nd, pipeline_mode=pl.Buffered(1))


def _ln(x):
    mu = jnp.mean(x, axis=-1, keepdims=True)
    xc = x - mu
    var = jnp.mean(xc * xc, axis=-1, keepdims=True)
    return xc * lax.rsqrt(var + LN_EPS)


def _rms(x, gain):
    ms = jnp.mean(x * x, axis=-1, keepdims=True)
    return x * lax.rsqrt(ms + RMS_EPS) * gain


def _rope(x, c, s):
    return x * c + pltpu.roll(x, LANES // 2, 1) * s


def _adaln_kernel(c_ref, w_ref, b_ref, o_ref):
    c = c_ref[...]
    act = (c * jax.nn.sigmoid(c)).astype(BF16)
    o_ref[...] = jnp.dot(act, w_ref[...].astype(BF16), preferred_element_type=F32) + b_ref[...]


def _adaln(c_pad, w_ada, b_ada):
    n = w_ada.shape[1]
    tn = 1024
    return pl.pallas_call(
        _adaln_kernel,
        grid=(n // tn,),
        in_specs=[
            pl.BlockSpec((8, D_MODEL), lambda j: (0, 0)),
            pl.BlockSpec((D_MODEL, tn), lambda j: (0, j)),
            pl.BlockSpec((1, tn), lambda j: (0, j)),
        ],
        out_specs=pl.BlockSpec((8, tn), lambda j: (0, j)),
        out_shape=jax.ShapeDtypeStruct((8, n), F32),
        compiler_params=pltpu.CompilerParams(
            dimension_semantics=("arbitrary",), vmem_limit_bytes=VMEM_LIMIT),
        name="adaln",
    )(c_pad, w_ada, b_ada)


def _inproj_kernel(x_ref, mod_ref, cg_ref, sg_ref, cm_ref, sm_ref, w_ref,
                   gq_ref, gk_ref, gql_ref, gkvl_ref, wuq_ref, wuk_ref, wuv_ref,
                   qg_ref, kg_ref, vgt_ref, qm_ref, km_ref, vmt_ref, *, mla_scale):
    x = x_ref[...]
    shift = mod_ref[0, 0:1, :]
    scale = mod_ref[0, 1:2, :]
    h = (_ln(x) * (1.0 + scale) + shift).astype(BF16)
    proj = jnp.dot(h, w_ref[...], preferred_element_type=F32)

    cg = cg_ref[...]
    sg = sg_ref[...]
    cm = cm_ref[...]
    sm = sm_ref[...]

    for hd in range(GQA_Q_HEADS):
        q = proj[:, hd * HEAD_DIM:(hd + 1) * HEAD_DIM]
        q = _rope(_rms(q, gq_ref[...]), cg, sg)
        qg_ref[:, hd * HEAD_DIM:(hd + 1) * HEAD_DIM] = q.astype(BF16)
    off = GQA_Q_COLS
    for hd in range(GQA_KV_HEADS):
        k = proj[:, off + hd * HEAD_DIM: off + (hd + 1) * HEAD_DIM]
        k = _rope(_rms(k, gk_ref[...]), cg, sg)
        kg_ref[:, hd * HEAD_DIM:(hd + 1) * HEAD_DIM] = k.astype(BF16)
    off += GQA_KV_COLS
    for hd in range(GQA_KV_HEADS):
        v = proj[:, off + hd * HEAD_DIM: off + (hd + 1) * HEAD_DIM]
        vgt_ref[0, hd, 0] = v.T.astype(BF16)
    off += GQA_KV_COLS

    q_lat = _rms(proj[:, off: off + MLA_Q_RANK], gql_ref[...]).astype(BF16)
    off += MLA_Q_RANK
    kv_lat = _rms(proj[:, off: off + MLA_KV_RANK], gkvl_ref[...]).astype(BF16)
    off += MLA_KV_RANK
    k_rope = _rope(proj[:, off: off + LANES], cm, sm).astype(BF16)

    q_m = jnp.dot(q_lat, wuq_ref[...], preferred_element_type=F32) * mla_scale
    k_n = jnp.dot(kv_lat, wuk_ref[...], preferred_element_type=F32)
    v_m = jnp.dot(kv_lat, wuv_ref[...], preferred_element_type=F32)
    for hd in range(MLA_HEADS):
        b0 = hd * MLA_QK_PAD
        qm_ref[:, b0: b0 + LANES] = q_m[:, b0: b0 + LANES].astype(BF16)
        qm_ref[:, b0 + LANES: b0 + 2 * LANES] = _rope(
            q_m[:, b0 + LANES: b0 + 2 * LANES], cm, sm).astype(BF16)
        km_ref[:, b0: b0 + LANES] = k_n[:, hd * LANES:(hd + 1) * LANES].astype(BF16)
        km_ref[:, b0 + LANES: b0 + 2 * LANES] = k_rope
        vmt_ref[0, hd, 0] = v_m[:, hd * MLA_V_DIM:(hd + 1) * MLA_V_DIM].T.astype(BF16)


def _inproj(x2, mod, cg, sg, cm, sm, w_attn, gq, gk, gql, gkvl, wuq, wuk, wuv, batch, seq):
    t = x2.shape[0]
    tm = TM_IN
    nt = seq // tm
    n_attn = w_attn.shape[1]
    kern = functools.partial(
        _inproj_kernel, mla_scale=float(LOG2E * (MLA_NOPE_DIM + MLA_ROPE_DIM) ** -0.5))
    tok = lambda i: (i, 0)
    pos = lambda i: (i % nt, 0)
    in_specs = [
        pl.BlockSpec((tm, D_MODEL), tok),
        pl.BlockSpec((1, 6, D_MODEL), lambda i: (i // nt, 0, 0)),
        pl.BlockSpec((tm, LANES), pos),
        pl.BlockSpec((tm, LANES), pos),
        pl.BlockSpec((tm, LANES), pos),
        pl.BlockSpec((tm, LANES), pos),
        _const_spec((D_MODEL, n_attn)),
        _const_spec((1, HEAD_DIM)),
        _const_spec((1, HEAD_DIM)),
        _const_spec((1, MLA_Q_RANK)),
        _const_spec((1, MLA_KV_RANK)),
        _const_spec(wuq.shape),
        _const_spec(wuk.shape),
        _const_spec(wuv.shape),
    ]
    out_shape = [
        jax.ShapeDtypeStruct((t, GQA_Q_COLS), BF16),
        jax.ShapeDtypeStruct((t, GQA_KV_COLS), BF16),
        jax.ShapeDtypeStruct((batch, GQA_KV_HEADS, seq // TV_CHUNK, HEAD_DIM, TV_CHUNK), BF16),
        jax.ShapeDtypeStruct((t, MLA_HEADS * MLA_QK_PAD), BF16),
        jax.ShapeDtypeStruct((t, MLA_HEADS * MLA_QK_PAD), BF16),
        jax.ShapeDtypeStruct((batch, MLA_HEADS, seq // TV_CHUNK, MLA_V_DIM, TV_CHUNK), BF16),
    ]
    vt_map = lambda i: (i // nt, 0, i % nt, 0, 0)
    out_specs = [
        pl.BlockSpec((tm, GQA_Q_COLS), tok),
        pl.BlockSpec((tm, GQA_KV_COLS), tok),
        pl.BlockSpec((1, GQA_KV_HEADS, 1, HEAD_DIM, TV_CHUNK), vt_map),
        pl.BlockSpec((tm, MLA_HEADS * MLA_QK_PAD), tok),
        pl.BlockSpec((tm, MLA_HEADS * MLA_QK_PAD), tok),
        pl.BlockSpec((1, MLA_HEADS, 1, MLA_V_DIM, TV_CHUNK), vt_map),
    ]
    return pl.pallas_call(
        kern,
        grid=(t // tm,),
        in_specs=in_specs,
        out_specs=out_specs,
        out_shape=out_shape,
        compiler_params=pltpu.CompilerParams(
            dimension_semantics=("arbitrary",), vmem_limit_bytes=VMEM_LIMIT),
        name="inproj",
    )(x2, mod, cg, sg, cm, sm, w_attn, gq, gk, gql, gkvl, wuq, wuk, wuv)


def _attn_kernel(q_ref, k_ref, vt_ref, o_ref, qt_scr, st_scr, pb_scr, acc_scr,
                 *, groups, dq, dv, tq, tk, nk, tv, steps):
    mq = groups * tq
    nv = tk // tv
    for g in range(groups):
        qg = q_ref[0, :, g * dq:(g + 1) * dq].astype(F32)
        qt_scr[:, g * tq:(g + 1) * tq] = qg.T.astype(BF16)

    def scores(c, st_ref):
        cmax = None
        for r in range(tk // SCORE_ROWS):
            start = pl.multiple_of(c * tk + r * SCORE_ROWS, SCORE_ROWS)
            st = jnp.dot(k_ref[0, pl.ds(start, SCORE_ROWS), :], qt_scr[...],
                         preferred_element_type=F32)
            st_ref[r * SCORE_ROWS:(r + 1) * SCORE_ROWS, :] = st
            bmax = jnp.max(st, axis=0, keepdims=True)
            cmax = bmax if cmax is None else jnp.maximum(cmax, bmax)
        return cmax

    ones_rows = jnp.ones((SUM_ROWS, tk), BF16)

    def values(c, pb_ref, alpha):
        vt = jnp.concatenate([vt_ref[0, 0, c * nv + j] for j in range(nv)], axis=1)
        vt = jnp.concatenate([vt, ones_rows], axis=0)
        acc_scr[...] = alpha * acc_scr[...] + jnp.dot(
            vt, pb_ref[...], preferred_element_type=F32)

    def softmax(st_ref, pb_ref, cmax, m_prev):
        m_new = jnp.maximum(m_prev, cmax)
        alpha = jnp.exp2(m_prev - m_new)
        for r in range(tk // EXP_ROWS):
            rows = slice(r * EXP_ROWS, (r + 1) * EXP_ROWS)
            pb_ref[rows, :] = jnp.exp2((st_ref[rows, :] - m_new).astype(BF16))
        return m_new, alpha

    def step(c, cur, nxt, carry, first=False, last=False):
        cmax, m_prev, alpha_prev = carry
        cmax_next = cmax if last else scores(c + 1, st_scr.at[nxt])
        if not first:
            values(c - 1, pb_scr.at[nxt], alpha_prev)
        m_new, alpha = softmax(st_scr.at[cur], pb_scr.at[cur], cmax, m_prev)
        return cmax_next, m_new, alpha

    acc_scr[...] = jnp.zeros(acc_scr.shape, F32)
    carry = (scores(0, st_scr.at[0]), jnp.full((1, mq), -jnp.inf, F32), jnp.ones((1, mq), F32))
    carry = step(0, 0, 1, carry, first=True)

    n_loop = (nk - 2) // steps

    def body(i, carry):
        c0 = steps * i + 1
        for s in range(steps):
            carry = step(c0 + s, (1 + s) % 2, s % 2, carry)
        return carry

    if n_loop == 1:
        carry = body(0, carry)
    elif n_loop > 1:
        carry = lax.fori_loop(0, n_loop, body, carry)
    for c in range(n_loop * steps + 1, nk - 1):
        carry = step(c, c % 2, (c + 1) % 2, carry)
    _, _, alpha = step(nk - 1, (nk - 1) % 2, nk % 2, carry, last=True)
    values(nk - 1, pb_scr.at[(nk - 1) % 2], alpha)

    out_t = acc_scr[0:dv, :] / acc_scr[dv:dv + 1, :]
    for g in range(groups):
        o_ref[0, :, g * dv:(g + 1) * dv] = out_t[:, g * tq:(g + 1) * tq].T.astype(o_ref.dtype)


def _attention(q, k, vt, *, q_heads, kv_heads, dq, dv, tq):
    batch, seq, _ = q.shape
    groups = q_heads // kv_heads
    tk = TK_ATT
    tv = TV_CHUNK
    nk = seq // tk
    mq = groups * tq
    kern = functools.partial(_attn_kernel, groups=groups, dq=dq, dv=dv, tq=tq, tk=tk, nk=nk,
                             tv=tv, steps=ATT_STEPS)
    return pl.pallas_call(
        kern,
        grid=(batch, kv_heads, seq // tq),
        in_specs=[
            pl.BlockSpec((1, tq, groups * dq), lambda b, h, i: (b, i, h)),
            pl.BlockSpec((1, seq, dq), lambda b, h, i: (b, 0, h)),
            pl.BlockSpec((1, 1, seq // tv, dv, tv), lambda b, h, i: (b, h, 0, 0, 0)),
        ],
        out_specs=pl.BlockSpec((1, tq, groups * dv), lambda b, h, i: (b, i, h)),
        out_shape=jax.ShapeDtypeStruct((batch, seq, q_heads * dv), BF16),
        scratch_shapes=[
            pltpu.VMEM((dq, mq), BF16),
            pltpu.VMEM((2, tk, mq), F32),
            pltpu.VMEM((2, tk, mq), BF16),
            pltpu.VMEM((dv + SUM_ROWS, mq), F32),
        ],
        compiler_params=pltpu.CompilerParams(
            dimension_semantics=("arbitrary", "arbitrary", "arbitrary"),
            vmem_limit_bytes=VMEM_LIMIT),
        name=f"attn_g{groups}",
    )(q, k, vt)


def _merge_kernel(x_ref, mod_ref, yg_ref, ym_ref, wga_ref, wgb_ref, bga_ref, bgb_ref,
                  wbg_ref, wbm_ref, wo_ref, g_ref, b_ref, o_ref, h_scr, acc_scr):
    j = pl.program_id(1)

    @pl.when(j == 0)
    def _():
        shift = mod_ref[0, 0:1, :]
        scale = mod_ref[0, 1:2, :]
        h_scr[...] = (_ln(x_ref[...]) * (1.0 + scale) + shift).astype(BF16)
        acc_scr[...] = jnp.zeros(acc_scr.shape, F32)

    h = h_scr[...]
    la = jnp.dot(h, wga_ref[...], preferred_element_type=F32) + bga_ref[...]
    lb = jnp.dot(h, wgb_ref[...], preferred_element_type=F32) + bgb_ref[...]
    a = jnp.dot(yg_ref[...], wbg_ref[...], preferred_element_type=F32)
    b = jnp.dot(ym_ref[...], wbm_ref[...], preferred_element_type=F32)
    merged = jax.nn.sigmoid(la) * a + jax.nn.sigmoid(lb) * b
    acc_scr[...] += jnp.dot(merged.astype(BF16), wo_ref[...], preferred_element_type=F32)

    @pl.when(j == pl.num_programs(1) - 1)
    def _():
        gate = mod_ref[0, 2:3, :]
        r = DEEPNORM_ALPHA * x_ref[...] + gate * acc_scr[...]
        o_ref[...] = _ln(r) * g_ref[...] + b_ref[...]


def _merge(x2, mod, yg, ym, w_gate, b_gate, wbg, wbm, wo, ln_g, ln_b, seq):
    t = x2.shape[0]
    tm, tn = TM_MERGE, TN_MERGE
    nt = seq // tm
    nj = D_MODEL // tn
    tok = lambda i, j: (i, 0)
    return pl.pallas_call(
        _merge_kernel,
        grid=(t // tm, nj),
        in_specs=[
            pl.BlockSpec((tm, D_MODEL), tok),
            pl.BlockSpec((1, 6, D_MODEL), lambda i, j: (i // nt, 0, 0)),
            pl.BlockSpec((tm, yg.shape[1]), tok),
            pl.BlockSpec((tm, ym.shape[1]), tok),
            pl.BlockSpec((D_MODEL, tn), lambda i, j: (0, j)),
            pl.BlockSpec((D_MODEL, tn), lambda i, j: (0, j + nj)),
            pl.BlockSpec((1, tn), lambda i, j: (0, j)),
            pl.BlockSpec((1, tn), lambda i, j: (0, j + nj)),
            pl.BlockSpec((wbg.shape[0], tn), lambda i, j: (0, j)),
            pl.BlockSpec((wbm.shape[0], tn), lambda i, j: (0, j)),
            pl.BlockSpec((tn, D_MODEL), lambda i, j: (j, 0)),
            pl.BlockSpec((1, D_MODEL), lambda i, j: (0, 0)),
            pl.BlockSpec((1, D_MODEL), lambda i, j: (0, 0)),
        ],
        out_specs=pl.BlockSpec((tm, D_MODEL), tok),
        out_shape=jax.ShapeDtypeStruct((t, D_MODEL), F32),
        scratch_shapes=[pltpu.VMEM((tm, D_MODEL), BF16), pltpu.VMEM((tm, D_MODEL), F32)],
        compiler_params=pltpu.CompilerParams(
            dimension_semantics=("arbitrary", "arbitrary"), vmem_limit_bytes=VMEM_LIMIT),
        name="merge",
    )(x2, mod, yg, ym, w_gate, w_gate, b_gate, b_gate, wbg, wbm, wo, ln_g, ln_b)


def _ffn_kernel(x_ref, mod_ref, wg_ref, wu_ref, wd_ref, g_ref, b_ref, o_ref, h_scr, acc_scr):
    j = pl.program_id(1)

    @pl.when(j == 0)
    def _():
        shift = mod_ref[0, 3:4, :]
        scale = mod_ref[0, 4:5, :]
        h_scr[...] = (_ln(x_ref[...]) * (1.0 + scale) + shift).astype(BF16)
        acc_scr[...] = jnp.zeros(acc_scr.shape, F32)

    h = h_scr[...]
    gt = jnp.dot(h, wg_ref[...], preferred_element_type=F32)
    up = jnp.dot(h, wu_ref[...], preferred_element_type=F32)
    a = (gt * jax.nn.sigmoid(gt) * up).astype(BF16)
    acc_scr[...] += jnp.dot(a, wd_ref[...], preferred_element_type=F32)

    @pl.when(j == pl.num_programs(1) - 1)
    def _():
        gate = mod_ref[0, 5:6, :]
        r = DEEPNORM_ALPHA * x_ref[...] + gate * acc_scr[...]
        o_ref[...] = _ln(r) * g_ref[...] + b_ref[...]


def _ffn(x1, mod, wg, wu, wd, ln_g, ln_b, seq):
    t = x1.shape[0]
    tm, tf = TM_FFN, TF_FFN
    nt = seq // tm
    tok = lambda i, j: (i, 0)
    return pl.pallas_call(
        _ffn_kernel,
        grid=(t // tm, D_FF // tf),
        in_specs=[
            pl.BlockSpec((tm, D_MODEL), tok),
            pl.BlockSpec((1, 6, D_MODEL), lambda i, j: (i // nt, 0, 0)),
            pl.BlockSpec((D_MODEL, tf), lambda i, j: (0, j)),
            pl.BlockSpec((D_MODEL, tf), lambda i, j: (0, j)),
            pl.BlockSpec((tf, D_MODEL), lambda i, j: (j, 0)),
            pl.BlockSpec((1, D_MODEL), lambda i, j: (0, 0)),
            pl.BlockSpec((1, D_MODEL), lambda i, j: (0, 0)),
        ],
        out_specs=pl.BlockSpec((tm, D_MODEL), tok),
        out_shape=jax.ShapeDtypeStruct((t, D_MODEL), F32),
        scratch_shapes=[pltpu.VMEM((tm, D_MODEL), BF16), pltpu.VMEM((tm, D_MODEL), F32)],
        compiler_params=pltpu.CompilerParams(
            dimension_semantics=("arbitrary", "arbitrary"), vmem_limit_bytes=VMEM_LIMIT),
        name="ffn",
    )(x1, mod, wg, wu, wd, ln_g, ln_b)


def _rope_tables(seq, dim):
    rows = seq // GRID_W
    quarter = dim // 4
    inv_freq = ROPE_THETA ** (-jnp.arange(quarter, dtype=F32) / quarter)
    row_ang = jnp.arange(rows, dtype=F32)[:, None] * inv_freq
    col_ang = jnp.arange(GRID_W, dtype=F32)[:, None] * inv_freq
    ang = jnp.concatenate([
        jnp.broadcast_to(row_ang[:, None, :], (rows, GRID_W, quarter)),
        jnp.broadcast_to(col_ang[None, :, :], (rows, GRID_W, quarter)),
    ], axis=-1).reshape(seq, 2 * quarter)
    half = dim // 2
    pad = jnp.zeros((seq, LANES // 2 - half), F32)
    cos, sin = jnp.cos(ang), jnp.sin(ang)
    c = jnp.concatenate([cos, pad, cos, pad], axis=-1)
    s = jnp.concatenate([-sin, pad, sin, pad], axis=-1)
    return c, s


def _deinterleave(n):
    return np.concatenate([np.arange(0, n, 2), np.arange(1, n, 2)])


def kernel(x, c, w_ada, b_ada, w_in, b_gates, gqa_q_gain, gqa_k_gain, mla_q_gain, mla_kv_gain,
           w_mla_uq, w_mla_ukv, w_branch_gqa, w_branch_mla, w_out, ln1_g, ln1_b,
           w_ffn_gate, w_ffn_up, w_ffn_down, ln2_g, ln2_b):
    batch, seq, d = x.shape
    assert d == D_MODEL and w_ada.shape[0] == DEPTH
    t = batch * seq
    x2 = x.reshape(t, d)

    cg, sg = _rope_tables(seq, HEAD_DIM)
    cm, sm = _rope_tables(seq, MLA_ROPE_DIM)
    perm_head = _deinterleave(HEAD_DIM)
    zeros32 = lambda rows: jnp.zeros((rows, LANES // 2 - MLA_ROPE_DIM // 2), BF16)

    c_pad = jnp.zeros((8, d), F32).at[:batch].set(c)

    for l in range(DEPTH):
        mod = _adaln(c_pad, w_ada[l], b_ada[l][None, :])[:batch].reshape(batch, 6, d)

        wl = w_in[l]
        o0 = 0
        wq = wl[:, o0:o0 + GQA_Q_COLS].reshape(d, GQA_Q_HEADS, HEAD_DIM)[:, :, perm_head]
        o0 += GQA_Q_COLS
        wk = wl[:, o0:o0 + GQA_KV_COLS].reshape(d, GQA_KV_HEADS, HEAD_DIM)[:, :, perm_head]
        o0 += GQA_KV_COLS
        wv = wl[:, o0:o0 + GQA_KV_COLS]
        o0 += GQA_KV_COLS
        wql = wl[:, o0:o0 + MLA_Q_RANK]
        o0 += MLA_Q_RANK
        wkvl = wl[:, o0:o0 + MLA_KV_RANK]
        o0 += MLA_KV_RANK
        wkr = wl[:, o0:o0 + MLA_ROPE_DIM].astype(BF16)
        o0 += MLA_ROPE_DIM
        w_gate = wl[:, o0:].astype(BF16)
        wkr_pad = jnp.concatenate(
            [wkr[:, 0::2], zeros32(d), wkr[:, 1::2], zeros32(d)], axis=-1)
        w_attn = jnp.concatenate([
            wq.reshape(d, GQA_Q_COLS).astype(BF16), wk.reshape(d, GQA_KV_COLS).astype(BF16),
            wv.astype(BF16), wql.astype(BF16), wkvl.astype(BF16), wkr_pad], axis=-1)

        uq = w_mla_uq[l].astype(BF16).reshape(MLA_Q_RANK, MLA_HEADS, MLA_NOPE_DIM + MLA_ROPE_DIM)
        uq_r = uq[:, :, MLA_NOPE_DIM:]
        z = jnp.zeros((MLA_Q_RANK, MLA_HEADS, LANES // 2 - MLA_ROPE_DIM // 2), BF16)
        wuq = jnp.concatenate(
            [uq[:, :, :MLA_NOPE_DIM], uq_r[:, :, 0::2], z, uq_r[:, :, 1::2], z],
            axis=-1).reshape(MLA_Q_RANK, MLA_HEADS * MLA_QK_PAD)
        ukv = w_mla_ukv[l].astype(BF16).reshape(MLA_KV_RANK, MLA_HEADS, MLA_NOPE_DIM + MLA_V_DIM)
        wuk = ukv[:, :, :MLA_NOPE_DIM].reshape(MLA_KV_RANK, MLA_HEADS * MLA_NOPE_DIM)
        wuv = ukv[:, :, MLA_NOPE_DIM:].reshape(MLA_KV_RANK, MLA_HEADS * MLA_V_DIM)

        gq = (gqa_q_gain[l][perm_head] * (LOG2E * HEAD_DIM ** -0.5))[None, :]
        gk = gqa_k_gain[l][perm_head][None, :]

        qg, kg, vgt, qm, km, vmt = _inproj(
            x2, mod, cg, sg, cm, sm, w_attn, gq, gk, mla_q_gain[l][None, :],
            mla_kv_gain[l][None, :], wuq, wuk, wuv, batch, seq)

        y_gqa = _attention(
            qg.reshape(batch, seq, GQA_Q_COLS), kg.reshape(batch, seq, GQA_KV_COLS), vgt,
            q_heads=GQA_Q_HEADS, kv_heads=GQA_KV_HEADS, dq=HEAD_DIM, dv=HEAD_DIM, tq=TQ_GQA)
        y_mla = _attention(
            qm.reshape(batch, seq, MLA_HEADS * MLA_QK_PAD),
            km.reshape(batch, seq, MLA_HEADS * MLA_QK_PAD), vmt,
            q_heads=MLA_HEADS, kv_heads=MLA_HEADS, dq=MLA_QK_PAD, dv=MLA_V_DIM, tq=TQ_MLA)

        x2 = _merge(
            x2, mod, y_gqa.reshape(t, GQA_Q_COLS), y_mla.reshape(t, MLA_HEADS * MLA_V_DIM),
            w_gate, b_gates[l][None, :], w_branch_gqa[l].astype(BF16),
            w_branch_mla[l].astype(BF16), w_out[l].astype(BF16),
            ln1_g[l][None, :], ln1_b[l][None, :], seq)

        x2 = _ffn(
            x2, mod, w_ffn_gate[l].astype(BF16), w_ffn_up[l].astype(BF16),
            w_ffn_down[l].astype(BF16), ln2_g[l][None, :], ln2_b[l][None, :], seq)

    return x2.reshape(batch, seq, d)
```

```python
import functools

import jax
import jax.numpy as jnp
import numpy as np
from jax import lax
from jax.experimental import pallas as pl
from jax.experimental.pallas import tpu as pltpu

D_MODEL = 2048
GRID_W = 64
ROPE_THETA = 10000.0
HEAD_DIM = 128
GQA_Q_HEADS = 8
GQA_KV_HEADS = 2
MLA_HEADS = 8
MLA_Q_RANK = 512
MLA_KV_RANK = 512
MLA_NOPE_DIM = 128
MLA_ROPE_DIM = 64
MLA_V_DIM = 128
D_FF = 5632
DEPTH = 1
DEEPNORM_ALPHA = (2.0 * DEPTH) ** 0.25
LN_EPS = 1e-5
RMS_EPS = 1e-6

GQA_Q_COLS = GQA_Q_HEADS * HEAD_DIM
GQA_KV_COLS = GQA_KV_HEADS * HEAD_DIM
MLA_QK_PAD = 256
LANES = 128
LOG2E = 1.4426950408889634

VMEM_LIMIT = 56 * 1024 * 1024

TM_IN = 256
TV_CHUNK = TM_IN
TK_ATT = 256
ATT_STEPS = 6
SCORE_ROWS = 256
SUM_ROWS = 16
EXP_ROWS = 64
TQ_GQA = 512
TQ_MLA = 2048
TM_MERGE = 512
TN_MERGE = 512
TM_FFN = 512
TF_FFN = 512

BF16 = jnp.bfloat16
F32 = jnp.float32


def _const_spec(shape):
    nd = len(shape)
    return pl.BlockSpec(shape, lambda *_: (0,) * nd, pipeline_mode=pl.Buffered(1))


def _ln(x):
    mu = jnp.mean(x, axis=-1, keepdims=True)
    xc = x - mu
    var = jnp.mean(xc * xc, axis=-1, keepdims=True)
    return xc * lax.rsqrt(var + LN_EPS)


def _rms(x, gain):
    ms = jnp.mean(x * x, axis=-1, keepdims=True)
    return x * lax.rsqrt(ms + RMS_EPS) * gain


def _rope(x, c, s):
    return x * c + pltpu.roll(x, LANES // 2, 1) * s


def _adaln_kernel(c_ref, w_ref, b_ref, o_ref):
    c = c_ref[...]
    act = (c * jax.nn.sigmoid(c)).astype(BF16)
    o_ref[...] = jnp.dot(act, w_ref[...].astype(BF16), preferred_element_type=F32) + b_ref[...]


def _adaln(c_pad, w_ada, b_ada):
    n = w_ada.shape[1]
    tn = 1024
    return pl.pallas_call(
        _adaln_kernel,
        grid=(n // tn,),
        in_specs=[
            pl.BlockSpec((8, D_MODEL), lambda j: (0, 0)),
            pl.BlockSpec((D_MODEL, tn), lambda j: (0, j)),
            pl.BlockSpec((1, tn), lambda j: (0, j)),
        ],
        out_specs=pl.BlockSpec((8, tn), lambda j: (0, j)),
        out_shape=jax.ShapeDtypeStruct((8, n), F32),
        compiler_params=pltpu.CompilerParams(
            dimension_semantics=("arbitrary",), vmem_limit_bytes=VMEM_LIMIT),
        name="adaln",
    )(c_pad, w_ada, b_ada)


def _inproj_kernel(x_ref, mod_ref, cg_ref, sg_ref, cm_ref, sm_ref, w_ref,
                   gq_ref, gk_ref, gql_ref, gkvl_ref, wuq_ref, wuk_ref, wuv_ref,
                   qg_ref, kg_ref, vgt_ref, qm_ref, km_ref, vmt_ref, *, mla_scale):
    x = x_ref[...]
    shift = mod_ref[0, 0:1, :]
    scale = mod_ref[0, 1:2, :]
    h = (_ln(x) * (1.0 + scale) + shift).astype(BF16)
    proj = jnp.dot(h, w_ref[...], preferred_element_type=F32)

    cg = cg_ref[...]
    sg = sg_ref[...]
    cm = cm_ref[...]
    sm = sm_ref[...]

    for hd in range(GQA_Q_HEADS):
        q = proj[:, hd * HEAD_DIM:(hd + 1) * HEAD_DIM]
        q = _rope(_rms(q, gq_ref[...]), cg, sg)
        qg_ref[:, hd * HEAD_DIM:(hd + 1) * HEAD_DIM] = q.astype(BF16)
    off = GQA_Q_COLS
    for hd in range(GQA_KV_HEADS):
        k = proj[:, off + hd * HEAD_DIM: off + (hd + 1) * HEAD_DIM]
        k = _rope(_rms(k, gk_ref[...]), cg, sg)
        kg_ref[:, hd * HEAD_DIM:(hd + 1) * HEAD_DIM] = k.astype(BF16)
    off += GQA_KV_COLS
    for hd in range(GQA_KV_HEADS):
        v = proj[:, off + hd * HEAD_DIM: off + (hd + 1) * HEAD_DIM]
        vgt_ref[0, hd, 0] = v.T.astype(BF16)
    off += GQA_KV_COLS

    q_lat = _rms(proj[:, off: off + MLA_Q_RANK], gql_ref[...]).astype(BF16)
    off += MLA_Q_RANK
    kv_lat = _rms(proj[:, off: off + MLA_KV_RANK], gkvl_ref[...]).astype(BF16)
    off += MLA_KV_RANK
    k_rope = _rope(proj[:, off: off + LANES], cm, sm).astype(BF16)

    q_m = jnp.dot(q_lat, wuq_ref[...], preferred_element_type=F32) * mla_scale
    k_n = jnp.dot(kv_lat, wuk_ref[...], preferred_element_type=F32)
    v_m = jnp.dot(kv_lat, wuv_ref[...], preferred_element_type=F32)
    for hd in range(MLA_HEADS):
        b0 = hd * MLA_QK_PAD
        qm_ref[:, b0: b0 + LANES] = q_m[:, b0: b0 + LANES].astype(BF16)
        qm_ref[:, b0 + LANES: b0 + 2 * LANES] = _rope(
            q_m[:, b0 + LANES: b0 + 2 * LANES], cm, sm).astype(BF16)
        km_ref[:, b0: b0 + LANES] = k_n[:, hd * LANES:(hd + 1) * LANES].astype(BF16)
        km_ref[:, b0 + LANES: b0 + 2 * LANES] = k_rope
        vmt_ref[0, hd, 0] = v_m[:, hd * MLA_V_DIM:(hd + 1) * MLA_V_DIM].T.astype(BF16)


def _inproj(x2, mod, cg, sg, cm, sm, w_attn, gq, gk, gql, gkvl, wuq, wuk, wuv, batch, seq):
    t = x2.shape[0]
    tm = TM_IN
    nt = seq // tm
    n_attn = w_attn.shape[1]
    kern = functools.partial(
        _inproj_kernel, mla_scale=float(LOG2E * (MLA_NOPE_DIM + MLA_ROPE_DIM) ** -0.5))
    tok = lambda i: (i, 0)
    pos = lambda i: (i % nt, 0)
    in_specs = [
        pl.BlockSpec((tm, D_MODEL), tok),
        pl.BlockSpec((1, 6, D_MODEL), lambda i: (i // nt, 0, 0)),
        pl.BlockSpec((tm, LANES), pos),
        pl.BlockSpec((tm, LANES), pos),
        pl.BlockSpec((tm, LANES), pos),
        pl.BlockSpec((tm, LANES), pos),
        _const_spec((D_MODEL, n_attn)),
        _const_spec((1, HEAD_DIM)),
        _const_spec((1, HEAD_DIM)),
        _const_spec((1, MLA_Q_RANK)),
        _const_spec((1, MLA_KV_RANK)),
        _const_spec(wuq.shape),
        _const_spec(wuk.shape),
        _const_spec(wuv.shape),
    ]
    out_shape = [
        jax.ShapeDtypeStruct((t, GQA_Q_COLS), BF16),
        jax.ShapeDtypeStruct((t, GQA_KV_COLS), BF16),
        jax.ShapeDtypeStruct((batch, GQA_KV_HEADS, seq // TV_CHUNK, HEAD_DIM, TV_CHUNK), BF16),
        jax.ShapeDtypeStruct((t, MLA_HEADS * MLA_QK_PAD), BF16),
        jax.ShapeDtypeStruct((t, MLA_HEADS * MLA_QK_PAD), BF16),
        jax.ShapeDtypeStruct((batch, MLA_HEADS, seq // TV_CHUNK, MLA_V_DIM, TV_CHUNK), BF16),
    ]
    vt_map = lambda i: (i // nt, 0, i % nt, 0, 0)
    out_specs = [
        pl.BlockSpec((tm, GQA_Q_COLS), tok),
        pl.BlockSpec((tm, GQA_KV_COLS), tok),
        pl.BlockSpec((1, GQA_KV_HEADS, 1, HEAD_DIM, TV_CHUNK), vt_map),
        pl.BlockSpec((tm, MLA_HEADS * MLA_QK_PAD), tok),
        pl.BlockSpec((tm, MLA_HEADS * MLA_QK_PAD), tok),
        pl.BlockSpec((1, MLA_HEADS, 1, MLA_V_DIM, TV_CHUNK), vt_map),
    ]
    return pl.pallas_call(
        kern,
        grid=(t // tm,),
        in_specs=in_specs,
        out_specs=out_specs,
        out_shape=out_shape,
        compiler_params=pltpu.CompilerParams(
            dimension_semantics=("arbitrary",), vmem_limit_bytes=VMEM_LIMIT),
        name="inproj",
    )(x2, mod, cg, sg, cm, sm, w_attn, gq, gk, gql, gkvl, wuq, wuk, wuv)


def _attn_kernel(q_ref, k_ref, vt_ref, o_ref, qt_scr, st_scr, pb_scr, acc_scr,
                 *, groups, dq, dv, tq, tk, nk, tv, steps):
    mq = groups * tq
    nv = tk // tv
    for g in range(groups):
        qg = q_ref[0, :, g * dq:(g + 1) * dq].astype(F32)
        qt_scr[:, g * tq:(g + 1) * tq] = qg.T.astype(BF16)

    def scores(c, st_ref):
        cmax = None
        for r in range(tk // SCORE_ROWS):
            start = pl.multiple_of(c * tk + r * SCORE_ROWS, SCORE_ROWS)
            st = jnp.dot(k_ref[0, pl.ds(start, SCORE_ROWS), :], qt_scr[...],
                         preferred_element_type=F32)
            st_ref[r * SCORE_ROWS:(r + 1) * SCORE_ROWS, :] = st
            bmax = jnp.max(st, axis=0, keepdims=True)
            cmax = bmax if cmax is None else jnp.maximum(cmax, bmax)
        return cmax

    ones_rows = jnp.ones((SUM_ROWS, tk), BF16)

    def values(c, pb_ref, alpha):
        vt = jnp.concatenate([vt_ref[0, 0, c * nv + j] for j in range(nv)], axis=1)
        vt = jnp.concatenate([vt, ones_rows], axis=0)
        acc_scr[...] = alpha * acc_scr[...] + jnp.dot(
            vt, pb_ref[...], preferred_element_type=F32)

    def softmax(st_ref, pb_ref, cmax, m_prev):
        m_new = jnp.maximum(m_prev, cmax)
        alpha = jnp.exp2(m_prev - m_new)
        for r in range(tk // EXP_ROWS):
            rows = slice(r * EXP_ROWS, (r + 1) * EXP_ROWS)
            pb_ref[rows, :] = jnp.exp2((st_ref[rows, :] - m_new).astype(BF16))
        return m_new, alpha

    def step(c, cur, nxt, carry, first=False, last=False):
        cmax, m_prev, alpha_prev = carry
        cmax_next = cmax if last else scores(c + 1, st_scr.at[nxt])
        if not first:
            values(c - 1, pb_scr.at[nxt], alpha_prev)
        m_new, alpha = softmax(st_scr.at[cur], pb_scr.at[cur], cmax, m_prev)
        return cmax_next, m_new, alpha

    acc_scr[...] = jnp.zeros(acc_scr.shape, F32)
    carry = (scores(0, st_scr.at[0]), jnp.full((1, mq), -jnp.inf, F32), jnp.ones((1, mq), F32))
    carry = step(0, 0, 1, carry, first=True)

    n_loop = (nk - 2) // steps

    def body(i, carry):
        c0 = steps * i + 1
        for s in range(steps):
            carry = step(c0 + s, (1 + s) % 2, s % 2, carry)
        return carry

    if n_loop == 1:
        carry = body(0, carry)
    elif n_loop > 1:
        carry = lax.fori_loop(0, n_loop, body, carry)
    for c in range(n_loop * steps + 1, nk - 1):
        carry = step(c, c % 2, (c + 1) % 2, carry)
    _, _, alpha = step(nk - 1, (nk - 1) % 2, nk % 2, carry, last=True)
    values(nk - 1, pb_scr.at[(nk - 1) % 2], alpha)

    out_t = acc_scr[0:dv, :] / acc_scr[dv:dv + 1, :]
    for g in range(groups):
        o_ref[0, :, g * dv:(g + 1) * dv] = out_t[:, g * tq:(g + 1) * tq].T.astype(o_ref.dtype)


def _attention(q, k, vt, *, q_heads, kv_heads, dq, dv, tq):
    batch, seq, _ = q.shape
    groups = q_heads // kv_heads
    tk = TK_ATT
    tv = TV_CHUNK
    nk = seq // tk
    mq = groups * tq
    kern = functools.partial(_attn_kernel, groups=groups, dq=dq, dv=dv, tq=tq, tk=tk, nk=nk,
                             tv=tv, steps=ATT_STEPS)
    return pl.pallas_call(
        kern,
        grid=(batch, kv_heads, seq // tq),
        in_specs=[
            pl.BlockSpec((1, tq, groups * dq), lambda b, h, i: (b, i, h)),
            pl.BlockSpec((1, seq, dq), lambda b, h, i: (b, 0, h)),
            pl.BlockSpec((1, 1, seq // tv, dv, tv), lambda b, h, i: (b, h, 0, 0, 0)),
        ],
        out_specs=pl.BlockSpec((1, tq, groups * dv), lambda b, h, i: (b, i, h)),
        out_shape=jax.ShapeDtypeStruct((batch, seq, q_heads * dv), BF16),
        scratch_shapes=[
            pltpu.VMEM((dq, mq), BF16),
            pltpu.VMEM((2, tk, mq), F32),
            pltpu.VMEM((2, tk, mq), BF16),
            pltpu.VMEM((dv + SUM_ROWS, mq), F32),
        ],
        compiler_params=pltpu.CompilerParams(
            dimension_semantics=("arbitrary", "arbitrary", "arbitrary"),
            vmem_limit_bytes=VMEM_LIMIT),
        name=f"attn_g{groups}",
    )(q, k, vt)


def _merge_kernel(x_ref, mod_ref, yg_ref, ym_ref, wga_ref, wgb_ref, bga_ref, bgb_ref,
                  wbg_ref, wbm_ref, wo_ref, g_ref, b_ref, o_ref, h_scr, acc_scr):
    j = pl.program_id(1)

    @pl.when(j == 0)
    def _():
        shift = mod_ref[0, 0:1, :]
        scale = mod_ref[0, 1:2, :]
        h_scr[...] = (_ln(x_ref[...]) * (1.0 + scale) + shift).astype(BF16)
        acc_scr[...] = jnp.zeros(acc_scr.shape, F32)

    h = h_scr[...]
    la = jnp.dot(h, wga_ref[...], preferred_element_type=F32) + bga_ref[...]
    lb = jnp.dot(h, wgb_ref[...], preferred_element_type=F32) + bgb_ref[...]
    a = jnp.dot(yg_ref[...], wbg_ref[...], preferred_element_type=F32)
    b = jnp.dot(ym_ref[...], wbm_ref[...], preferred_element_type=F32)
    merged = jax.nn.sigmoid(la) * a + jax.nn.sigmoid(lb) * b
    acc_scr[...] += jnp.dot(merged.astype(BF16), wo_ref[...], preferred_element_type=F32)

    @pl.when(j == pl.num_programs(1) - 1)
    def _():
        gate = mod_ref[0, 2:3, :]
        r = DEEPNORM_ALPHA * x_ref[...] + gate * acc_scr[...]
        o_ref[...] = _ln(r) * g_ref[...] + b_ref[...]


def _merge(x2, mod, yg, ym, w_gate, b_gate, wbg, wbm, wo, ln_g, ln_b, seq):
    t = x2.shape[0]
    tm, tn = TM_MERGE, TN_MERGE
    nt = seq // tm
    nj = D_MODEL // tn
    tok = lambda i, j: (i, 0)
    return pl.pallas_call(
        _merge_kernel,
        grid=(t // tm, nj),
        in_specs=[
            pl.BlockSpec((tm, D_MODEL), tok),
            pl.BlockSpec((1, 6, D_MODEL), lambda i, j: (i // nt, 0, 0)),
            pl.BlockSpec((tm, yg.shape[1]), tok),
            pl.BlockSpec((tm, ym.shape[1]), tok),
            pl.BlockSpec((D_MODEL, tn), lambda i, j: (0, j)),
            pl.BlockSpec((D_MODEL, tn), lambda i, j: (0, j + nj)),
            pl.BlockSpec((1, tn), lambda i, j: (0, j)),
            pl.BlockSpec((1, tn), lambda i, j: (0, j + nj)),
            pl.BlockSpec((wbg.shape[0], tn), lambda i, j: (0, j)),
            pl.BlockSpec((wbm.shape[0], tn), lambda i, j: (0, j)),
            pl.BlockSpec((tn, D_MODEL), lambda i, j: (j, 0)),
            pl.BlockSpec((1, D_MODEL), lambda i, j: (0, 0)),
            pl.BlockSpec((1, D_MODEL), lambda i, j: (0, 0)),
        ],
        out_specs=pl.BlockSpec((tm, D_MODEL), tok),
        out_shape=jax.ShapeDtypeStruct((t, D_MODEL), F32),
        scratch_shapes=[pltpu.VMEM((tm, D_MODEL), BF16), pltpu.VMEM((tm, D_MODEL), F32)],
        compiler_params=pltpu.CompilerParams(
            dimension_semantics=("arbitrary", "arbitrary"), vmem_limit_bytes=VMEM_LIMIT),
        name="merge",
    )(x2, mod, yg, ym, w_gate, w_gate, b_gate, b_gate, wbg, wbm, wo, ln_g, ln_b)


def _ffn_kernel(x_ref, mod_ref, wg_ref, wu_ref, wd_ref, g_ref, b_ref, o_ref, h_scr, acc_scr):
    j = pl.program_id(1)

    @pl.when(j == 0)
    def _():
        shift = mod_ref[0, 3:4, :]
        scale = mod_ref[0, 4:5, :]
        h_scr[...] = (_ln(x_ref[...]) * (1.0 + scale) + shift).astype(BF16)
        acc_scr[...] = jnp.zeros(acc_scr.shape, F32)

    h = h_scr[...]
    gt = jnp.dot(h, wg_ref[...], preferred_element_type=F32)
    up = jnp.dot(h, wu_ref[...], preferred_element_type=F32)
    a = (gt * jax.nn.sigmoid(gt) * up).astype(BF16)
    acc_scr[...] += jnp.dot(a, wd_ref[...], preferred_element_type=F32)

    @pl.when(j == pl.num_programs(1) - 1)
    def _():
        gate = mod_ref[0, 5:6, :]
        r = DEEPNORM_ALPHA * x_ref[...] + gate * acc_scr[...]
        o_ref[...] = _ln(r) * g_ref[...] + b_ref[...]


def _ffn(x1, mod, wg, wu, wd, ln_g, ln_b, seq):
    t = x1.shape[0]
    tm, tf = TM_FFN, TF_FFN
    nt = seq // tm
    tok = lambda i, j: (i, 0)
    return pl.pallas_call(
        _ffn_kernel,
        grid=(t // tm, D_FF // tf),
        in_specs=[
            pl.BlockSpec((tm, D_MODEL), tok),
            pl.BlockSpec((1, 6, D_MODEL), lambda i, j: (i // nt, 0, 0)),
            pl.BlockSpec((D_MODEL, tf), lambda i, j: (0, j)),
            pl.BlockSpec((D_MODEL, tf), lambda i, j: (0, j)),
            pl.BlockSpec((tf, D_MODEL), lambda i, j: (j, 0)),
            pl.BlockSpec((1, D_MODEL), lambda i, j: (0, 0)),
            pl.BlockSpec((1, D_MODEL), lambda i, j: (0, 0)),
        ],
        out_specs=pl.BlockSpec((tm, D_MODEL), tok),
        out_shape=jax.ShapeDtypeStruct((t, D_MODEL), F32),
        scratch_shapes=[pltpu.VMEM((tm, D_MODEL), BF16), pltpu.VMEM((tm, D_MODEL), F32)],
        compiler_params=pltpu.CompilerParams(
            dimension_semantics=("arbitrary", "arbitrary"), vmem_limit_bytes=VMEM_LIMIT),
        name="ffn",
    )(x1, mod, wg, wu, wd, ln_g, ln_b)


def _rope_tables(seq, dim):
    rows = seq // GRID_W
    quarter = dim // 4
    inv_freq = ROPE_THETA ** (-jnp.arange(quarter, dtype=F32) / quarter)
    row_ang = jnp.arange(rows, dtype=F32)[:, None] * inv_freq
    col_ang = jnp.arange(GRID_W, dtype=F32)[:, None] * inv_freq
    ang = jnp.concatenate([
        jnp.broadcast_to(row_ang[:, None, :], (rows, GRID_W, quarter)),
        jnp.broadcast_to(col_ang[None, :, :], (rows, GRID_W, quarter)),
    ], axis=-1).reshape(seq, 2 * quarter)
    half = dim // 2
    pad = jnp.zeros((seq, LANES // 2 - half), F32)
    cos, sin = jnp.cos(ang), jnp.sin(ang)
    c = jnp.concatenate([cos, pad, cos, pad], axis=-1)
    s = jnp.concatenate([-sin, pad, sin, pad], axis=-1)
    return c, s


def _deinterleave(n):
    return np.concatenate([np.arange(0, n, 2), np.arange(1, n, 2)])


def kernel(x, c, w_ada, b_ada, w_in, b_gates, gqa_q_gain, gqa_k_gain, mla_q_gain, mla_kv_gain,
           w_mla_uq, w_mla_ukv, w_branch_gqa, w_branch_mla, w_out, ln1_g, ln1_b,
           w_ffn_gate, w_ffn_up, w_ffn_down, ln2_g, ln2_b):
    batch, seq, d = x.shape
    assert d == D_MODEL and w_ada.shape[0] == DEPTH
    t = batch * seq
    x2 = x.reshape(t, d)

    cg, sg = _rope_tables(seq, HEAD_DIM)
    cm, sm = _rope_tables(seq, MLA_ROPE_DIM)
    perm_head = _deinterleave(HEAD_DIM)
    zeros32 = lambda rows: jnp.zeros((rows, LANES // 2 - MLA_ROPE_DIM // 2), BF16)

    c_pad = jnp.zeros((8, d), F32).at[:batch].set(c)

    for l in range(DEPTH):
        mod = _adaln(c_pad, w_ada[l], b_ada[l][None, :])[:batch].reshape(batch, 6, d)

        wl = w_in[l]
        o0 = 0
        wq = wl[:, o0:o0 + GQA_Q_COLS].reshape(d, GQA_Q_HEADS, HEAD_DIM)[:, :, perm_head]
        o0 += GQA_Q_COLS
        wk = wl[:, o0:o0 + GQA_KV_COLS].reshape(d, GQA_KV_HEADS, HEAD_DIM)[:, :, perm_head]
        o0 += GQA_KV_COLS
        wv = wl[:, o0:o0 + GQA_KV_COLS]
        o0 += GQA_KV_COLS
        wql = wl[:, o0:o0 + MLA_Q_RANK]
        o0 += MLA_Q_RANK
        wkvl = wl[:, o0:o0 + MLA_KV_RANK]
        o0 += MLA_KV_RANK
        wkr = wl[:, o0:o0 + MLA_ROPE_DIM].astype(BF16)
        o0 += MLA_ROPE_DIM
        w_gate = wl[:, o0:].astype(BF16)
        wkr_pad = jnp.concatenate(
            [wkr[:, 0::2], zeros32(d), wkr[:, 1::2], zeros32(d)], axis=-1)
        w_attn = jnp.concatenate([
            wq.reshape(d, GQA_Q_COLS).astype(BF16), wk.reshape(d, GQA_KV_COLS).astype(BF16),
            wv.astype(BF16), wql.astype(BF16), wkvl.astype(BF16), wkr_pad], axis=-1)

        uq = w_mla_uq[l].astype(BF16).reshape(MLA_Q_RANK, MLA_HEADS, MLA_NOPE_DIM + MLA_ROPE_DIM)
        uq_r = uq[:, :, MLA_NOPE_DIM:]
        z = jnp.zeros((MLA_Q_RANK, MLA_HEADS, LANES // 2 - MLA_ROPE_DIM // 2), BF16)
        wuq = jnp.concatenate(
            [uq[:, :, :MLA_NOPE_DIM], uq_r[:, :, 0::2], z, uq_r[:, :, 1::2], z],
            axis=-1).reshape(MLA_Q_RANK, MLA_HEADS * MLA_QK_PAD)
        ukv = w_mla_ukv[l].astype(BF16).reshape(MLA_KV_RANK, MLA_HEADS, MLA_NOPE_DIM + MLA_V_DIM)
        wuk = ukv[:, :, :MLA_NOPE_DIM].reshape(MLA_KV_RANK, MLA_HEADS * MLA_NOPE_DIM)
        wuv = ukv[:, :, MLA_NOPE_DIM:].reshape(MLA_KV_RANK, MLA_HEADS * MLA_V_DIM)

        gq = (gqa_q_gain[l][perm_head] * (LOG2E * HEAD_DIM ** -0.5))[None, :]
        gk = gqa_k_gain[l][perm_head][None, :]

        qg, kg, vgt, qm, km, vmt = _inproj(
            x2, mod, cg, sg, cm, sm, w_attn, gq, gk, mla_q_gain[l][None, :],
            mla_kv_gain[l][None, :], wuq, wuk, wuv, batch, seq)

        y_gqa = _attention(
            qg.reshape(batch, seq, GQA_Q_COLS), kg.reshape(batch, seq, GQA_KV_COLS), vgt,
            q_heads=GQA_Q_HEADS, kv_heads=GQA_KV_HEADS, dq=HEAD_DIM, dv=HEAD_DIM, tq=TQ_GQA)
        y_mla = _attention(
            qm.reshape(batch, seq, MLA_HEADS * MLA_QK_PAD),
            km.reshape(batch, seq, MLA_HEADS * MLA_QK_PAD), vmt,
            q_heads=MLA_HEADS, kv_heads=MLA_HEADS, dq=MLA_QK_PAD, dv=MLA_V_DIM, tq=TQ_MLA)

        x2 = _merge(
            x2, mod, y_gqa.reshape(t, GQA_Q_COLS), y_mla.reshape(t, MLA_HEADS * MLA_V_DIM),
            w_gate, b_gates[l][None, :], w_branch_gqa[l].astype(BF16),
            w_branch_mla[l].astype(BF16), w_out[l].astype(BF16),
            ln1_g[l][None, :], ln1_b[l][None, :], seq)

        x2 = _ffn(
            x2, mod, w_ffn_gate[l].astype(BF16), w_ffn_up[l].astype(BF16),
            w_ffn_down[l].astype(BF16), ln2_g[l][None, :], ln2_b[l][None, :], seq)

    return x2.reshape(batch, seq, d)
```

```python
import functools

import jax
import jax.numpy as jnp
import numpy as np
from jax import lax
from jax.experimental import pallas as pl
from jax.experimental.pallas import tpu as pltpu

D_MODEL = 2048
GRID_W = 64
ROPE_THETA = 10000.0
HEAD_DIM = 128
GQA_Q_HEADS = 8
GQA_KV_HEADS = 2
MLA_HEADS = 8
MLA_Q_RANK = 512
MLA_KV_RANK = 512
MLA_NOPE_DIM = 128
MLA_ROPE_DIM = 64
MLA_V_DIM = 128
D_FF = 5632
DEPTH = 1
DEEPNORM_ALPHA = (2.0 * DEPTH) ** 0.25
LN_EPS = 1e-5
RMS_EPS = 1e-6

GQA_Q_COLS = GQA_Q_HEADS * HEAD_DIM
GQA_KV_COLS = GQA_KV_HEADS * HEAD_DIM
MLA_QK_PAD = 256
LANES = 128
LOG2E = 1.4426950408889634

VMEM_LIMIT = 56 * 1024 * 1024

TM_IN = 256
TV_CHUNK = TM_IN
TK_ATT = 256
ATT_STEPS = 6
SCORE_ROWS = 256
SUM_ROWS = 16
EXP_ROWS = 64
TQ_GQA = 1024
TQ_MLA = 4096
TM_MERGE = 512
TN_MERGE = 512
TM_FFN = 512
TF_FFN = 512

BF16 = jnp.bfloat16
F32 = jnp.float32


def _const_spec(shape):
    nd = len(shape)
    return pl.BlockSpec(shape, lambda *_: (0,) * nd, pipeline_mode=pl.Buffered(1))


def _ln(x):
    mu = jnp.mean(x, axis=-1, keepdims=True)
    xc = x - mu
    var = jnp.mean(xc * xc, axis=-1, keepdims=True)
    return xc * lax.rsqrt(var + LN_EPS)


def _rms(x, gain):
    ms = jnp.mean(x * x, axis=-1, keepdims=True)
    return x * lax.rsqrt(ms + RMS_EPS) * gain


def _rope(x, c, s):
    return x * c + pltpu.roll(x, LANES // 2, 1) * s


def _adaln_kernel(c_ref, w_ref, b_ref, o_ref):
    c = c_ref[...]
    act = (c * jax.nn.sigmoid(c)).astype(BF16)
    o_ref[...] = jnp.dot(act, w_ref[...].astype(BF16), preferred_element_type=F32) + b_ref[...]


def _adaln(c_pad, w_ada, b_ada):
    n = w_ada.shape[1]
    tn = 1024
    return pl.pallas_call(
        _adaln_kernel,
        grid=(n // tn,),
        in_specs=[
            pl.BlockSpec((8, D_MODEL), lambda j: (0, 0)),
            pl.BlockSpec((D_MODEL, tn), lambda j: (0, j)),
            pl.BlockSpec((1, tn), lambda j: (0, j)),
        ],
        out_specs=pl.BlockSpec((8, tn), lambda j: (0, j)),
        out_shape=jax.ShapeDtypeStruct((8, n), F32),
        compiler_params=pltpu.CompilerParams(
            dimension_semantics=("arbitrary",), vmem_limit_bytes=VMEM_LIMIT),
        name="adaln",
    )(c_pad, w_ada, b_ada)


def _inproj_kernel(x_ref, mod_ref, cg_ref, sg_ref, cm_ref, sm_ref, w_ref,
                   gq_ref, gk_ref, gql_ref, gkvl_ref, wuq_ref, wuk_ref, wuv_ref,
                   qg_ref, kg_ref, vgt_ref, qm_ref, km_ref, vmt_ref, *, mla_scale):
    x = x_ref[...]
    shift = mod_ref[0, 0:1, :]
    scale = mod_ref[0, 1:2, :]
    h = (_ln(x) * (1.0 + scale) + shift).astype(BF16)
    proj = jnp.dot(h, w_ref[...], preferred_element_type=F32)

    cg = cg_ref[...]
    sg = sg_ref[...]
    cm = cm_ref[...]
    sm = sm_ref[...]

    for hd in range(GQA_Q_HEADS):
        q = proj[:, hd * HEAD_DIM:(hd + 1) * HEAD_DIM]
        q = _rope(_rms(q, gq_ref[...]), cg, sg)
        qg_ref[:, hd * HEAD_DIM:(hd + 1) * HEAD_DIM] = q.astype(BF16)
    off = GQA_Q_COLS
    for hd in range(GQA_KV_HEADS):
        k = proj[:, off + hd * HEAD_DIM: off + (hd + 1) * HEAD_DIM]
        k = _rope(_rms(k, gk_ref[...]), cg, sg)
        kg_ref[:, hd * HEAD_DIM:(hd + 1) * HEAD_DIM] = k.astype(BF16)
    off += GQA_KV_COLS
    for hd in range(GQA_KV_HEADS):
        v = proj[:, off + hd * HEAD_DIM: off + (hd + 1) * HEAD_DIM]
        vgt_ref[0, hd, 0] = v.T.astype(BF16)
    off += GQA_KV_COLS

    q_lat = _rms(proj[:, off: off + MLA_Q_RANK], gql_ref[...]).astype(BF16)
    off += MLA_Q_RANK
    kv_lat = _rms(proj[:, off: off + MLA_KV_RANK], gkvl_ref[...]).astype(BF16)
    off += MLA_KV_RANK
    k_rope = _rope(proj[:, off: off + LANES], cm, sm).astype(BF16)

    q_m = jnp.dot(q_lat, wuq_ref[...], preferred_element_type=F32) * mla_scale
    k_n = jnp.dot(kv_lat, wuk_ref[...], preferred_element_type=F32)
    v_m = jnp.dot(kv_lat, wuv_ref[...], preferred_element_type=F32)
    for hd in range(MLA_HEADS):
        b0 = hd * MLA_QK_PAD
        qm_ref[:, b0: b0 + LANES] = q_m[:, b0: b0 + LANES].astype(BF16)
        qm_ref[:, b0 + LANES: b0 + 2 * LANES] = _rope(
            q_m[:, b0 + LANES: b0 + 2 * LANES], cm, sm).astype(BF16)
        km_ref[:, b0: b0 + LANES] = k_n[:, hd * LANES:(hd + 1) * LANES].astype(BF16)
        km_ref[:, b0 + LANES: b0 + 2 * LANES] = k_rope
        vmt_ref[0, hd, 0] = v_m[:, hd * MLA_V_DIM:(hd + 1) * MLA_V_DIM].T.astype(BF16)


def _inproj(x2, mod, cg, sg, cm, sm, w_attn, gq, gk, gql, gkvl, wuq, wuk, wuv, batch, seq):
    t = x2.shape[0]
    tm = TM_IN
    nt = seq // tm
    n_attn = w_attn.shape[1]
    kern = functools.partial(
        _inproj_kernel, mla_scale=float(LOG2E * (MLA_NOPE_DIM + MLA_ROPE_DIM) ** -0.5))
    tok = lambda i: (i, 0)
    pos = lambda i: (i % nt, 0)
    in_specs = [
        pl.BlockSpec((tm, D_MODEL), tok),
        pl.BlockSpec((1, 6, D_MODEL), lambda i: (i // nt, 0, 0)),
        pl.BlockSpec((tm, LANES), pos),
        pl.BlockSpec((tm, LANES), pos),
        pl.BlockSpec((tm, LANES), pos),
        pl.BlockSpec((tm, LANES), pos),
        _const_spec((D_MODEL, n_attn)),
        _const_spec((1, HEAD_DIM)),
        _const_spec((1, HEAD_DIM)),
        _const_spec((1, MLA_Q_RANK)),
        _const_spec((1, MLA_KV_RANK)),
        _const_spec(wuq.shape),
        _const_spec(wuk.shape),
        _const_spec(wuv.shape),
    ]
    out_shape = [
        jax.ShapeDtypeStruct((t, GQA_Q_COLS), BF16),
        jax.ShapeDtypeStruct((t, GQA_KV_COLS), BF16),
        jax.ShapeDtypeStruct((batch, GQA_KV_HEADS, seq // TV_CHUNK, HEAD_DIM, TV_CHUNK), BF16),
        jax.ShapeDtypeStruct((t, MLA_HEADS * MLA_QK_PAD), BF16),
        jax.ShapeDtypeStruct((t, MLA_HEADS * MLA_QK_PAD), BF16),
        jax.ShapeDtypeStruct((batch, MLA_HEADS, seq // TV_CHUNK, MLA_V_DIM, TV_CHUNK), BF16),
    ]
    vt_map = lambda i: (i // nt, 0, i % nt, 0, 0)
    out_specs = [
        pl.BlockSpec((tm, GQA_Q_COLS), tok),
        pl.BlockSpec((tm, GQA_KV_COLS), tok),
        pl.BlockSpec((1, GQA_KV_HEADS, 1, HEAD_DIM, TV_CHUNK), vt_map),
        pl.BlockSpec((tm, MLA_HEADS * MLA_QK_PAD), tok),
        pl.BlockSpec((tm, MLA_HEADS * MLA_QK_PAD), tok),
        pl.BlockSpec((1, MLA_HEADS, 1, MLA_V_DIM, TV_CHUNK), vt_map),
    ]
    return pl.pallas_call(
        kern,
        grid=(t // tm,),
        in_specs=in_specs,
        out_specs=out_specs,
        out_shape=out_shape,
        compiler_params=pltpu.CompilerParams(
            dimension_semantics=("arbitrary",), vmem_limit_bytes=VMEM_LIMIT),
        name="inproj",
    )(x2, mod, cg, sg, cm, sm, w_attn, gq, gk, gql, gkvl, wuq, wuk, wuv)


def _attn_kernel(q_ref, k_ref, vt_ref, o_ref, qt_scr, st_scr, pb_scr, acc_scr,
                 *, groups, dq, dv, tq, tk, nk, tv, steps):
    mq = groups * tq
    nv = tk // tv
    for g in range(groups):
        qg = q_ref[0, :, g * dq:(g + 1) * dq].astype(F32)
        qt_scr[:, g * tq:(g + 1) * tq] = qg.T.astype(BF16)

    def scores(c, st_ref):
        cmax = None
        for r in range(tk // SCORE_ROWS):
            start = pl.multiple_of(c * tk + r * SCORE_ROWS, SCORE_ROWS)
            st = jnp.dot(k_ref[0, pl.ds(start, SCORE_ROWS), :], qt_scr[...],
                         preferred_element_type=F32)
            st_ref[r * SCORE_ROWS:(r + 1) * SCORE_ROWS, :] = st
            bmax = jnp.max(st, axis=0, keepdims=True)
            cmax = bmax if cmax is None else jnp.maximum(cmax, bmax)
        return cmax

    ones_rows = jnp.ones((SUM_ROWS, tk), BF16)

    def values(c, pb_ref, alpha):
        vt = jnp.concatenate([vt_ref[0, 0, c * nv + j] for j in range(nv)], axis=1)
        vt = jnp.concatenate([vt, ones_rows], axis=0)
        acc_scr[...] = alpha * acc_scr[...] + jnp.dot(
            vt, pb_ref[...], preferred_element_type=F32)

    def softmax(st_ref, pb_ref, cmax, m_prev):
        m_new = jnp.maximum(m_prev, cmax)
        alpha = jnp.exp2(m_prev - m_new)
        for r in range(tk // EXP_ROWS):
            rows = slice(r * EXP_ROWS, (r + 1) * EXP_ROWS)
            pb_ref[rows, :] = jnp.exp2((st_ref[rows, :] - m_new).astype(BF16))
        return m_new, alpha

    def step(c, cur, nxt, carry, first=False, last=False):
        cmax, m_prev, alpha_prev = carry
        cmax_next = cmax if last else scores(c + 1, st_scr.at[nxt])
        if not first:
            values(c - 1, pb_scr.at[nxt], alpha_prev)
        m_new, alpha = softmax(st_scr.at[cur], pb_scr.at[cur], cmax, m_prev)
        return cmax_next, m_new, alpha

    acc_scr[...] = jnp.zeros(acc_scr.shape, F32)
    carry = (scores(0, st_scr.at[0]), jnp.full((1, mq), -jnp.inf, F32), jnp.ones((1, mq), F32))
    carry = step(0, 0, 1, carry, first=True)

    n_loop = (nk - 2) // steps

    def body(i, carry):
        c0 = steps * i + 1
        for s in range(steps):
            carry = step(c0 + s, (1 + s) % 2, s % 2, carry)
        return carry

    if n_loop == 1:
        carry = body(0, carry)
    elif n_loop > 1:
        carry = lax.fori_loop(0, n_loop, body, carry)
    for c in range(n_loop * steps + 1, nk - 1):
        carry = step(c, c % 2, (c + 1) % 2, carry)
    _, _, alpha = step(nk - 1, (nk - 1) % 2, nk % 2, carry, last=True)
    values(nk - 1, pb_scr.at[(nk - 1) % 2], alpha)

    out_t = acc_scr[0:dv, :] / acc_scr[dv:dv + 1, :]
    for g in range(groups):
        o_ref[0, :, g * dv:(g + 1) * dv] = out_t[:, g * tq:(g + 1) * tq].T.astype(o_ref.dtype)


def _attention(q, k, vt, *, q_heads, kv_heads, dq, dv, tq):
    batch, seq, _ = q.shape
    groups = q_heads // kv_heads
    tk = TK_ATT
    tv = TV_CHUNK
    nk = seq // tk
    mq = groups * tq
    kern = functools.partial(_attn_kernel, groups=groups, dq=dq, dv=dv, tq=tq, tk=tk, nk=nk,
                             tv=tv, steps=ATT_STEPS)
    return pl.pallas_call(
        kern,
        grid=(batch, kv_heads, seq // tq),
        in_specs=[
            pl.BlockSpec((1, tq, groups * dq), lambda b, h, i: (b, i, h)),
            pl.BlockSpec((1, seq, dq), lambda b, h, i: (b, 0, h)),
            pl.BlockSpec((1, 1, seq // tv, dv, tv), lambda b, h, i: (b, h, 0, 0, 0)),
        ],
        out_specs=pl.BlockSpec((1, tq, groups * dv), lambda b, h, i: (b, i, h)),
        out_shape=jax.ShapeDtypeStruct((batch, seq, q_heads * dv), BF16),
        scratch_shapes=[
            pltpu.VMEM((dq, mq), BF16),
            pltpu.VMEM((2, tk, mq), F32),
            pltpu.VMEM((2, tk, mq), BF16),
            pltpu.VMEM((dv + SUM_ROWS, mq), F32),
        ],
        compiler_params=pltpu.CompilerParams(
            dimension_semantics=("arbitrary", "arbitrary", "arbitrary"),
            vmem_limit_bytes=VMEM_LIMIT),
        name=f"attn_g{groups}",
    )(q, k, vt)


def _merge_kernel(x_ref, mod_ref, yg_ref, ym_ref, wga_ref, wgb_ref, bga_ref, bgb_ref,
                  wbg_ref, wbm_ref, wo_ref, g_ref, b_ref, o_ref, h_scr, acc_scr):
    j = pl.program_id(1)

    @pl.when(j == 0)
    def _():
        shift = mod_ref[0, 0:1, :]
        scale = mod_ref[0, 1:2, :]
        h_scr[...] = (_ln(x_ref[...]) * (1.0 + scale) + shift).astype(BF16)
        acc_scr[...] = jnp.zeros(acc_scr.shape, F32)

    h = h_scr[...]
    la = jnp.dot(h, wga_ref[...], preferred_element_type=F32) + bga_ref[...]
    lb = jnp.dot(h, wgb_ref[...], preferred_element_type=F32) + bgb_ref[...]
    a = jnp.dot(yg_ref[...], wbg_ref[...], preferred_element_type=F32)
    b = jnp.dot(ym_ref[...], wbm_ref[...], preferred_element_type=F32)
    merged = jax.nn.sigmoid(la) * a + jax.nn.sigmoid(lb) * b
    acc_scr[...] += jnp.dot(merged.astype(BF16), wo_ref[...], preferred_element_type=F32)

    @pl.when(j == pl.num_programs(1) - 1)
    def _():
        gate = mod_ref[0, 2:3, :]
        r = DEEPNORM_ALPHA * x_ref[...] + gate * acc_scr[...]
        o_ref[...] = _ln(r) * g_ref[...] + b_ref[...]


def _merge(x2, mod, yg, ym, w_gate, b_gate, wbg, wbm, wo, ln_g, ln_b, seq):
    t = x2.shape[0]
    tm, tn = TM_MERGE, TN_MERGE
    nt = seq // tm
    nj = D_MODEL // tn
    tok = lambda i, j: (i, 0)
    return pl.pallas_call(
        _merge_kernel,
        grid=(t // tm, nj),
        in_specs=[
            pl.BlockSpec((tm, D_MODEL), tok),
            pl.BlockSpec((1, 6, D_MODEL), lambda i, j: (i // nt, 0, 0)),
            pl.BlockSpec((tm, yg.shape[1]), tok),
            pl.BlockSpec((tm, ym.shape[1]), tok),
            pl.BlockSpec((D_MODEL, tn), lambda i, j: (0, j)),
            pl.BlockSpec((D_MODEL, tn), lambda i, j: (0, j + nj)),
            pl.BlockSpec((1, tn), lambda i, j: (0, j)),
            pl.BlockSpec((1, tn), lambda i, j: (0, j + nj)),
            pl.BlockSpec((wbg.shape[0], tn), lambda i, j: (0, j)),
            pl.BlockSpec((wbm.shape[0], tn), lambda i, j: (0, j)),
            pl.BlockSpec((tn, D_MODEL), lambda i, j: (j, 0)),
            pl.BlockSpec((1, D_MODEL), lambda i, j: (0, 0)),
            pl.BlockSpec((1, D_MODEL), lambda i, j: (0, 0)),
        ],
        out_specs=pl.BlockSpec((tm, D_MODEL), tok),
        out_shape=jax.ShapeDtypeStruct((t, D_MODEL), F32),
        scratch_shapes=[pltpu.VMEM((tm, D_MODEL), BF16), pltpu.VMEM((tm, D_MODEL), F32)],
        compiler_params=pltpu.CompilerParams(
            dimension_semantics=("arbitrary", "arbitrary"), vmem_limit_bytes=VMEM_LIMIT),
        name="merge",
    )(x2, mod, yg, ym, w_gate, w_gate, b_gate, b_gate, wbg, wbm, wo, ln_g, ln_b)


def _ffn_kernel(x_ref, mod_ref, wg_ref, wu_ref, wd_ref, g_ref, b_ref, o_ref, h_scr, acc_scr):
    j = pl.program_id(1)

    @pl.when(j == 0)
    def _():
        shift = mod_ref[0, 3:4, :]
        scale = mod_ref[0, 4:5, :]
        h_scr[...] = (_ln(x_ref[...]) * (1.0 + scale) + shift).astype(BF16)
        acc_scr[...] = jnp.zeros(acc_scr.shape, F32)

    h = h_scr[...]
    gt = jnp.dot(h, wg_ref[...], preferred_element_type=F32)
    up = jnp.dot(h, wu_ref[...], preferred_element_type=F32)
    a = (gt * jax.nn.sigmoid(gt) * up).astype(BF16)
    acc_scr[...] += jnp.dot(a, wd_ref[...], preferred_element_type=F32)

    @pl.when(j == pl.num_programs(1) - 1)
    def _():
        gate = mod_ref[0, 5:6, :]
        r = DEEPNORM_ALPHA * x_ref[...] + gate * acc_scr[...]
        o_ref[...] = _ln(r) * g_ref[...] + b_ref[...]


def _ffn(x1, mod, wg, wu, wd, ln_g, ln_b, seq):
    t = x1.shape[0]
    tm, tf = TM_FFN, TF_FFN
    nt = seq // tm
    tok = lambda i, j: (i, 0)
    return pl.pallas_call(
        _ffn_kernel,
        grid=(t // tm, D_FF // tf),
        in_specs=[
            pl.BlockSpec((tm, D_MODEL), tok),
            pl.BlockSpec((1, 6, D_MODEL), lambda i, j: (i // nt, 0, 0)),
            pl.BlockSpec((D_MODEL, tf), lambda i, j: (0, j)),
            pl.BlockSpec((D_MODEL, tf), lambda i, j: (0, j)),
            pl.BlockSpec((tf, D_MODEL), lambda i, j: (j, 0)),
            pl.BlockSpec((1, D_MODEL), lambda i, j: (0, 0)),
            pl.BlockSpec((1, D_MODEL), lambda i, j: (0, 0)),
        ],
        out_specs=pl.BlockSpec((tm, D_MODEL), tok),
        out_shape=jax.ShapeDtypeStruct((t, D_MODEL), F32),
        scratch_shapes=[pltpu.VMEM((tm, D_MODEL), BF16), pltpu.VMEM((tm, D_MODEL), F32)],
        compiler_params=pltpu.CompilerParams(
            dimension_semantics=("arbitrary", "arbitrary"), vmem_limit_bytes=VMEM_LIMIT),
        name="ffn",
    )(x1, mod, wg, wu, wd, ln_g, ln_b)


def _rope_tables(seq, dim):
    rows = seq // GRID_W
    quarter = dim // 4
    inv_freq = ROPE_THETA ** (-jnp.arange(quarter, dtype=F32) / quarter)
    row_ang = jnp.arange(rows, dtype=F32)[:, None] * inv_freq
    col_ang = jnp.arange(GRID_W, dtype=F32)[:, None] * inv_freq
    ang = jnp.concatenate([
        jnp.broadcast_to(row_ang[:, None, :], (rows, GRID_W, quarter)),
        jnp.broadcast_to(col_ang[None, :, :], (rows, GRID_W, quarter)),
    ], axis=-1).reshape(seq, 2 * quarter)
    half = dim // 2
    pad = jnp.zeros((seq, LANES // 2 - half), F32)
    cos, sin = jnp.cos(ang), jnp.sin(ang)
    c = jnp.concatenate([cos, pad, cos, pad], axis=-1)
    s = jnp.concatenate([-sin, pad, sin, pad], axis=-1)
    return c, s


def _deinterleave(n):
    return np.concatenate([np.arange(0, n, 2), np.arange(1, n, 2)])


def kernel(x, c, w_ada, b_ada, w_in, b_gates, gqa_q_gain, gqa_k_gain, mla_q_gain, mla_kv_gain,
           w_mla_uq, w_mla_ukv, w_branch_gqa, w_branch_mla, w_out, ln1_g, ln1_b,
           w_ffn_gate, w_ffn_up, w_ffn_down, ln2_g, ln2_b):
    batch, seq, d = x.shape
    assert d == D_MODEL and w_ada.shape[0] == DEPTH
    t = batch * seq
    x2 = x.reshape(t, d)

    cg, sg = _rope_tables(seq, HEAD_DIM)
    cm, sm = _rope_tables(seq, MLA_ROPE_DIM)
    perm_head = _deinterleave(HEAD_DIM)
    zeros32 = lambda rows: jnp.zeros((rows, LANES // 2 - MLA_ROPE_DIM // 2), BF16)

    c_pad = jnp.zeros((8, d), F32).at[:batch].set(c)

    for l in range(DEPTH):
        mod = _adaln(c_pad, w_ada[l], b_ada[l][None, :])[:batch].reshape(batch, 6, d)

        wl = w_in[l]
        o0 = 0
        wq = wl[:, o0:o0 + GQA_Q_COLS].reshape(d, GQA_Q_HEADS, HEAD_DIM)[:, :, perm_head]
        o0 += GQA_Q_COLS
        wk = wl[:, o0:o0 + GQA_KV_COLS].reshape(d, GQA_KV_HEADS, HEAD_DIM)[:, :, perm_head]
        o0 += GQA_KV_COLS
        wv = wl[:, o0:o0 + GQA_KV_COLS]
        o0 += GQA_KV_COLS
        wql = wl[:, o0:o0 + MLA_Q_RANK]
        o0 += MLA_Q_RANK
        wkvl = wl[:, o0:o0 + MLA_KV_RANK]
        o0 += MLA_KV_RANK
        wkr = wl[:, o0:o0 + MLA_ROPE_DIM].astype(BF16)
        o0 += MLA_ROPE_DIM
        w_gate = wl[:, o0:].astype(BF16)
        wkr_pad = jnp.concatenate(
            [wkr[:, 0::2], zeros32(d), wkr[:, 1::2], zeros32(d)], axis=-1)
        w_attn = jnp.concatenate([
            wq.reshape(d, GQA_Q_COLS).astype(BF16), wk.reshape(d, GQA_KV_COLS).astype(BF16),
            wv.astype(BF16), wql.astype(BF16), wkvl.astype(BF16), wkr_pad], axis=-1)

        uq = w_mla_uq[l].astype(BF16).reshape(MLA_Q_RANK, MLA_HEADS, MLA_NOPE_DIM + MLA_ROPE_DIM)
        uq_r = uq[:, :, MLA_NOPE_DIM:]
        z = jnp.zeros((MLA_Q_RANK, MLA_HEADS, LANES // 2 - MLA_ROPE_DIM // 2), BF16)
        wuq = jnp.concatenate(
            [uq[:, :, :MLA_NOPE_DIM], uq_r[:, :, 0::2], z, uq_r[:, :, 1::2], z],
            axis=-1).reshape(MLA_Q_RANK, MLA_HEADS * MLA_QK_PAD)
        ukv = w_mla_ukv[l].astype(BF16).reshape(MLA_KV_RANK, MLA_HEADS, MLA_NOPE_DIM + MLA_V_DIM)
        wuk = ukv[:, :, :MLA_NOPE_DIM].reshape(MLA_KV_RANK, MLA_HEADS * MLA_NOPE_DIM)
        wuv = ukv[:, :, MLA_NOPE_DIM:].reshape(MLA_KV_RANK, MLA_HEADS * MLA_V_DIM)

        gq = (gqa_q_gain[l][perm_head] * (LOG2E * HEAD_DIM ** -0.5))[None, :]
        gk = gqa_k_gain[l][perm_head][None, :]

        qg, kg, vgt, qm, km, vmt = _inproj(
            x2, mod, cg, sg, cm, sm, w_attn, gq, gk, mla_q_gain[l][None, :],
            mla_kv_gain[l][None, :], wuq, wuk, wuv, batch, seq)

        y_gqa = _attention(
            qg.reshape(batch, seq, GQA_Q_COLS), kg.reshape(batch, seq, GQA_KV_COLS), vgt,
            q_heads=GQA_Q_HEADS, kv_heads=GQA_KV_HEADS, dq=HEAD_DIM, dv=HEAD_DIM, tq=TQ_GQA)
        y_mla = _attention(
            qm.reshape(batch, seq, MLA_HEADS * MLA_QK_PAD),
            km.reshape(batch, seq, MLA_HEADS * MLA_QK_PAD), vmt,
            q_heads=MLA_HEADS, kv_heads=MLA_HEADS, dq=MLA_QK_PAD, dv=MLA_V_DIM, tq=TQ_MLA)

        x2 = _merge(
            x2, mod, y_gqa.reshape(t, GQA_Q_COLS), y_mla.reshape(t, MLA_HEADS * MLA_V_DIM),
            w_gate, b_gates[l][None, :], w_branch_gqa[l].astype(BF16),
            w_branch_mla[l].astype(BF16), w_out[l].astype(BF16),
            ln1_g[l][None, :], ln1_b[l][None, :], seq)

        x2 = _ffn(
            x2, mod, w_ffn_gate[l].astype(BF16), w_ffn_up[l].astype(BF16),
            w_ffn_down[l].astype(BF16), ln2_g[l][None, :], ln2_b[l][None, :], seq)

    return x2.reshape(batch, seq, d)
```

```python
import functools

import jax
import jax.numpy as jnp
import numpy as np
from jax import lax
from jax.experimental import pallas as pl
from jax.experimental.pallas import tpu as pltpu

D_MODEL = 2048
GRID_W = 64
ROPE_THETA = 10000.0
HEAD_DIM = 128
GQA_Q_HEADS = 8
GQA_KV_HEADS = 2
MLA_HEADS = 8
MLA_Q_RANK = 512
MLA_KV_RANK = 512
MLA_NOPE_DIM = 128
MLA_ROPE_DIM = 64
MLA_V_DIM = 128
D_FF = 5632
DEPTH = 1
DEEPNORM_ALPHA = (2.0 * DEPTH) ** 0.25
LN_EPS = 1e-5
RMS_EPS = 1e-6

GQA_Q_COLS = GQA_Q_HEADS * HEAD_DIM
GQA_KV_COLS = GQA_KV_HEADS * HEAD_DIM
MLA_QK_PAD = 256
LANES = 128
LOG2E = 1.4426950408889634

VMEM_LIMIT = 56 * 1024 * 1024

TM_IN = 256
TV_CHUNK = TM_IN
TK_ATT = 256
ATT_STEPS = 6
COL_BLOCK = 512
SUM_ROWS = 16
EXP_ROWS = 64
TQ_GQA = 1024
TQ_MLA = 4096
TM_MERGE = 512
TN_MERGE = 512
TM_FFN = 512
TF_FFN = 512

BF16 = jnp.bfloat16
F32 = jnp.float32


def _const_spec(shape):
    nd = len(shape)
    return pl.BlockSpec(shape, lambda *_: (0,) * nd, pipeline_mode=pl.Buffered(1))


def _ln(x):
    mu = jnp.mean(x, axis=-1, keepdims=True)
    xc = x - mu
    var = jnp.mean(xc * xc, axis=-1, keepdims=True)
    return xc * lax.rsqrt(var + LN_EPS)


def _rms(x, gain):
    ms = jnp.mean(x * x, axis=-1, keepdims=True)
    return x * lax.rsqrt(ms + RMS_EPS) * gain


def _rope(x, c, s):
    return x * c + pltpu.roll(x, LANES // 2, 1) * s


def _adaln_kernel(c_ref, w_ref, b_ref, o_ref):
    c = c_ref[...]
    act = (c * jax.nn.sigmoid(c)).astype(BF16)
    o_ref[...] = jnp.dot(act, w_ref[...].astype(BF16), preferred_element_type=F32) + b_ref[...]


def _adaln(c_pad, w_ada, b_ada):
    n = w_ada.shape[1]
    tn = 1024
    return pl.pallas_call(
        _adaln_kernel,
        grid=(n // tn,),
        in_specs=[
            pl.BlockSpec((8, D_MODEL), lambda j: (0, 0)),
            pl.BlockSpec((D_MODEL, tn), lambda j: (0, j)),
            pl.BlockSpec((1, tn), lambda j: (0, j)),
        ],
        out_specs=pl.BlockSpec((8, tn), lambda j: (0, j)),
        out_shape=jax.ShapeDtypeStruct((8, n), F32),
        compiler_params=pltpu.CompilerParams(
            dimension_semantics=("arbitrary",), vmem_limit_bytes=VMEM_LIMIT),
        name="adaln",
    )(c_pad, w_ada, b_ada)


def _inproj_kernel(x_ref, mod_ref, cg_ref, sg_ref, cm_ref, sm_ref, w_ref,
                   gq_ref, gk_ref, gql_ref, gkvl_ref, wuq_ref, wuk_ref, wuv_ref,
                   qg_ref, kg_ref, vgt_ref, qm_ref, km_ref, vmt_ref, *, mla_scale):
    x = x_ref[...]
    shift = mod_ref[0, 0:1, :]
    scale = mod_ref[0, 1:2, :]
    h = (_ln(x) * (1.0 + scale) + shift).astype(BF16)
    proj = jnp.dot(h, w_ref[...], preferred_element_type=F32)

    cg = cg_ref[...]
    sg = sg_ref[...]
    cm = cm_ref[...]
    sm = sm_ref[...]

    for hd in range(GQA_Q_HEADS):
        q = proj[:, hd * HEAD_DIM:(hd + 1) * HEAD_DIM]
        q = _rope(_rms(q, gq_ref[...]), cg, sg)
        qg_ref[:, hd * HEAD_DIM:(hd + 1) * HEAD_DIM] = q.astype(BF16)
    off = GQA_Q_COLS
    for hd in range(GQA_KV_HEADS):
        k = proj[:, off + hd * HEAD_DIM: off + (hd + 1) * HEAD_DIM]
        k = _rope(_rms(k, gk_ref[...]), cg, sg)
        kg_ref[:, hd * HEAD_DIM:(hd + 1) * HEAD_DIM] = k.astype(BF16)
    off += GQA_KV_COLS
    for hd in range(GQA_KV_HEADS):
        v = proj[:, off + hd * HEAD_DIM: off + (hd + 1) * HEAD_DIM]
        vgt_ref[0, hd, 0] = v.T.astype(BF16)
    off += GQA_KV_COLS

    q_lat = _rms(proj[:, off: off + MLA_Q_RANK], gql_ref[...]).astype(BF16)
    off += MLA_Q_RANK
    kv_lat = _rms(proj[:, off: off + MLA_KV_RANK], gkvl_ref[...]).astype(BF16)
    off += MLA_KV_RANK
    k_rope = _rope(proj[:, off: off + LANES], cm, sm).astype(BF16)

    q_m = jnp.dot(q_lat, wuq_ref[...], preferred_element_type=F32) * mla_scale
    k_n = jnp.dot(kv_lat, wuk_ref[...], preferred_element_type=F32)
    v_m = jnp.dot(kv_lat, wuv_ref[...], preferred_element_type=F32)
    for hd in range(MLA_HEADS):
        b0 = hd * MLA_QK_PAD
        qm_ref[:, b0: b0 + LANES] = q_m[:, b0: b0 + LANES].astype(BF16)
        qm_ref[:, b0 + LANES: b0 + 2 * LANES] = _rope(
            q_m[:, b0 + LANES: b0 + 2 * LANES], cm, sm).astype(BF16)
        km_ref[:, b0: b0 + LANES] = k_n[:, hd * LANES:(hd + 1) * LANES].astype(BF16)
        km_ref[:, b0 + LANES: b0 + 2 * LANES] = k_rope
        vmt_ref[0, hd, 0] = v_m[:, hd * MLA_V_DIM:(hd + 1) * MLA_V_DIM].T.astype(BF16)


def _inproj(x2, mod, cg, sg, cm, sm, w_attn, gq, gk, gql, gkvl, wuq, wuk, wuv, batch, seq):
    t = x2.shape[0]
    tm = TM_IN
    nt = seq // tm
    n_attn = w_attn.shape[1]
    kern = functools.partial(
        _inproj_kernel, mla_scale=float(LOG2E * (MLA_NOPE_DIM + MLA_ROPE_DIM) ** -0.5))
    tok = lambda i: (i, 0)
    pos = lambda i: (i % nt, 0)
    in_specs = [
        pl.BlockSpec((tm, D_MODEL), tok),
        pl.BlockSpec((1, 6, D_MODEL), lambda i: (i // nt, 0, 0)),
        pl.BlockSpec((tm, LANES), pos),
        pl.BlockSpec((tm, LANES), pos),
        pl.BlockSpec((tm, LANES), pos),
        pl.BlockSpec((tm, LANES), pos),
        _const_spec((D_MODEL, n_attn)),
        _const_spec((1, HEAD_DIM)),
        _const_spec((1, HEAD_DIM)),
        _const_spec((1, MLA_Q_RANK)),
        _const_spec((1, MLA_KV_RANK)),
        _const_spec(wuq.shape),
        _const_spec(wuk.shape),
        _const_spec(wuv.shape),
    ]
    out_shape = [
        jax.ShapeDtypeStruct((t, GQA_Q_COLS), BF16),
        jax.ShapeDtypeStruct((t, GQA_KV_COLS), BF16),
        jax.ShapeDtypeStruct((batch, GQA_KV_HEADS, seq // TV_CHUNK, HEAD_DIM, TV_CHUNK), BF16),
        jax.ShapeDtypeStruct((t, MLA_HEADS * MLA_QK_PAD), BF16),
        jax.ShapeDtypeStruct((t, MLA_HEADS * MLA_QK_PAD), BF16),
        jax.ShapeDtypeStruct((batch, MLA_HEADS, seq // TV_CHUNK, MLA_V_DIM, TV_CHUNK), BF16),
    ]
    vt_map = lambda i: (i // nt, 0, i % nt, 0, 0)
    out_specs = [
        pl.BlockSpec((tm, GQA_Q_COLS), tok),
        pl.BlockSpec((tm, GQA_KV_COLS), tok),
        pl.BlockSpec((1, GQA_KV_HEADS, 1, HEAD_DIM, TV_CHUNK), vt_map),
        pl.BlockSpec((tm, MLA_HEADS * MLA_QK_PAD), tok),
        pl.BlockSpec((tm, MLA_HEADS * MLA_QK_PAD), tok),
        pl.BlockSpec((1, MLA_HEADS, 1, MLA_V_DIM, TV_CHUNK), vt_map),
    ]
    return pl.pallas_call(
        kern,
        grid=(t // tm,),
        in_specs=in_specs,
        out_specs=out_specs,
        out_shape=out_shape,
        compiler_params=pltpu.CompilerParams(
            dimension_semantics=("arbitrary",), vmem_limit_bytes=VMEM_LIMIT),
        name="inproj",
    )(x2, mod, cg, sg, cm, sm, w_attn, gq, gk, gql, gkvl, wuq, wuk, wuv)


def _attn_kernel(q_ref, k_ref, vt_ref, o_ref, qt_scr, st_scr, pb_scr, acc_scr,
                 *, groups, dq, dv, tq, tk, nk, tv, steps):
    mq = groups * tq
    nv = tk // tv
    cb = COL_BLOCK
    nb = mq // cb
    cols = [slice(n * cb, (n + 1) * cb) for n in range(nb)]
    grp = [(n * cb) // tq for n in range(nb)]
    tok = [(n * cb) % tq for n in range(nb)]

    def load_k(c):
        return k_ref[0, pl.ds(pl.multiple_of(c * tk, tk), tk), :]

    def scores(kc, st_ref, n):
        st = jnp.dot(kc, qt_scr[:, cols[n]], preferred_element_type=F32)
        st_ref[:, cols[n]] = st
        return jnp.max(st, axis=0, keepdims=True)

    ones_rows = jnp.ones((SUM_ROWS, tk), BF16)

    def load_vt(c):
        vt = jnp.concatenate([vt_ref[0, 0, c * nv + j] for j in range(nv)], axis=1)
        return jnp.concatenate([vt, ones_rows], axis=0)

    def values(vt, pb_ref, alpha, n):
        acc_scr[:, cols[n]] = alpha * acc_scr[:, cols[n]] + jnp.dot(
            vt, pb_ref[:, cols[n]], preferred_element_type=F32)

    def softmax(st_ref, pb_ref, cmax, m_prev, n):
        m_new = jnp.maximum(m_prev, cmax)
        alpha = jnp.exp2(m_prev - m_new)
        for r in range(tk // EXP_ROWS):
            rows = slice(r * EXP_ROWS, (r + 1) * EXP_ROWS)
            pb_ref[rows, cols[n]] = jnp.exp2((st_ref[rows, cols[n]] - m_new).astype(BF16))
        return m_new, alpha

    def step(c, cur, nxt, carry, first=False, last=False):
        cmax, m_prev, alpha_prev = carry
        kc = None if last else load_k(c + 1)
        vt = None if first else load_vt(c - 1)
        cmax_next, m_new, alpha = list(cmax), [], []
        for n in range(nb):
            if not last:
                cmax_next[n] = scores(kc, st_scr.at[nxt], n)
        for n in range(nb):
            m_n, a_n = softmax(st_scr.at[cur], pb_scr.at[cur], cmax[n], m_prev[n], n)
            m_new.append(m_n)
            alpha.append(a_n)
            if not first:
                values(vt, pb_scr.at[nxt], alpha_prev[n], n)
        return cmax_next, m_new, alpha

    acc_scr[...] = jnp.zeros(acc_scr.shape, F32)
    k0 = load_k(0)
    cmax0 = []
    for n in range(nb):
        qn = q_ref[0, tok[n]:tok[n] + cb, grp[n] * dq:(grp[n] + 1) * dq].astype(F32)
        qt_scr[:, cols[n]] = qn.T.astype(BF16)
        cmax0.append(scores(k0, st_scr.at[0], n))
    carry = (cmax0, [jnp.full((1, cb), -jnp.inf, F32)] * nb, [jnp.ones((1, cb), F32)] * nb)
    carry = step(0, 0, 1, carry, first=True)

    n_loop = (nk - 2) // steps

    def body(i, carry):
        c0 = steps * i + 1
        for s in range(steps):
            carry = step(c0 + s, (1 + s) % 2, s % 2, carry)
        return carry

    if n_loop == 1:
        carry = body(0, carry)
    elif n_loop > 1:
        carry = lax.fori_loop(0, n_loop, body, carry)
    for c in range(n_loop * steps + 1, nk - 1):
        carry = step(c, c % 2, (c + 1) % 2, carry)
    _, _, alpha = step(nk - 1, (nk - 1) % 2, nk % 2, carry, last=True)
    vt = load_vt(nk - 1)
    for n in range(nb):
        values(vt, pb_scr.at[(nk - 1) % 2], alpha[n], n)
        out_t = acc_scr[0:dv, cols[n]] / acc_scr[dv:dv + 1, cols[n]]
        o_ref[0, tok[n]:tok[n] + cb, grp[n] * dv:(grp[n] + 1) * dv] = out_t.T.astype(o_ref.dtype)


def _attention(q, k, vt, *, q_heads, kv_heads, dq, dv, tq):
    batch, seq, _ = q.shape
    groups = q_heads // kv_heads
    tk = TK_ATT
    tv = TV_CHUNK
    nk = seq // tk
    mq = groups * tq
    kern = functools.partial(_attn_kernel, groups=groups, dq=dq, dv=dv, tq=tq, tk=tk, nk=nk,
                             tv=tv, steps=ATT_STEPS)
    return pl.pallas_call(
        kern,
        grid=(batch, kv_heads, seq // tq),
        in_specs=[
            pl.BlockSpec((1, tq, groups * dq), lambda b, h, i: (b, i, h)),
            pl.BlockSpec((1, seq, dq), lambda b, h, i: (b, 0, h)),
            pl.BlockSpec((1, 1, seq // tv, dv, tv), lambda b, h, i: (b, h, 0, 0, 0)),
        ],
        out_specs=pl.BlockSpec((1, tq, groups * dv), lambda b, h, i: (b, i, h)),
        out_shape=jax.ShapeDtypeStruct((batch, seq, q_heads * dv), BF16),
        scratch_shapes=[
            pltpu.VMEM((dq, mq), BF16),
            pltpu.VMEM((2, tk, mq), F32),
            pltpu.VMEM((2, tk, mq), BF16),
            pltpu.VMEM((dv + SUM_ROWS, mq), F32),
        ],
        compiler_params=pltpu.CompilerParams(
            dimension_semantics=("arbitrary", "arbitrary", "arbitrary"),
            vmem_limit_bytes=VMEM_LIMIT),
        name=f"attn_g{groups}",
    )(q, k, vt)


def _sublayer_kernel(*refs, chunk_fn, n_mid, mod_rows, rows, row_blocks):
    x0_ref, xn_ref, xp_ref, mod0_ref, modn_ref, modp_ref = refs[:6]
    mid_refs = refs[6:6 + n_mid]
    g_ref, b_ref, o_ref, h0_scr, h1_scr, acc0_scr, acc1_scr = refs[6 + n_mid:]
    shift_row, scale_row, gate_row = mod_rows
    i = pl.program_id(0)
    j = pl.program_id(1)

    def modulated_ln(x, mod_ref):
        scale = mod_ref[0, scale_row:scale_row + 1, :]
        shift = mod_ref[0, shift_row:shift_row + 1, :]
        return (_ln(x) * (1.0 + scale) + shift).astype(BF16)

    @pl.when((i == 0) & (j == 0))
    def _():
        h0_scr[...] = modulated_ln(x0_ref[...], mod0_ref)
        acc0_scr[...] = jnp.zeros(acc0_scr.shape, F32)
        acc1_scr[...] = jnp.zeros(acc1_scr.shape, F32)

    def stages(h_cur, h_nxt, acc_cur, acc_prev):
        rs = pl.ds(pl.multiple_of(jnp.minimum(j, row_blocks - 1) * rows, rows), rows)
        h_nxt[rs, :] = modulated_ln(xn_ref[...], modn_ref)
        gate = modp_ref[0, gate_row:gate_row + 1, :]
        r = DEEPNORM_ALPHA * xp_ref[...] + gate * acc_prev[rs, :]
        o_ref[...] = _ln(r) * g_ref[...] + b_ref[...]

        part = chunk_fn(h_cur[...], *mid_refs)
        acc_cur[...] = jnp.where(j == 0, part, acc_cur[...] + part)

    @pl.when(i % 2 == 0)
    def _():
        stages(h0_scr, h1_scr, acc0_scr, acc1_scr)

    @pl.when(i % 2 == 1)
    def _():
        stages(h1_scr, h0_scr, acc1_scr, acc0_scr)


def _sublayer(x2, mod, mid_args, mid_specs, ln_g, ln_b, *, chunk_fn, mod_rows, tm, nj, seq, name):
    t = x2.shape[0]
    nt = t // tm
    per_batch = seq // tm
    rows = next(r for r in range(16, tm + 1, 16) if tm % r == 0 and tm // r <= nj)
    rb = tm // rows
    nxt = lambda i: jnp.minimum(i + 1, nt - 1)
    prv = lambda i: jnp.maximum(i - 1, 0)
    blk = lambda j: jnp.minimum(j, rb - 1)
    kern = functools.partial(_sublayer_kernel, chunk_fn=chunk_fn, n_mid=len(mid_args),
                             mod_rows=mod_rows, rows=rows, row_blocks=rb)
    const2 = lambda i, j: (0, 0)
    return pl.pallas_call(
        kern,
        grid=(nt + 1, nj),
        in_specs=[
            pl.BlockSpec((tm, D_MODEL), const2, pipeline_mode=pl.Buffered(1)),
            pl.BlockSpec((rows, D_MODEL), lambda i, j: (nxt(i) * rb + blk(j), 0)),
            pl.BlockSpec((rows, D_MODEL), lambda i, j: (prv(i) * rb + blk(j), 0)),
            pl.BlockSpec((1, 6, D_MODEL), lambda i, j: (0, 0, 0)),
            pl.BlockSpec((1, 6, D_MODEL), lambda i, j: (nxt(i) // per_batch, 0, 0)),
            pl.BlockSpec((1, 6, D_MODEL), lambda i, j: (prv(i) // per_batch, 0, 0)),
            *[pl.BlockSpec(shape, functools.partial(
                lambda i, j, f: f(jnp.minimum(i, nt - 1), j), f=imap))
              for shape, imap in mid_specs],
            pl.BlockSpec((1, D_MODEL), const2),
            pl.BlockSpec((1, D_MODEL), const2),
        ],
        out_specs=pl.BlockSpec((rows, D_MODEL), lambda i, j: (prv(i) * rb + blk(j), 0)),
        out_shape=jax.ShapeDtypeStruct((t, D_MODEL), F32),
        scratch_shapes=[pltpu.VMEM((tm, D_MODEL), BF16), pltpu.VMEM((tm, D_MODEL), BF16),
                        pltpu.VMEM((tm, D_MODEL), F32), pltpu.VMEM((tm, D_MODEL), F32)],
        compiler_params=pltpu.CompilerParams(
            dimension_semantics=("arbitrary", "arbitrary"), vmem_limit_bytes=VMEM_LIMIT),
        name=name,
    )(x2, x2, x2, mod, mod, mod, *mid_args, ln_g, ln_b)


def _merge_chunk(h, yg_ref, ym_ref, wga_ref, wgb_ref, bga_ref, bgb_ref, wbg_ref, wbm_ref, wo_ref):
    la = jnp.dot(h, wga_ref[...], preferred_element_type=F32) + bga_ref[...]
    lb = jnp.dot(h, wgb_ref[...], preferred_element_type=F32) + bgb_ref[...]
    a = jnp.dot(yg_ref[...], wbg_ref[...], preferred_element_type=F32)
    b = jnp.dot(ym_ref[...], wbm_ref[...], preferred_element_type=F32)
    merged = jax.nn.sigmoid(la) * a + jax.nn.sigmoid(lb) * b
    return jnp.dot(merged.astype(BF16), wo_ref[...], preferred_element_type=F32)


def _merge(x2, mod, yg, ym, w_gate, b_gate, wbg, wbm, wo, ln_g, ln_b, seq):
    tm, tn = TM_MERGE, TN_MERGE
    nj = D_MODEL // tn
    mid_specs = [
        ((tm, yg.shape[1]), lambda i, j: (i, 0)),
        ((tm, ym.shape[1]), lambda i, j: (i, 0)),
        ((D_MODEL, tn), lambda i, j: (0, j)),
        ((D_MODEL, tn), lambda i, j: (0, j + nj)),
        ((1, tn), lambda i, j: (0, j)),
        ((1, tn), lambda i, j: (0, j + nj)),
        ((wbg.shape[0], tn), lambda i, j: (0, j)),
        ((wbm.shape[0], tn), lambda i, j: (0, j)),
        ((tn, D_MODEL), lambda i, j: (j, 0)),
    ]
    return _sublayer(x2, mod, (yg, ym, w_gate, w_gate, b_gate, b_gate, wbg, wbm, wo), mid_specs,
                     ln_g, ln_b, chunk_fn=_merge_chunk, mod_rows=(0, 1, 2), tm=tm, nj=nj,
                     seq=seq, name="merge")


def _ffn_chunk(h, wg_ref, wu_ref, wd_ref):
    gt = jnp.dot(h, wg_ref[...], preferred_element_type=F32)
    up = jnp.dot(h, wu_ref[...], preferred_element_type=F32)
    a = (gt * jax.nn.sigmoid(gt) * up).astype(BF16)
    return jnp.dot(a, wd_ref[...], preferred_element_type=F32)


def _ffn(x1, mod, wg, wu, wd, ln_g, ln_b, seq):
    tm, tf = TM_FFN, TF_FFN
    mid_specs = [
        ((D_MODEL, tf), lambda i, j: (0, j)),
        ((D_MODEL, tf), lambda i, j: (0, j)),
        ((tf, D_MODEL), lambda i, j: (j, 0)),
    ]
    return _sublayer(x1, mod, (wg, wu, wd), mid_specs, ln_g, ln_b, chunk_fn=_ffn_chunk,
                     mod_rows=(3, 4, 5), tm=tm, nj=D_FF // tf, seq=seq, name="ffn")


def _rope_tables(seq, dim):
    rows = seq // GRID_W
    quarter = dim // 4
    inv_freq = ROPE_THETA ** (-jnp.arange(quarter, dtype=F32) / quarter)
    row_ang = jnp.arange(rows, dtype=F32)[:, None] * inv_freq
    col_ang = jnp.arange(GRID_W, dtype=F32)[:, None] * inv_freq
    ang = jnp.concatenate([
        jnp.broadcast_to(row_ang[:, None, :], (rows, GRID_W, quarter)),
        jnp.broadcast_to(col_ang[None, :, :], (rows, GRID_W, quarter)),
    ], axis=-1).reshape(seq, 2 * quarter)
    half = dim // 2
    pad = jnp.zeros((seq, LANES // 2 - half), F32)
    cos, sin = jnp.cos(ang), jnp.sin(ang)
    c = jnp.concatenate([cos, pad, cos, pad], axis=-1)
    s = jnp.concatenate([-sin, pad, sin, pad], axis=-1)
    return c, s


def _deinterleave(n):
    return np.concatenate([np.arange(0, n, 2), np.arange(1, n, 2)])


def kernel(x, c, w_ada, b_ada, w_in, b_gates, gqa_q_gain, gqa_k_gain, mla_q_gain, mla_kv_gain,
           w_mla_uq, w_mla_ukv, w_branch_gqa, w_branch_mla, w_out, ln1_g, ln1_b,
           w_ffn_gate, w_ffn_up, w_ffn_down, ln2_g, ln2_b):
    batch, seq, d = x.shape
    assert d == D_MODEL and w_ada.shape[0] == DEPTH
    t = batch * seq
    x2 = x.reshape(t, d)

    cg, sg = _rope_tables(seq, HEAD_DIM)
    cm, sm = _rope_tables(seq, MLA_ROPE_DIM)
    perm_head = _deinterleave(HEAD_DIM)
    zeros32 = lambda rows: jnp.zeros((rows, LANES // 2 - MLA_ROPE_DIM // 2), BF16)

    c_pad = jnp.zeros((8, d), F32).at[:batch].set(c)

    for l in range(DEPTH):
        mod = _adaln(c_pad, w_ada[l], b_ada[l][None, :])[:batch].reshape(batch, 6, d)

        wl = w_in[l]
        o0 = 0
        wq = wl[:, o0:o0 + GQA_Q_COLS].reshape(d, GQA_Q_HEADS, HEAD_DIM)[:, :, perm_head]
        o0 += GQA_Q_COLS
        wk = wl[:, o0:o0 + GQA_KV_COLS].reshape(d, GQA_KV_HEADS, HEAD_DIM)[:, :, perm_head]
        o0 += GQA_KV_COLS
        wv = wl[:, o0:o0 + GQA_KV_COLS]
        o0 += GQA_KV_COLS
        wql = wl[:, o0:o0 + MLA_Q_RANK]
        o0 += MLA_Q_RANK
        wkvl = wl[:, o0:o0 + MLA_KV_RANK]
        o0 += MLA_KV_RANK
        wkr = wl[:, o0:o0 + MLA_ROPE_DIM].astype(BF16)
        o0 += MLA_ROPE_DIM
        w_gate = wl[:, o0:].astype(BF16)
        wkr_pad = jnp.concatenate(
            [wkr[:, 0::2], zeros32(d), wkr[:, 1::2], zeros32(d)], axis=-1)
        w_attn = jnp.concatenate([
            wq.reshape(d, GQA_Q_COLS).astype(BF16), wk.reshape(d, GQA_KV_COLS).astype(BF16),
            wv.astype(BF16), wql.astype(BF16), wkvl.astype(BF16), wkr_pad], axis=-1)

        uq = w_mla_uq[l].astype(BF16).reshape(MLA_Q_RANK, MLA_HEADS, MLA_NOPE_DIM + MLA_ROPE_DIM)
        uq_r = uq[:, :, MLA_NOPE_DIM:]
        z = jnp.zeros((MLA_Q_RANK, MLA_HEADS, LANES // 2 - MLA_ROPE_DIM // 2), BF16)
        wuq = jnp.concatenate(
            [uq[:, :, :MLA_NOPE_DIM], uq_r[:, :, 0::2], z, uq_r[:, :, 1::2], z],
            axis=-1).reshape(MLA_Q_RANK, MLA_HEADS * MLA_QK_PAD)
        ukv = w_mla_ukv[l].astype(BF16).reshape(MLA_KV_RANK, MLA_HEADS, MLA_NOPE_DIM + MLA_V_DIM)
        wuk = ukv[:, :, :MLA_NOPE_DIM].reshape(MLA_KV_RANK, MLA_HEADS * MLA_NOPE_DIM)
        wuv = ukv[:, :, MLA_NOPE_DIM:].reshape(MLA_KV_RANK, MLA_HEADS * MLA_V_DIM)

        gq = (gqa_q_gain[l][perm_head] * (LOG2E * HEAD_DIM ** -0.5))[None, :]
        gk = gqa_k_gain[l][perm_head][None, :]

        qg, kg, vgt, qm, km, vmt = _inproj(
            x2, mod, cg, sg, cm, sm, w_attn, gq, gk, mla_q_gain[l][None, :],
            mla_kv_gain[l][None, :], wuq, wuk, wuv, batch, seq)

        y_gqa = _attention(
            qg.reshape(batch, seq, GQA_Q_COLS), kg.reshape(batch, seq, GQA_KV_COLS), vgt,
            q_heads=GQA_Q_HEADS, kv_heads=GQA_KV_HEADS, dq=HEAD_DIM, dv=HEAD_DIM, tq=TQ_GQA)
        y_mla = _attention(
            qm.reshape(batch, seq, MLA_HEADS * MLA_QK_PAD),
            km.reshape(batch, seq, MLA_HEADS * MLA_QK_PAD), vmt,
            q_heads=MLA_HEADS, kv_heads=MLA_HEADS, dq=MLA_QK_PAD, dv=MLA_V_DIM, tq=TQ_MLA)

        x2 = _merge(
            x2, mod, y_gqa.reshape(t, GQA_Q_COLS), y_mla.reshape(t, MLA_HEADS * MLA_V_DIM),
            w_gate, b_gates[l][None, :], w_branch_gqa[l].astype(BF16),
            w_branch_mla[l].astype(BF16), w_out[l].astype(BF16),
            ln1_g[l][None, :], ln1_b[l][None, :], seq)

        x2 = _ffn(
            x2, mod, w_ffn_gate[l].astype(BF16), w_ffn_up[l].astype(BF16),
            w_ffn_down[l].astype(BF16), ln2_g[l][None, :], ln2_b[l][None, :], seq)

    return x2.reshape(batch, seq, d)
```

```python
import functools

import jax
import jax.numpy as jnp
import numpy as np
from jax import lax
from jax.experimental import pallas as pl
from jax.experimental.pallas import tpu as pltpu

D_MODEL = 2048
GRID_W = 64
ROPE_THETA = 10000.0
HEAD_DIM = 128
GQA_Q_HEADS = 8
GQA_KV_HEADS = 2
MLA_HEADS = 8
MLA_Q_RANK = 512
MLA_KV_RANK = 512
MLA_NOPE_DIM = 128
MLA_ROPE_DIM = 64
MLA_V_DIM = 128
D_FF = 5632
DEPTH = 1
DEEPNORM_ALPHA = (2.0 * DEPTH) ** 0.25
LN_EPS = 1e-5
RMS_EPS = 1e-6

GQA_Q_COLS = GQA_Q_HEADS * HEAD_DIM
GQA_KV_COLS = GQA_KV_HEADS * HEAD_DIM
MLA_QK_PAD = 256
LANES = 128
LOG2E = 1.4426950408889634

VMEM_LIMIT = 56 * 1024 * 1024

TM_IN = 256
TV_CHUNK = TM_IN
TK_ATT = 256
ATT_STEPS = 10
COL_BLOCK = 512
SUM_ROWS = 16
EXP_ROWS = 64
TQ_GQA = 1024
TQ_MLA = 4096
TM_MERGE = 512
TN_MERGE = 512
TM_FFN = 512
TF_FFN = 512

BF16 = jnp.bfloat16
F32 = jnp.float32


def _const_spec(shape):
    nd = len(shape)
    return pl.BlockSpec(shape, lambda *_: (0,) * nd, pipeline_mode=pl.Buffered(1))


def _ln(x):
    mu = jnp.mean(x, axis=-1, keepdims=True)
    xc = x - mu
    var = jnp.mean(xc * xc, axis=-1, keepdims=True)
    return xc * lax.rsqrt(var + LN_EPS)


def _rms(x, gain):
    ms = jnp.mean(x * x, axis=-1, keepdims=True)
    return x * lax.rsqrt(ms + RMS_EPS) * gain


def _rope(x, c, s):
    return x * c + pltpu.roll(x, LANES // 2, 1) * s


def _adaln_kernel(c_ref, w_ref, b_ref, o_ref):
    c = c_ref[...]
    act = (c * jax.nn.sigmoid(c)).astype(BF16)
    o_ref[...] = jnp.dot(act, w_ref[...].astype(BF16), preferred_element_type=F32) + b_ref[...]


def _adaln(c_pad, w_ada, b_ada):
    n = w_ada.shape[1]
    tn = 1024
    return pl.pallas_call(
        _adaln_kernel,
        grid=(n // tn,),
        in_specs=[
            pl.BlockSpec((8, D_MODEL), lambda j: (0, 0)),
            pl.BlockSpec((D_MODEL, tn), lambda j: (0, j)),
            pl.BlockSpec((1, tn), lambda j: (0, j)),
        ],
        out_specs=pl.BlockSpec((8, tn), lambda j: (0, j)),
        out_shape=jax.ShapeDtypeStruct((8, n), F32),
        compiler_params=pltpu.CompilerParams(
            dimension_semantics=("arbitrary",), vmem_limit_bytes=VMEM_LIMIT),
        name="adaln",
    )(c_pad, w_ada, b_ada)


def _inproj_kernel(x_ref, mod_ref, cg_ref, sg_ref, cm_ref, sm_ref, w_ref,
                   gq_ref, gk_ref, gql_ref, gkvl_ref, wuq_ref, wuk_ref, wuv_ref,
                   qg_ref, kg_ref, vgt_ref, qm_ref, km_ref, vmt_ref, *, mla_scale):
    x = x_ref[...]
    shift = mod_ref[0, 0:1, :]
    scale = mod_ref[0, 1:2, :]
    h = (_ln(x) * (1.0 + scale) + shift).astype(BF16)
    proj = jnp.dot(h, w_ref[...], preferred_element_type=F32)

    cg = cg_ref[...]
    sg = sg_ref[...]
    cm = cm_ref[...]
    sm = sm_ref[...]
    ones_rows = jnp.ones((SUM_ROWS, x.shape[0]), BF16)

    for hd in range(GQA_Q_HEADS):
        q = proj[:, hd * HEAD_DIM:(hd + 1) * HEAD_DIM]
        q = _rope(_rms(q, gq_ref[...]), cg, sg)
        qg_ref[:, hd * HEAD_DIM:(hd + 1) * HEAD_DIM] = q.astype(BF16)
    off = GQA_Q_COLS
    for hd in range(GQA_KV_HEADS):
        k = proj[:, off + hd * HEAD_DIM: off + (hd + 1) * HEAD_DIM]
        k = _rope(_rms(k, gk_ref[...]), cg, sg)
        kg_ref[:, hd * HEAD_DIM:(hd + 1) * HEAD_DIM] = k.astype(BF16)
    off += GQA_KV_COLS
    for hd in range(GQA_KV_HEADS):
        v = proj[:, off + hd * HEAD_DIM: off + (hd + 1) * HEAD_DIM]
        vgt_ref[0, hd, 0, 0:HEAD_DIM, :] = v.T.astype(BF16)
        vgt_ref[0, hd, 0, HEAD_DIM:, :] = ones_rows
    off += GQA_KV_COLS

    q_lat = _rms(proj[:, off: off + MLA_Q_RANK], gql_ref[...]).astype(BF16)
    off += MLA_Q_RANK
    kv_lat = _rms(proj[:, off: off + MLA_KV_RANK], gkvl_ref[...]).astype(BF16)
    off += MLA_KV_RANK
    k_rope = _rope(proj[:, off: off + LANES], cm, sm).astype(BF16)

    q_m = jnp.dot(q_lat, wuq_ref[...], preferred_element_type=F32) * mla_scale
    k_n = jnp.dot(kv_lat, wuk_ref[...], preferred_element_type=F32)
    v_m = jnp.dot(kv_lat, wuv_ref[...], preferred_element_type=F32)
    for hd in range(MLA_HEADS):
        b0 = hd * MLA_QK_PAD
        qm_ref[:, b0: b0 + LANES] = q_m[:, b0: b0 + LANES].astype(BF16)
        qm_ref[:, b0 + LANES: b0 + 2 * LANES] = _rope(
            q_m[:, b0 + LANES: b0 + 2 * LANES], cm, sm).astype(BF16)
        km_ref[:, b0: b0 + LANES] = k_n[:, hd * LANES:(hd + 1) * LANES].astype(BF16)
        km_ref[:, b0 + LANES: b0 + 2 * LANES] = k_rope
        vmt_ref[0, hd, 0, 0:MLA_V_DIM, :] = (
            v_m[:, hd * MLA_V_DIM:(hd + 1) * MLA_V_DIM].T.astype(BF16))
        vmt_ref[0, hd, 0, MLA_V_DIM:, :] = ones_rows


def _inproj(x2, mod, cg, sg, cm, sm, w_attn, gq, gk, gql, gkvl, wuq, wuk, wuv, batch, seq):
    t = x2.shape[0]
    tm = TM_IN
    nt = seq // tm
    n_attn = w_attn.shape[1]
    kern = functools.partial(
        _inproj_kernel, mla_scale=float(LOG2E * (MLA_NOPE_DIM + MLA_ROPE_DIM) ** -0.5))
    tok = lambda i: (i, 0)
    pos = lambda i: (i % nt, 0)
    in_specs = [
        pl.BlockSpec((tm, D_MODEL), tok),
        pl.BlockSpec((1, 6, D_MODEL), lambda i: (i // nt, 0, 0)),
        pl.BlockSpec((tm, LANES), pos),
        pl.BlockSpec((tm, LANES), pos),
        pl.BlockSpec((tm, LANES), pos),
        pl.BlockSpec((tm, LANES), pos),
        _const_spec((D_MODEL, n_attn)),
        _const_spec((1, HEAD_DIM)),
        _const_spec((1, HEAD_DIM)),
        _const_spec((1, MLA_Q_RANK)),
        _const_spec((1, MLA_KV_RANK)),
        _const_spec(wuq.shape),
        _const_spec(wuk.shape),
        _const_spec(wuv.shape),
    ]
    out_shape = [
        jax.ShapeDtypeStruct((t, GQA_Q_COLS), BF16),
        jax.ShapeDtypeStruct((t, GQA_KV_COLS), BF16),
        jax.ShapeDtypeStruct(
            (batch, GQA_KV_HEADS, seq // TV_CHUNK, HEAD_DIM + SUM_ROWS, TV_CHUNK), BF16),
        jax.ShapeDtypeStruct((t, MLA_HEADS * MLA_QK_PAD), BF16),
        jax.ShapeDtypeStruct((t, MLA_HEADS * MLA_QK_PAD), BF16),
        jax.ShapeDtypeStruct(
            (batch, MLA_HEADS, seq // TV_CHUNK, MLA_V_DIM + SUM_ROWS, TV_CHUNK), BF16),
    ]
    vt_map = lambda i: (i // nt, 0, i % nt, 0, 0)
    out_specs = [
        pl.BlockSpec((tm, GQA_Q_COLS), tok),
        pl.BlockSpec((tm, GQA_KV_COLS), tok),
        pl.BlockSpec((1, GQA_KV_HEADS, 1, HEAD_DIM + SUM_ROWS, TV_CHUNK), vt_map),
        pl.BlockSpec((tm, MLA_HEADS * MLA_QK_PAD), tok),
        pl.BlockSpec((tm, MLA_HEADS * MLA_QK_PAD), tok),
        pl.BlockSpec((1, MLA_HEADS, 1, MLA_V_DIM + SUM_ROWS, TV_CHUNK), vt_map),
    ]
    return pl.pallas_call(
        kern,
        grid=(t // tm,),
        in_specs=in_specs,
        out_specs=out_specs,
        out_shape=out_shape,
        compiler_params=pltpu.CompilerParams(
            dimension_semantics=("arbitrary",), vmem_limit_bytes=VMEM_LIMIT),
        name="inproj",
    )(x2, mod, cg, sg, cm, sm, w_attn, gq, gk, gql, gkvl, wuq, wuk, wuv)


def _attn_kernel(q_ref, k_ref, vt_ref, o_ref, qt_scr, st_scr, pb_scr, acc_scr,
                 *, groups, dq, dv, tq, tk, nk, steps):
    mq = groups * tq
    cb = COL_BLOCK
    nb = mq // cb
    cols = [slice(n * cb, (n + 1) * cb) for n in range(nb)]
    grp = [(n * cb) // tq for n in range(nb)]
    tok = [(n * cb) % tq for n in range(nb)]

    def scores(c, st_ref, n):
        kc = k_ref[0, pl.ds(pl.multiple_of(c * tk, tk), tk), :]
        st = jnp.dot(kc, qt_scr[:, cols[n]], preferred_element_type=F32)
        st_ref[:, cols[n]] = st
        return jnp.max(st, axis=0, keepdims=True)

    def values(c, pb_ref, alpha, n):
        acc_scr[:, cols[n]] = alpha * acc_scr[:, cols[n]] + jnp.dot(
            vt_ref[0, 0, c], pb_ref[:, cols[n]], preferred_element_type=F32)

    def softmax(st_ref, pb_ref, cmax, m_prev, n):
        m_new = jnp.maximum(m_prev, cmax)
        alpha = jnp.exp2(m_prev - m_new)
        for r in range(tk // EXP_ROWS):
            rows = slice(r * EXP_ROWS, (r + 1) * EXP_ROWS)
            pb_ref[rows, cols[n]] = jnp.exp2((st_ref[rows, cols[n]] - m_new).astype(BF16))
        return m_new, alpha

    def step(c, cur, nxt, carry, first=False, last=False):
        cmax, m_prev, alpha_prev = carry
        cmax_next, m_new, alpha = list(cmax), [], []
        for n in range(nb):
            if not last:
                cmax_next[n] = scores(c + 1, st_scr.at[nxt], n)
        for n in range(nb):
            m_n, a_n = softmax(st_scr.at[cur], pb_scr.at[cur], cmax[n], m_prev[n], n)
            m_new.append(m_n)
            alpha.append(a_n)
            if not first:
                values(c - 1, pb_scr.at[nxt], alpha_prev[n], n)
        return cmax_next, m_new, alpha

    acc_scr[...] = jnp.zeros(acc_scr.shape, F32)
    cmax0 = []
    for n in range(nb):
        qn = q_ref[0, tok[n]:tok[n] + cb, grp[n] * dq:(grp[n] + 1) * dq].astype(F32)
        qt_scr[:, cols[n]] = qn.T.astype(BF16)
        cmax0.append(scores(0, st_scr.at[0], n))
    carry = (cmax0, [jnp.full((1, cb), -jnp.inf, F32)] * nb, [jnp.ones((1, cb), F32)] * nb)
    carry = step(0, 0, 1, carry, first=True)

    n_loop = (nk - 2) // steps

    def body(i, carry):
        c0 = steps * i + 1
        for s in range(steps):
            carry = step(c0 + s, (1 + s) % 2, s % 2, carry)
        return carry

    if n_loop == 1:
        carry = body(0, carry)
    elif n_loop > 1:
        carry = lax.fori_loop(0, n_loop, body, carry)
    for c in range(n_loop * steps + 1, nk - 1):
        carry = step(c, c % 2, (c + 1) % 2, carry)
    _, _, alpha = step(nk - 1, (nk - 1) % 2, nk % 2, carry, last=True)
    for n in range(nb):
        values(nk - 1, pb_scr.at[(nk - 1) % 2], alpha[n], n)
        out_t = acc_scr[0:dv, cols[n]] / acc_scr[dv:dv + 1, cols[n]]
        o_ref[0, tok[n]:tok[n] + cb, grp[n] * dv:(grp[n] + 1) * dv] = out_t.T.astype(o_ref.dtype)


def _attention(q, k, vt, *, q_heads, kv_heads, dq, dv, tq):
    batch, seq, _ = q.shape
    groups = q_heads // kv_heads
    tk = TK_ATT
    assert tk == TV_CHUNK and vt.shape[-2:] == (dv + SUM_ROWS, tk)
    nk = seq // tk
    mq = groups * tq
    kern = functools.partial(_attn_kernel, groups=groups, dq=dq, dv=dv, tq=tq, tk=tk, nk=nk,
                             steps=ATT_STEPS)
    return pl.pallas_call(
        kern,
        grid=(batch, kv_heads, seq // tq),
        in_specs=[
            pl.BlockSpec((1, tq, groups * dq), lambda b, h, i: (b, i, h)),
            pl.BlockSpec((1, seq, dq), lambda b, h, i: (b, 0, h)),
            pl.BlockSpec((1, 1, nk, dv + SUM_ROWS, tk), lambda b, h, i: (b, h, 0, 0, 0)),
        ],
        out_specs=pl.BlockSpec((1, tq, groups * dv), lambda b, h, i: (b, i, h)),
        out_shape=jax.ShapeDtypeStruct((batch, seq, q_heads * dv), BF16),
        scratch_shapes=[
            pltpu.VMEM((dq, mq), BF16),
            pltpu.VMEM((2, tk, mq), F32),
            pltpu.VMEM((2, tk, mq), BF16),
            pltpu.VMEM((dv + SUM_ROWS, mq), F32),
        ],
        compiler_params=pltpu.CompilerParams(
            dimension_semantics=("arbitrary", "arbitrary", "arbitrary"),
            vmem_limit_bytes=VMEM_LIMIT),
        name=f"attn_g{groups}",
    )(q, k, vt)


def _merge_kernel(x_ref, mod_ref, yg_ref, ym_ref, wga_ref, wgb_ref, bga_ref, bgb_ref,
                  wbg_ref, wbm_ref, wo_ref, g_ref, b_ref, o_ref, h_scr, acc_scr):
    j = pl.program_id(1)

    @pl.when(j == 0)
    def _():
        shift = mod_ref[0, 0:1, :]
        scale = mod_ref[0, 1:2, :]
        h_scr[...] = (_ln(x_ref[...]) * (1.0 + scale) + shift).astype(BF16)
        acc_scr[...] = jnp.zeros(acc_scr.shape, F32)

    h = h_scr[...]
    la = jnp.dot(h, wga_ref[...], preferred_element_type=F32) + bga_ref[...]
    lb = jnp.dot(h, wgb_ref[...], preferred_element_type=F32) + bgb_ref[...]
    a = jnp.dot(yg_ref[...], wbg_ref[...], preferred_element_type=F32)
    b = jnp.dot(ym_ref[...], wbm_ref[...], preferred_element_type=F32)
    merged = jax.nn.sigmoid(la) * a + jax.nn.sigmoid(lb) * b
    acc_scr[...] += jnp.dot(merged.astype(BF16), wo_ref[...], preferred_element_type=F32)

    @pl.when(j == pl.num_programs(1) - 1)
    def _():
        gate = mod_ref[0, 2:3, :]
        r = DEEPNORM_ALPHA * x_ref[...] + gate * acc_scr[...]
        o_ref[...] = _ln(r) * g_ref[...] + b_ref[...]


def _merge(x2, mod, yg, ym, w_gate, b_gate, wbg, wbm, wo, ln_g, ln_b, seq):
    t = x2.shape[0]
    tm, tn = TM_MERGE, TN_MERGE
    nt = seq // tm
    nj = D_MODEL // tn
    tok = lambda i, j: (i, 0)
    return pl.pallas_call(
        _merge_kernel,
        grid=(t // tm, nj),
        in_specs=[
            pl.BlockSpec((tm, D_MODEL), tok),
            pl.BlockSpec((1, 6, D_MODEL), lambda i, j: (i // nt, 0, 0)),
            pl.BlockSpec((tm, yg.shape[1]), tok),
            pl.BlockSpec((tm, ym.shape[1]), tok),
            pl.BlockSpec((D_MODEL, tn), lambda i, j: (0, j)),
            pl.BlockSpec((D_MODEL, tn), lambda i, j: (0, j + nj)),
            pl.BlockSpec((1, tn), lambda i, j: (0, j)),
            pl.BlockSpec((1, tn), lambda i, j: (0, j + nj)),
            pl.BlockSpec((wbg.shape[0], tn), lambda i, j: (0, j)),
            pl.BlockSpec((wbm.shape[0], tn), lambda i, j: (0, j)),
            pl.BlockSpec((tn, D_MODEL), lambda i, j: (j, 0)),
            pl.BlockSpec((1, D_MODEL), lambda i, j: (0, 0)),
            pl.BlockSpec((1, D_MODEL), lambda i, j: (0, 0)),
        ],
        out_specs=pl.BlockSpec((tm, D_MODEL), tok),
        out_shape=jax.ShapeDtypeStruct((t, D_MODEL), F32),
        scratch_shapes=[pltpu.VMEM((tm, D_MODEL), BF16), pltpu.VMEM((tm, D_MODEL), F32)],
        compiler_params=pltpu.CompilerParams(
            dimension_semantics=("arbitrary", "arbitrary"), vmem_limit_bytes=VMEM_LIMIT),
        name="merge",
    )(x2, mod, yg, ym, w_gate, w_gate, b_gate, b_gate, wbg, wbm, wo, ln_g, ln_b)


def _ffn_kernel(x_ref, mod_ref, wg_ref, wu_ref, wd_ref, g_ref, b_ref, o_ref, h_scr, acc_scr):
    j = pl.program_id(1)

    @pl.when(j == 0)
    def _():
        shift = mod_ref[0, 3:4, :]
        scale = mod_ref[0, 4:5, :]
        h_scr[...] = (_ln(x_ref[...]) * (1.0 + scale) + shift).astype(BF16)
        acc_scr[...] = jnp.zeros(acc_scr.shape, F32)

    h = h_scr[...]
    gt = jnp.dot(h, wg_ref[...], preferred_element_type=F32)
    up = jnp.dot(h, wu_ref[...], preferred_element_type=F32)
    a = (gt * jax.nn.sigmoid(gt) * up).astype(BF16)
    acc_scr[...] += jnp.dot(a, wd_ref[...], preferred_element_type=F32)

    @pl.when(j == pl.num_programs(1) - 1)
    def _():
        gate = mod_ref[0, 5:6, :]
        r = DEEPNORM_ALPHA * x_ref[...] + gate * acc_scr[...]
        o_ref[...] = _ln(r) * g_ref[...] + b_ref[...]


def _ffn(x1, mod, wg, wu, wd, ln_g, ln_b, seq):
    t = x1.shape[0]
    tm, tf = TM_FFN, TF_FFN
    nt = seq // tm
    tok = lambda i, j: (i, 0)
    return pl.pallas_call(
        _ffn_kernel,
        grid=(t // tm, D_FF // tf),
        in_specs=[
            pl.BlockSpec((tm, D_MODEL), tok),
            pl.BlockSpec((1, 6, D_MODEL), lambda i, j: (i // nt, 0, 0)),
            pl.BlockSpec((D_MODEL, tf), lambda i, j: (0, j)),
            pl.BlockSpec((D_MODEL, tf), lambda i, j: (0, j)),
            pl.BlockSpec((tf, D_MODEL), lambda i, j: (j, 0)),
            pl.BlockSpec((1, D_MODEL), lambda i, j: (0, 0)),
            pl.BlockSpec((1, D_MODEL), lambda i, j: (0, 0)),
        ],
        out_specs=pl.BlockSpec((tm, D_MODEL), tok),
        out_shape=jax.ShapeDtypeStruct((t, D_MODEL), F32),
        scratch_shapes=[pltpu.VMEM((tm, D_MODEL), BF16), pltpu.VMEM((tm, D_MODEL), F32)],
        compiler_params=pltpu.CompilerParams(
            dimension_semantics=("arbitrary", "arbitrary"), vmem_limit_bytes=VMEM_LIMIT),
        name="ffn",
    )(x1, mod, wg, wu, wd, ln_g, ln_b)


def _rope_tables(seq, dim):
    rows = seq // GRID_W
    quarter = dim // 4
    inv_freq = ROPE_THETA ** (-jnp.arange(quarter, dtype=F32) / quarter)
    row_ang = jnp.arange(rows, dtype=F32)[:, None] * inv_freq
    col_ang = jnp.arange(GRID_W, dtype=F32)[:, None] * inv_freq

    def table(fn):
        return jnp.concatenate([
            jnp.broadcast_to(fn(row_ang)[:, None, :], (rows, GRID_W, quarter)),
            jnp.broadcast_to(fn(col_ang)[None, :, :], (rows, GRID_W, quarter)),
        ], axis=-1).reshape(seq, 2 * quarter)

    half = dim // 2
    pad = jnp.zeros((seq, LANES // 2 - half), F32)
    cos, sin = table(jnp.cos), table(jnp.sin)
    c = jnp.concatenate([cos, pad, cos, pad], axis=-1)
    s = jnp.concatenate([-sin, pad, sin, pad], axis=-1)
    return c, s


def _deinterleave(n):
    return np.concatenate([np.arange(0, n, 2), np.arange(1, n, 2)])


def kernel(x, c, w_ada, b_ada, w_in, b_gates, gqa_q_gain, gqa_k_gain, mla_q_gain, mla_kv_gain,
           w_mla_uq, w_mla_ukv, w_branch_gqa, w_branch_mla, w_out, ln1_g, ln1_b,
           w_ffn_gate, w_ffn_up, w_ffn_down, ln2_g, ln2_b):
    batch, seq, d = x.shape
    assert d == D_MODEL and w_ada.shape[0] == DEPTH
    t = batch * seq
    x2 = x.reshape(t, d)

    cg, sg = _rope_tables(seq, HEAD_DIM)
    cm, sm = _rope_tables(seq, MLA_ROPE_DIM)
    perm_head = _deinterleave(HEAD_DIM)
    zeros32 = lambda rows: jnp.zeros((rows, LANES // 2 - MLA_ROPE_DIM // 2), BF16)

    c_pad = jnp.zeros((8, d), F32).at[:batch].set(c)

    for l in range(DEPTH):
        mod = _adaln(c_pad, w_ada[l], b_ada[l][None, :])[:batch].reshape(batch, 6, d)

        wl = w_in[l]
        o0 = 0
        wq = wl[:, o0:o0 + GQA_Q_COLS].reshape(d, GQA_Q_HEADS, HEAD_DIM)[:, :, perm_head]
        o0 += GQA_Q_COLS
        wk = wl[:, o0:o0 + GQA_KV_COLS].reshape(d, GQA_KV_HEADS, HEAD_DIM)[:, :, perm_head]
        o0 += GQA_KV_COLS
        wv = wl[:, o0:o0 + GQA_KV_COLS]
        o0 += GQA_KV_COLS
        wql = wl[:, o0:o0 + MLA_Q_RANK]
        o0 += MLA_Q_RANK
        wkvl = wl[:, o0:o0 + MLA_KV_RANK]
        o0 += MLA_KV_RANK
        wkr = wl[:, o0:o0 + MLA_ROPE_DIM].astype(BF16)
        o0 += MLA_ROPE_DIM
        w_gate = wl[:, o0:].astype(BF16)
        wkr_pad = jnp.concatenate(
            [wkr[:, 0::2], zeros32(d), wkr[:, 1::2], zeros32(d)], axis=-1)
        w_attn = jnp.concatenate([
            wq.reshape(d, GQA_Q_COLS).astype(BF16), wk.reshape(d, GQA_KV_COLS).astype(BF16),
            wv.astype(BF16), wql.astype(BF16), wkvl.astype(BF16), wkr_pad], axis=-1)

        uq = w_mla_uq[l].astype(BF16).reshape(MLA_Q_RANK, MLA_HEADS, MLA_NOPE_DIM + MLA_ROPE_DIM)
        uq_r = uq[:, :, MLA_NOPE_DIM:]
        z = jnp.zeros((MLA_Q_RANK, MLA_HEADS, LANES // 2 - MLA_ROPE_DIM // 2), BF16)
        wuq = jnp.concatenate(
            [uq[:, :, :MLA_NOPE_DIM], uq_r[:, :, 0::2], z, uq_r[:, :, 1::2], z],
            axis=-1).reshape(MLA_Q_RANK, MLA_HEADS * MLA_QK_PAD)
        ukv = w_mla_ukv[l].astype(BF16).reshape(MLA_KV_RANK, MLA_HEADS, MLA_NOPE_DIM + MLA_V_DIM)
        wuk = ukv[:, :, :MLA_NOPE_DIM].reshape(MLA_KV_RANK, MLA_HEADS * MLA_NOPE_DIM)
        wuv = ukv[:, :, MLA_NOPE_DIM:].reshape(MLA_KV_RANK, MLA_HEADS * MLA_V_DIM)

        gq = (gqa_q_gain[l][perm_head] * (LOG2E * HEAD_DIM ** -0.5))[None, :]
        gk = gqa_k_gain[l][perm_head][None, :]

        qg, kg, vgt, qm, km, vmt = _inproj(
            x2, mod, cg, sg, cm, sm, w_attn, gq, gk, mla_q_gain[l][None, :],
            mla_kv_gain[l][None, :], wuq, wuk, wuv, batch, seq)

        y_gqa = _attention(
            qg.reshape(batch, seq, GQA_Q_COLS), kg.reshape(batch, seq, GQA_KV_COLS), vgt,
            q_heads=GQA_Q_HEADS, kv_heads=GQA_KV_HEADS, dq=HEAD_DIM, dv=HEAD_DIM, tq=TQ_GQA)
        y_mla = _attention(
            qm.reshape(batch, seq, MLA_HEADS * MLA_QK_PAD),
            km.reshape(batch, seq, MLA_HEADS * MLA_QK_PAD), vmt,
            q_heads=MLA_HEADS, kv_heads=MLA_HEADS, dq=MLA_QK_PAD, dv=MLA_V_DIM, tq=TQ_MLA)

        x2 = _merge(
            x2, mod, y_gqa.reshape(t, GQA_Q_COLS), y_mla.reshape(t, MLA_HEADS * MLA_V_DIM),
            w_gate, b_gates[l][None, :], w_branch_gqa[l].astype(BF16),
            w_branch_mla[l].astype(BF16), w_out[l].astype(BF16),
            ln1_g[l][None, :], ln1_b[l][None, :], seq)

        x2 = _ffn(
            x2, mod, w_ffn_gate[l].astype(BF16), w_ffn_up[l].astype(BF16),
            w_ffn_down[l].astype(BF16), ln2_g[l][None, :], ln2_b[l][None, :], seq)

    return x2.reshape(batch, seq, d)
```

```python
import functools

import jax
import jax.numpy as jnp
import numpy as np
from jax import lax
from jax.experimental import pallas as pl
from jax.experimental.pallas import tpu as pltpu

D_MODEL = 2048
GRID_W = 64
ROPE_THETA = 10000.0
HEAD_DIM = 128
GQA_Q_HEADS = 8
GQA_KV_HEADS = 2
MLA_HEADS = 8
MLA_Q_RANK = 512
MLA_KV_RANK = 512
MLA_NOPE_DIM = 128
MLA_ROPE_DIM = 64
MLA_V_DIM = 128
D_FF = 5632
DEPTH = 1
DEEPNORM_ALPHA = (2.0 * DEPTH) ** 0.25
LN_EPS = 1e-5
RMS_EPS = 1e-6

GQA_Q_COLS = GQA_Q_HEADS * HEAD_DIM
GQA_KV_COLS = GQA_KV_HEADS * HEAD_DIM
MLA_QK_PAD = 256
LANES = 128
LOG2E = 1.4426950408889634

VMEM_LIMIT = 56 * 1024 * 1024

TM_IN = 256
TV_CHUNK = TM_IN
TK_ATT = 256
ATT_STEPS = 10
COL_BLOCK = 512
SUM_ROWS = 16
EXP_ROWS = 64
TQ_GQA = 1024
TQ_MLA = 4096
TM_MERGE = 512
TN_MERGE = 512
TM_FFN = 512
TF_FFN = 512

BF16 = jnp.bfloat16
F32 = jnp.float32


def _const_spec(shape):
    nd = len(shape)
    return pl.BlockSpec(shape, lambda *_: (0,) * nd, pipeline_mode=pl.Buffered(1))


def _ln(x):
    mu = jnp.mean(x, axis=-1, keepdims=True)
    xc = x - mu
    var = jnp.mean(xc * xc, axis=-1, keepdims=True)
    return xc * lax.rsqrt(var + LN_EPS)


def _rms(x, gain):
    ms = jnp.mean(x * x, axis=-1, keepdims=True)
    return x * lax.rsqrt(ms + RMS_EPS) * gain


def _rope(x, c, s):
    return x * c + pltpu.roll(x, LANES // 2, 1) * s


def _adaln_kernel(c_ref, w_ref, b_ref, o_ref):
    c = c_ref[...]
    act = (c * jax.nn.sigmoid(c)).astype(BF16)
    o_ref[...] = jnp.dot(act, w_ref[...].astype(BF16), preferred_element_type=F32) + b_ref[...]


def _adaln(c_pad, w_ada, b_ada):
    n = w_ada.shape[1]
    tn = 1024
    return pl.pallas_call(
        _adaln_kernel,
        grid=(n // tn,),
        in_specs=[
            pl.BlockSpec((8, D_MODEL), lambda j: (0, 0)),
            pl.BlockSpec((D_MODEL, tn), lambda j: (0, j)),
            pl.BlockSpec((1, tn), lambda j: (0, j)),
        ],
        out_specs=pl.BlockSpec((8, tn), lambda j: (0, j)),
        out_shape=jax.ShapeDtypeStruct((8, n), F32),
        compiler_params=pltpu.CompilerParams(
            dimension_semantics=("arbitrary",), vmem_limit_bytes=VMEM_LIMIT),
        name="adaln",
    )(c_pad, w_ada, b_ada)


def _inproj_kernel(x_ref, mod_ref, cg_ref, sg_ref, cm_ref, sm_ref, w_ref,
                   gq_ref, gk_ref, gql_ref, gkvl_ref, wuq_ref, wuk_ref, wuv_ref,
                   h_ref, qg_ref, kg_ref, vgt_ref, qm_ref, km_ref, vmt_ref, *, mla_scale):
    x = x_ref[...]
    shift = mod_ref[0, 0:1, :]
    scale = mod_ref[0, 1:2, :]
    h = (_ln(x) * (1.0 + scale) + shift).astype(BF16)
    h_ref[...] = h
    proj = jnp.dot(h, w_ref[...], preferred_element_type=F32)

    cg = cg_ref[...]
    sg = sg_ref[...]
    cm = cm_ref[...]
    sm = sm_ref[...]
    ones_rows = jnp.ones((SUM_ROWS, x.shape[0]), BF16)

    for hd in range(GQA_Q_HEADS):
        q = proj[:, hd * HEAD_DIM:(hd + 1) * HEAD_DIM]
        q = _rope(_rms(q, gq_ref[...]), cg, sg)
        qg_ref[:, hd * HEAD_DIM:(hd + 1) * HEAD_DIM] = q.astype(BF16)
    off = GQA_Q_COLS
    for hd in range(GQA_KV_HEADS):
        k = proj[:, off + hd * HEAD_DIM: off + (hd + 1) * HEAD_DIM]
        k = _rope(_rms(k, gk_ref[...]), cg, sg)
        kg_ref[:, hd * HEAD_DIM:(hd + 1) * HEAD_DIM] = k.astype(BF16)
    off += GQA_KV_COLS
    for hd in range(GQA_KV_HEADS):
        v = proj[:, off + hd * HEAD_DIM: off + (hd + 1) * HEAD_DIM]
        vgt_ref[0, hd, 0, 0:HEAD_DIM, :] = v.T.astype(BF16)
        vgt_ref[0, hd, 0, HEAD_DIM:, :] = ones_rows
    off += GQA_KV_COLS

    q_lat = _rms(proj[:, off: off + MLA_Q_RANK], gql_ref[...]).astype(BF16)
    off += MLA_Q_RANK
    kv_lat = _rms(proj[:, off: off + MLA_KV_RANK], gkvl_ref[...]).astype(BF16)
    off += MLA_KV_RANK
    k_rope = _rope(proj[:, off: off + LANES], cm, sm).astype(BF16)

    q_m = jnp.dot(q_lat, wuq_ref[...], preferred_element_type=F32) * mla_scale
    k_n = jnp.dot(kv_lat, wuk_ref[...], preferred_element_type=F32)
    v_m = jnp.dot(kv_lat, wuv_ref[...], preferred_element_type=F32)
    for hd in range(MLA_HEADS):
        b0 = hd * MLA_QK_PAD
        qm_ref[:, b0: b0 + LANES] = q_m[:, b0: b0 + LANES].astype(BF16)
        qm_ref[:, b0 + LANES: b0 + 2 * LANES] = _rope(
            q_m[:, b0 + LANES: b0 + 2 * LANES], cm, sm).astype(BF16)
        km_ref[:, b0: b0 + LANES] = k_n[:, hd * LANES:(hd + 1) * LANES].astype(BF16)
        km_ref[:, b0 + LANES: b0 + 2 * LANES] = k_rope
        vmt_ref[0, hd, 0, 0:MLA_V_DIM, :] = (
            v_m[:, hd * MLA_V_DIM:(hd + 1) * MLA_V_DIM].T.astype(BF16))
        vmt_ref[0, hd, 0, MLA_V_DIM:, :] = ones_rows


def _inproj(x2, mod, cg, sg, cm, sm, w_attn, gq, gk, gql, gkvl, wuq, wuk, wuv, batch, seq):
    t = x2.shape[0]
    tm = TM_IN
    nt = seq // tm
    n_attn = w_attn.shape[1]
    kern = functools.partial(
        _inproj_kernel, mla_scale=float(LOG2E * (MLA_NOPE_DIM + MLA_ROPE_DIM) ** -0.5))
    tok = lambda i: (i, 0)
    pos = lambda i: (i % nt, 0)
    in_specs = [
        pl.BlockSpec((tm, D_MODEL), tok),
        pl.BlockSpec((1, 6, D_MODEL), lambda i: (i // nt, 0, 0)),
        pl.BlockSpec((tm, LANES), pos),
        pl.BlockSpec((tm, LANES), pos),
        pl.BlockSpec((tm, LANES), pos),
        pl.BlockSpec((tm, LANES), pos),
        _const_spec((D_MODEL, n_attn)),
        _const_spec((1, HEAD_DIM)),
        _const_spec((1, HEAD_DIM)),
        _const_spec((1, MLA_Q_RANK)),
        _const_spec((1, MLA_KV_RANK)),
        _const_spec(wuq.shape),
        _const_spec(wuk.shape),
        _const_spec(wuv.shape),
    ]
    out_shape = [
        jax.ShapeDtypeStruct((t, D_MODEL), BF16),
        jax.ShapeDtypeStruct((t, GQA_Q_COLS), BF16),
        jax.ShapeDtypeStruct((t, GQA_KV_COLS), BF16),
        jax.ShapeDtypeStruct(
            (batch, GQA_KV_HEADS, seq // TV_CHUNK, HEAD_DIM + SUM_ROWS, TV_CHUNK), BF16),
        jax.ShapeDtypeStruct((t, MLA_HEADS * MLA_QK_PAD), BF16),
        jax.ShapeDtypeStruct((t, MLA_HEADS * MLA_QK_PAD), BF16),
        jax.ShapeDtypeStruct(
            (batch, MLA_HEADS, seq // TV_CHUNK, MLA_V_DIM + SUM_ROWS, TV_CHUNK), BF16),
    ]
    vt_map = lambda i: (i // nt, 0, i % nt, 0, 0)
    out_specs = [
        pl.BlockSpec((tm, D_MODEL), tok),
        pl.BlockSpec((tm, GQA_Q_COLS), tok),
        pl.BlockSpec((tm, GQA_KV_COLS), tok),
        pl.BlockSpec((1, GQA_KV_HEADS, 1, HEAD_DIM + SUM_ROWS, TV_CHUNK), vt_map),
        pl.BlockSpec((tm, MLA_HEADS * MLA_QK_PAD), tok),
        pl.BlockSpec((tm, MLA_HEADS * MLA_QK_PAD), tok),
        pl.BlockSpec((1, MLA_HEADS, 1, MLA_V_DIM + SUM_ROWS, TV_CHUNK), vt_map),
    ]
    return pl.pallas_call(
        kern,
        grid=(t // tm,),
        in_specs=in_specs,
        out_specs=out_specs,
        out_shape=out_shape,
        compiler_params=pltpu.CompilerParams(
            dimension_semantics=("arbitrary",), vmem_limit_bytes=VMEM_LIMIT),
        name="inproj",
    )(x2, mod, cg, sg, cm, sm, w_attn, gq, gk, gql, gkvl, wuq, wuk, wuv)


def _attn_kernel(q_ref, k_ref, vt_ref, o_ref, qt_scr, st_scr, pb_scr, acc_scr,
                 *, groups, dq, dv, tq, tk, nk, steps):
    mq = groups * tq
    cb = COL_BLOCK
    nb = mq // cb
    cols = [slice(n * cb, (n + 1) * cb) for n in range(nb)]
    grp = [(n * cb) // tq for n in range(nb)]
    tok = [(n * cb) % tq for n in range(nb)]

    def scores(c, st_ref, n):
        kc = k_ref[0, pl.ds(pl.multiple_of(c * tk, tk), tk), :]
        st = jnp.dot(kc, qt_scr[:, cols[n]], preferred_element_type=F32)
        st_ref[:, cols[n]] = st
        return jnp.max(st, axis=0, keepdims=True)

    def values(c, pb_ref, alpha, n):
        acc_scr[:, cols[n]] = alpha * acc_scr[:, cols[n]] + jnp.dot(
            vt_ref[0, 0, c], pb_ref[:, cols[n]], preferred_element_type=F32)

    def softmax(st_ref, pb_ref, cmax, m_prev, n):
        m_new = jnp.maximum(m_prev, cmax)
        alpha = jnp.exp2(m_prev - m_new)
        for r in range(tk // EXP_ROWS):
            rows = slice(r * EXP_ROWS, (r + 1) * EXP_ROWS)
            pb_ref[rows, cols[n]] = jnp.exp2((st_ref[rows, cols[n]] - m_new).astype(BF16))
        return m_new, alpha

    def step(c, cur, nxt, carry, first=False, last=False):
        cmax, m_prev, alpha_prev = carry
        cmax_next, m_new, alpha = list(cmax), [], []
        for n in range(nb):
            if not last:
                cmax_next[n] = scores(c + 1, st_scr.at[nxt], n)
        for n in range(nb):
            m_n, a_n = softmax(st_scr.at[cur], pb_scr.at[cur], cmax[n], m_prev[n], n)
            m_new.append(m_n)
            alpha.append(a_n)
            if not first:
                values(c - 1, pb_scr.at[nxt], alpha_prev[n], n)
        return cmax_next, m_new, alpha

    acc_scr[...] = jnp.zeros(acc_scr.shape, F32)
    cmax0 = []
    for n in range(nb):
        qn = q_ref[0, tok[n]:tok[n] + cb, grp[n] * dq:(grp[n] + 1) * dq].astype(F32)
        qt_scr[:, cols[n]] = qn.T.astype(BF16)
        cmax0.append(scores(0, st_scr.at[0], n))
    carry = (cmax0, [jnp.full((1, cb), -jnp.inf, F32)] * nb, [jnp.ones((1, cb), F32)] * nb)
    carry = step(0, 0, 1, carry, first=True)

    n_loop = (nk - 2) // steps

    def body(i, carry):
        c0 = steps * i + 1
        for s in range(steps):
            carry = step(c0 + s, (1 + s) % 2, s % 2, carry)
        return carry

    if n_loop == 1:
        carry = body(0, carry)
    elif n_loop > 1:
        carry = lax.fori_loop(0, n_loop, body, carry)
    for c in range(n_loop * steps + 1, nk - 1):
        carry = step(c, c % 2, (c + 1) % 2, carry)
    _, _, alpha = step(nk - 1, (nk - 1) % 2, nk % 2, carry, last=True)
    for n in range(nb):
        values(nk - 1, pb_scr.at[(nk - 1) % 2], alpha[n], n)
        out_t = acc_scr[0:dv, cols[n]] / acc_scr[dv:dv + 1, cols[n]]
        o_ref[0, tok[n]:tok[n] + cb, grp[n] * dv:(grp[n] + 1) * dv] = out_t.T.astype(o_ref.dtype)


def _attention(q, k, vt, *, q_heads, kv_heads, dq, dv, tq):
    batch, seq, _ = q.shape
    groups = q_heads // kv_heads
    tk = TK_ATT
    assert tk == TV_CHUNK and vt.shape[-2:] == (dv + SUM_ROWS, tk)
    nk = seq // tk
    mq = groups * tq
    kern = functools.partial(_attn_kernel, groups=groups, dq=dq, dv=dv, tq=tq, tk=tk, nk=nk,
                             steps=ATT_STEPS)
    return pl.pallas_call(
        kern,
        grid=(batch, kv_heads, seq // tq),
        in_specs=[
            pl.BlockSpec((1, tq, groups * dq), lambda b, h, i: (b, i, h)),
            pl.BlockSpec((1, seq, dq), lambda b, h, i: (b, 0, h)),
            pl.BlockSpec((1, 1, nk, dv + SUM_ROWS, tk), lambda b, h, i: (b, h, 0, 0, 0)),
        ],
        out_specs=pl.BlockSpec((1, tq, groups * dv), lambda b, h, i: (b, i, h)),
        out_shape=jax.ShapeDtypeStruct((batch, seq, q_heads * dv), BF16),
        scratch_shapes=[
            pltpu.VMEM((dq, mq), BF16),
            pltpu.VMEM((2, tk, mq), F32),
            pltpu.VMEM((2, tk, mq), BF16),
            pltpu.VMEM((dv + SUM_ROWS, mq), F32),
        ],
        compiler_params=pltpu.CompilerParams(
            dimension_semantics=("arbitrary", "arbitrary", "arbitrary"),
            vmem_limit_bytes=VMEM_LIMIT),
        name=f"attn_g{groups}",
    )(q, k, vt)


def _merge_kernel(x_ref, mod_ref, h_ref, yg_ref, ym_ref, wga_ref, wgb_ref, bga_ref, bgb_ref,
                  wbg_ref, wbm_ref, wo_ref, g_ref, b_ref, o_ref, h_scr, acc_scr):
    j = pl.program_id(1)

    @pl.when(j == 0)
    def _():
        h_scr[...] = h_ref[...]
        acc_scr[...] = jnp.zeros(acc_scr.shape, F32)

    h = h_scr[...]
    la = jnp.dot(h, wga_ref[...], preferred_element_type=F32) + bga_ref[...]
    lb = jnp.dot(h, wgb_ref[...], preferred_element_type=F32) + bgb_ref[...]
    a = jnp.dot(yg_ref[...], wbg_ref[...], preferred_element_type=F32)
    b = jnp.dot(ym_ref[...], wbm_ref[...], preferred_element_type=F32)
    merged = jax.nn.sigmoid(la) * a + jax.nn.sigmoid(lb) * b
    acc_scr[...] += jnp.dot(merged.astype(BF16), wo_ref[...], preferred_element_type=F32)

    @pl.when(j == pl.num_programs(1) - 1)
    def _():
        gate = mod_ref[0, 2:3, :]
        r = DEEPNORM_ALPHA * x_ref[...] + gate * acc_scr[...]
        o_ref[...] = _ln(r) * g_ref[...] + b_ref[...]


def _merge(x2, mod, h, yg, ym, w_gate, b_gate, wbg, wbm, wo, ln_g, ln_b, seq):
    t = x2.shape[0]
    tm, tn = TM_MERGE, TN_MERGE
    nt = seq // tm
    nj = D_MODEL // tn
    tok = lambda i, j: (i, 0)
    return pl.pallas_call(
        _merge_kernel,
        grid=(t // tm, nj),
        in_specs=[
            pl.BlockSpec((tm, D_MODEL), tok),
            pl.BlockSpec((1, 6, D_MODEL), lambda i, j: (i // nt, 0, 0)),
            pl.BlockSpec((tm, D_MODEL), tok),
            pl.BlockSpec((tm, yg.shape[1]), tok),
            pl.BlockSpec((tm, ym.shape[1]), tok),
            pl.BlockSpec((D_MODEL, tn), lambda i, j: (0, j)),
            pl.BlockSpec((D_MODEL, tn), lambda i, j: (0, j + nj)),
            pl.BlockSpec((1, tn), lambda i, j: (0, j)),
            pl.BlockSpec((1, tn), lambda i, j: (0, j + nj)),
            pl.BlockSpec((wbg.shape[0], tn), lambda i, j: (0, j)),
            pl.BlockSpec((wbm.shape[0], tn), lambda i, j: (0, j)),
            pl.BlockSpec((tn, D_MODEL), lambda i, j: (j, 0)),
            pl.BlockSpec((1, D_MODEL), lambda i, j: (0, 0)),
            pl.BlockSpec((1, D_MODEL), lambda i, j: (0, 0)),
        ],
        out_specs=pl.BlockSpec((tm, D_MODEL), tok),
        out_shape=jax.ShapeDtypeStruct((t, D_MODEL), F32),
        scratch_shapes=[pltpu.VMEM((tm, D_MODEL), BF16), pltpu.VMEM((tm, D_MODEL), F32)],
        compiler_params=pltpu.CompilerParams(
            dimension_semantics=("arbitrary", "arbitrary"), vmem_limit_bytes=VMEM_LIMIT),
        name="merge",
    )(x2, mod, h, yg, ym, w_gate, w_gate, b_gate, b_gate, wbg, wbm, wo, ln_g, ln_b)


def _ffn_kernel(x_ref, mod_ref, wg_ref, wu_ref, wd_ref, g_ref, b_ref, o_ref, h_scr, acc_scr):
    j = pl.program_id(1)

    @pl.when(j == 0)
    def _():
        shift = mod_ref[0, 3:4, :]
        scale = mod_ref[0, 4:5, :]
        h_scr[...] = (_ln(x_ref[...]) * (1.0 + scale) + shift).astype(BF16)
        acc_scr[...] = jnp.zeros(acc_scr.shape, F32)

    h = h_scr[...]
    gt = jnp.dot(h, wg_ref[...], preferred_element_type=F32)
    up = jnp.dot(h, wu_ref[...], preferred_element_type=F32)
    a = (gt * jax.nn.sigmoid(gt) * up).astype(BF16)
    acc_scr[...] += jnp.dot(a, wd_ref[...], preferred_element_type=F32)

    @pl.when(j == pl.num_programs(1) - 1)
    def _():
        gate = mod_ref[0, 5:6, :]
        r = DEEPNORM_ALPHA * x_ref[...] + gate * acc_scr[...]
        o_ref[...] = _ln(r) * g_ref[...] + b_ref[...]


def _ffn(x1, mod, wg, wu, wd, ln_g, ln_b, seq):
    t = x1.shape[0]
    tm, tf = TM_FFN, TF_FFN
    nt = seq // tm
    tok = lambda i, j: (i, 0)
    return pl.pallas_call(
        _ffn_kernel,
        grid=(t // tm, D_FF // tf),
        in_specs=[
            pl.BlockSpec((tm, D_MODEL), tok),
            pl.BlockSpec((1, 6, D_MODEL), lambda i, j: (i // nt, 0, 0)),
            pl.BlockSpec((D_MODEL, tf), lambda i, j: (0, j)),
            pl.BlockSpec((D_MODEL, tf), lambda i, j: (0, j)),
            pl.BlockSpec((tf, D_MODEL), lambda i, j: (j, 0)),
            pl.BlockSpec((1, D_MODEL), lambda i, j: (0, 0)),
            pl.BlockSpec((1, D_MODEL), lambda i, j: (0, 0)),
        ],
        out_specs=pl.BlockSpec((tm, D_MODEL), tok),
        out_shape=jax.ShapeDtypeStruct((t, D_MODEL), F32),
        scratch_shapes=[pltpu.VMEM((tm, D_MODEL), BF16), pltpu.VMEM((tm, D_MODEL), F32)],
        compiler_params=pltpu.CompilerParams(
            dimension_semantics=("arbitrary", "arbitrary"), vmem_limit_bytes=VMEM_LIMIT),
        name="ffn",
    )(x1, mod, wg, wu, wd, ln_g, ln_b)


def _rope_tables(seq, dim):
    rows = seq // GRID_W
    quarter = dim // 4
    inv_freq = ROPE_THETA ** (-jnp.arange(quarter, dtype=F32) / quarter)
    row_ang = jnp.arange(rows, dtype=F32)[:, None] * inv_freq
    col_ang = jnp.arange(GRID_W, dtype=F32)[:, None] * inv_freq

    def table(fn):
        return jnp.concatenate([
            jnp.broadcast_to(fn(row_ang)[:, None, :], (rows, GRID_W, quarter)),
            jnp.broadcast_to(fn(col_ang)[None, :, :], (rows, GRID_W, quarter)),
        ], axis=-1).reshape(seq, 2 * quarter)

    half = dim // 2
    pad = jnp.zeros((seq, LANES // 2 - half), F32)
    cos, sin = table(jnp.cos), table(jnp.sin)
    c = jnp.concatenate([cos, pad, cos, pad], axis=-1)
    s = jnp.concatenate([-sin, pad, sin, pad], axis=-1)
    return c, s


def _deinterleave(n):
    return np.concatenate([np.arange(0, n, 2), np.arange(1, n, 2)])


def kernel(x, c, w_ada, b_ada, w_in, b_gates, gqa_q_gain, gqa_k_gain, mla_q_gain, mla_kv_gain,
           w_mla_uq, w_mla_ukv, w_branch_gqa, w_branch_mla, w_out, ln1_g, ln1_b,
           w_ffn_gate, w_ffn_up, w_ffn_down, ln2_g, ln2_b):
    batch, seq, d = x.shape
    assert d == D_MODEL and w_ada.shape[0] == DEPTH
    t = batch * seq
    x2 = x.reshape(t, d)

    cg, sg = _rope_tables(seq, HEAD_DIM)
    cm, sm = _rope_tables(seq, MLA_ROPE_DIM)
    perm_head = _deinterleave(HEAD_DIM)
    zeros32 = lambda rows: jnp.zeros((rows, LANES // 2 - MLA_ROPE_DIM // 2), BF16)

    c_pad = jnp.zeros((8, d), F32).at[:batch].set(c)

    for l in range(DEPTH):
        mod = _adaln(c_pad, w_ada[l], b_ada[l][None, :])[:batch].reshape(batch, 6, d)

        wl = w_in[l]
        o0 = 0
        wq = wl[:, o0:o0 + GQA_Q_COLS].reshape(d, GQA_Q_HEADS, HEAD_DIM)[:, :, perm_head]
        o0 += GQA_Q_COLS
        wk = wl[:, o0:o0 + GQA_KV_COLS].reshape(d, GQA_KV_HEADS, HEAD_DIM)[:, :, perm_head]
        o0 += GQA_KV_COLS
        wv = wl[:, o0:o0 + GQA_KV_COLS]
        o0 += GQA_KV_COLS
        wql = wl[:, o0:o0 + MLA_Q_RANK]
        o0 += MLA_Q_RANK
        wkvl = wl[:, o0:o0 + MLA_KV_RANK]
        o0 += MLA_KV_RANK
        wkr = wl[:, o0:o0 + MLA_ROPE_DIM].astype(BF16)
        o0 += MLA_ROPE_DIM
        w_gate = wl[:, o0:].astype(BF16)
        wkr_pad = jnp.concatenate(
            [wkr[:, 0::2], zeros32(d), wkr[:, 1::2], zeros32(d)], axis=-1)
        w_attn = jnp.concatenate([
            wq.reshape(d, GQA_Q_COLS).astype(BF16), wk.reshape(d, GQA_KV_COLS).astype(BF16),
            wv.astype(BF16), wql.astype(BF16), wkvl.astype(BF16), wkr_pad], axis=-1)

        uq = w_mla_uq[l].astype(BF16).reshape(MLA_Q_RANK, MLA_HEADS, MLA_NOPE_DIM + MLA_ROPE_DIM)
        uq_r = uq[:, :, MLA_NOPE_DIM:]
        z = jnp.zeros((MLA_Q_RANK, MLA_HEADS, LANES // 2 - MLA_ROPE_DIM // 2), BF16)
        wuq = jnp.concatenate(
            [uq[:, :, :MLA_NOPE_DIM], uq_r[:, :, 0::2], z, uq_r[:, :, 1::2], z],
            axis=-1).reshape(MLA_Q_RANK, MLA_HEADS * MLA_QK_PAD)
        ukv = w_mla_ukv[l].astype(BF16).reshape(MLA_KV_RANK, MLA_HEADS, MLA_NOPE_DIM + MLA_V_DIM)
        wuk = ukv[:, :, :MLA_NOPE_DIM].reshape(MLA_KV_RANK, MLA_HEADS * MLA_NOPE_DIM)
        wuv = ukv[:, :, MLA_NOPE_DIM:].reshape(MLA_KV_RANK, MLA_HEADS * MLA_V_DIM)

        gq = (gqa_q_gain[l][perm_head] * (LOG2E * HEAD_DIM ** -0.5))[None, :]
        gk = gqa_k_gain[l][perm_head][None, :]

        h1, qg, kg, vgt, qm, km, vmt = _inproj(
            x2, mod, cg, sg, cm, sm, w_attn, gq, gk, mla_q_gain[l][None, :],
            mla_kv_gain[l][None, :], wuq, wuk, wuv, batch, seq)

        y_gqa = _attention(
            qg.reshape(batch, seq, GQA_Q_COLS), kg.reshape(batch, seq, GQA_KV_COLS), vgt,
            q_heads=GQA_Q_HEADS, kv_heads=GQA_KV_HEADS, dq=HEAD_DIM, dv=HEAD_DIM, tq=TQ_GQA)
        y_mla = _attention(
            qm.reshape(batch, seq, MLA_HEADS * MLA_QK_PAD),
            km.reshape(batch, seq, MLA_HEADS * MLA_QK_PAD), vmt,
            q_heads=MLA_HEADS, kv_heads=MLA_HEADS, dq=MLA_QK_PAD, dv=MLA_V_DIM, tq=TQ_MLA)

        x2 = _merge(
            x2, mod, h1, y_gqa.reshape(t, GQA_Q_COLS), y_mla.reshape(t, MLA_HEADS * MLA_V_DIM),
            w_gate, b_gates[l][None, :], w_branch_gqa[l].astype(BF16),
            w_branch_mla[l].astype(BF16), w_out[l].astype(BF16),
            ln1_g[l][None, :], ln1_b[l][None, :], seq)

        x2 = _ffn(
            x2, mod, w_ffn_gate[l].astype(BF16), w_ffn_up[l].astype(BF16),
            w_ffn_down[l].astype(BF16), ln2_g[l][None, :], ln2_b[l][None, :], seq)

    return x2.reshape(batch, seq, d)
```

```python
import functools

import jax
import jax.numpy as jnp
import numpy as np
from jax import lax
from jax.experimental import pallas as pl
from jax.experimental.pallas import tpu as pltpu

D_MODEL = 2048
GRID_W = 64
ROPE_THETA = 10000.0
HEAD_DIM = 128
GQA_Q_HEADS = 8
GQA_KV_HEADS = 2
MLA_HEADS = 8
MLA_Q_RANK = 512
MLA_KV_RANK = 512
MLA_NOPE_DIM = 128
MLA_ROPE_DIM = 64
MLA_V_DIM = 128
D_FF = 5632
DEPTH = 1
DEEPNORM_ALPHA = (2.0 * DEPTH) ** 0.25
LN_EPS = 1e-5
RMS_EPS = 1e-6

GQA_Q_COLS = GQA_Q_HEADS * HEAD_DIM
GQA_KV_COLS = GQA_KV_HEADS * HEAD_DIM
MLA_QK_PAD = 256
LANES = 128
LOG2E = 1.4426950408889634

VMEM_LIMIT = 56 * 1024 * 1024

TM_IN = 256
TV_CHUNK = TM_IN
TK_ATT = 256
ATT_STEPS = 10
SUM_ROWS = 16
EXP_ROWS = 64
TQ_GQA = 1024
TQ_MLA = 4096
TM_MERGE = 512
TN_MERGE = 512
TM_FFN = 512
TF_FFN = 512

BF16 = jnp.bfloat16
F32 = jnp.float32


def _const_spec(shape):
    nd = len(shape)
    return pl.BlockSpec(shape, lambda *_: (0,) * nd, pipeline_mode=pl.Buffered(1))


def _ln(x):
    mu = jnp.mean(x, axis=-1, keepdims=True)
    xc = x - mu
    var = jnp.mean(xc * xc, axis=-1, keepdims=True)
    return xc * lax.rsqrt(var + LN_EPS)


def _rms(x, gain):
    ms = jnp.mean(x * x, axis=-1, keepdims=True)
    return x * lax.rsqrt(ms + RMS_EPS) * gain


def _rope(x, c, s):
    return x * c + pltpu.roll(x, LANES // 2, 1) * s


def _adaln_kernel(c_ref, w_ref, b_ref, o_ref):
    c = c_ref[...]
    act = (c * jax.nn.sigmoid(c)).astype(BF16)
    o_ref[...] = jnp.dot(act, w_ref[...].astype(BF16), preferred_element_type=F32) + b_ref[...]


def _adaln(c_pad, w_ada, b_ada):
    n = w_ada.shape[1]
    tn = 1024
    return pl.pallas_call(
        _adaln_kernel,
        grid=(n // tn,),
        in_specs=[
            pl.BlockSpec((8, D_MODEL), lambda j: (0, 0)),
            pl.BlockSpec((D_MODEL, tn), lambda j: (0, j)),
            pl.BlockSpec((1, tn), lambda j: (0, j)),
        ],
        out_specs=pl.BlockSpec((8, tn), lambda j: (0, j)),
        out_shape=jax.ShapeDtypeStruct((8, n), F32),
        compiler_params=pltpu.CompilerParams(
            dimension_semantics=("arbitrary",), vmem_limit_bytes=VMEM_LIMIT),
        name="adaln",
    )(c_pad, w_ada, b_ada)


def _inproj_kernel(x_ref, mod_ref, cg_ref, sg_ref, cm_ref, sm_ref, w_ref,
                   gq_ref, gk_ref, gql_ref, gkvl_ref, wuq_ref, wuk_ref, wuv_ref,
                   h_ref, qg_ref, kg_ref, vgt_ref, qm_ref, km_ref, vmt_ref, *, mla_scale):
    x = x_ref[...]
    shift = mod_ref[0, 0:1, :]
    scale = mod_ref[0, 1:2, :]
    h = (_ln(x) * (1.0 + scale) + shift).astype(BF16)
    h_ref[...] = h
    proj = jnp.dot(h, w_ref[...], preferred_element_type=F32)

    cg = cg_ref[...]
    sg = sg_ref[...]
    cm = cm_ref[...]
    sm = sm_ref[...]
    ones_rows = jnp.ones((SUM_ROWS, x.shape[0]), BF16)

    for hd in range(GQA_Q_HEADS):
        q = proj[:, hd * HEAD_DIM:(hd + 1) * HEAD_DIM]
        q = _rope(_rms(q, gq_ref[...]), cg, sg)
        qg_ref[0, hd, 0] = q.T.astype(BF16)
    off = GQA_Q_COLS
    for hd in range(GQA_KV_HEADS):
        k = proj[:, off + hd * HEAD_DIM: off + (hd + 1) * HEAD_DIM]
        k = _rope(_rms(k, gk_ref[...]), cg, sg)
        kg_ref[:, hd * HEAD_DIM:(hd + 1) * HEAD_DIM] = k.astype(BF16)
    off += GQA_KV_COLS
    for hd in range(GQA_KV_HEADS):
        v = proj[:, off + hd * HEAD_DIM: off + (hd + 1) * HEAD_DIM]
        vgt_ref[0, hd, 0, 0:HEAD_DIM, :] = v.T.astype(BF16)
        vgt_ref[0, hd, 0, HEAD_DIM:, :] = ones_rows
    off += GQA_KV_COLS

    q_lat = _rms(proj[:, off: off + MLA_Q_RANK], gql_ref[...]).astype(BF16)
    off += MLA_Q_RANK
    kv_lat = _rms(proj[:, off: off + MLA_KV_RANK], gkvl_ref[...]).astype(BF16)
    off += MLA_KV_RANK
    k_rope = _rope(proj[:, off: off + LANES], cm, sm).astype(BF16)

    q_m = jnp.dot(q_lat, wuq_ref[...], preferred_element_type=F32) * mla_scale
    k_n = jnp.dot(kv_lat, wuk_ref[...], preferred_element_type=F32)
    v_m = jnp.dot(kv_lat, wuv_ref[...], preferred_element_type=F32)
    for hd in range(MLA_HEADS):
        b0 = hd * MLA_QK_PAD
        qm_ref[0, hd, 0, 0:LANES, :] = q_m[:, b0: b0 + LANES].T.astype(BF16)
        qm_ref[0, hd, 0, LANES:, :] = _rope(
            q_m[:, b0 + LANES: b0 + 2 * LANES], cm, sm).T.astype(BF16)
        km_ref[:, b0: b0 + LANES] = k_n[:, hd * LANES:(hd + 1) * LANES].astype(BF16)
        km_ref[:, b0 + LANES: b0 + 2 * LANES] = k_rope
        vmt_ref[0, hd, 0, 0:MLA_V_DIM, :] = (
            v_m[:, hd * MLA_V_DIM:(hd + 1) * MLA_V_DIM].T.astype(BF16))
        vmt_ref[0, hd, 0, MLA_V_DIM:, :] = ones_rows


def _inproj(x2, mod, cg, sg, cm, sm, w_attn, gq, gk, gql, gkvl, wuq, wuk, wuv, batch, seq):
    t = x2.shape[0]
    tm = TM_IN
    nt = seq // tm
    n_attn = w_attn.shape[1]
    kern = functools.partial(
        _inproj_kernel, mla_scale=float(LOG2E * (MLA_NOPE_DIM + MLA_ROPE_DIM) ** -0.5))
    tok = lambda i: (i, 0)
    pos = lambda i: (i % nt, 0)
    in_specs = [
        pl.BlockSpec((tm, D_MODEL), tok),
        pl.BlockSpec((1, 6, D_MODEL), lambda i: (i // nt, 0, 0)),
        pl.BlockSpec((tm, LANES), pos),
        pl.BlockSpec((tm, LANES), pos),
        pl.BlockSpec((tm, LANES), pos),
        pl.BlockSpec((tm, LANES), pos),
        _const_spec((D_MODEL, n_attn)),
        _const_spec((1, HEAD_DIM)),
        _const_spec((1, HEAD_DIM)),
        _const_spec((1, MLA_Q_RANK)),
        _const_spec((1, MLA_KV_RANK)),
        _const_spec(wuq.shape),
        _const_spec(wuk.shape),
        _const_spec(wuv.shape),
    ]
    out_shape = [
        jax.ShapeDtypeStruct((t, D_MODEL), BF16),
        jax.ShapeDtypeStruct((batch, GQA_Q_HEADS, seq // tm, HEAD_DIM, tm), BF16),
        jax.ShapeDtypeStruct((t, GQA_KV_COLS), BF16),
        jax.ShapeDtypeStruct(
            (batch, GQA_KV_HEADS, seq // TV_CHUNK, HEAD_DIM + SUM_ROWS, TV_CHUNK), BF16),
        jax.ShapeDtypeStruct((batch, MLA_HEADS, seq // tm, MLA_QK_PAD, tm), BF16),
        jax.ShapeDtypeStruct((t, MLA_HEADS * MLA_QK_PAD), BF16),
        jax.ShapeDtypeStruct(
            (batch, MLA_HEADS, seq // TV_CHUNK, MLA_V_DIM + SUM_ROWS, TV_CHUNK), BF16),
    ]
    vt_map = lambda i: (i // nt, 0, i % nt, 0, 0)
    out_specs = [
        pl.BlockSpec((tm, D_MODEL), tok),
        pl.BlockSpec((1, GQA_Q_HEADS, 1, HEAD_DIM, tm), vt_map),
        pl.BlockSpec((tm, GQA_KV_COLS), tok),
        pl.BlockSpec((1, GQA_KV_HEADS, 1, HEAD_DIM + SUM_ROWS, TV_CHUNK), vt_map),
        pl.BlockSpec((1, MLA_HEADS, 1, MLA_QK_PAD, tm), vt_map),
        pl.BlockSpec((tm, MLA_HEADS * MLA_QK_PAD), tok),
        pl.BlockSpec((1, MLA_HEADS, 1, MLA_V_DIM + SUM_ROWS, TV_CHUNK), vt_map),
    ]
    return pl.pallas_call(
        kern,
        grid=(t // tm,),
        in_specs=in_specs,
        out_specs=out_specs,
        out_shape=out_shape,
        compiler_params=pltpu.CompilerParams(
            dimension_semantics=("arbitrary",), vmem_limit_bytes=VMEM_LIMIT),
        name="inproj",
    )(x2, mod, cg, sg, cm, sm, w_attn, gq, gk, gql, gkvl, wuq, wuk, wuv)


def _attn_kernel(qt_ref, k_ref, vt_ref, o_ref, st_scr, pb_scr, acc_scr,
                 *, groups, dq, dv, tq, tk, nk, steps):
    mq = groups * tq
    cb = qt_ref.shape[-1]
    nb = mq // cb
    cols = [slice(n * cb, (n + 1) * cb) for n in range(nb)]
    grp = [(n * cb) // tq for n in range(nb)]
    tok = [(n * cb) % tq for n in range(nb)]

    def scores(c, st_ref, n):
        kc = k_ref[0, pl.ds(pl.multiple_of(c * tk, tk), tk), :]
        st = jnp.dot(kc, qt_ref[0, grp[n], tok[n] // cb],
                     preferred_element_type=F32)
        st_ref[:, cols[n]] = st
        return jnp.max(st, axis=0, keepdims=True)

    def values(c, pb_ref, alpha, n):
        acc_scr[:, cols[n]] = alpha * acc_scr[:, cols[n]] + jnp.dot(
            vt_ref[0, 0, c], pb_ref[:, cols[n]], preferred_element_type=F32)

    def softmax(st_ref, pb_ref, cmax, m_prev, n):
        m_new = jnp.maximum(m_prev, cmax)
        alpha = jnp.exp2(m_prev - m_new)
        for r in range(tk // EXP_ROWS):
            rows = slice(r * EXP_ROWS, (r + 1) * EXP_ROWS)
            pb_ref[rows, cols[n]] = jnp.exp2((st_ref[rows, cols[n]] - m_new).astype(BF16))
        return m_new, alpha

    def step(c, cur, nxt, carry, first=False, last=False):
        cmax, m_prev, alpha_prev = carry
        cmax_next, m_new, alpha = list(cmax), [], []
        for n in range(nb):
            if not last:
                cmax_next[n] = scores(c + 1, st_scr.at[nxt], n)
        for n in range(nb):
            m_n, a_n = softmax(st_scr.at[cur], pb_scr.at[cur], cmax[n], m_prev[n], n)
            m_new.append(m_n)
            alpha.append(a_n)
            if not first:
                values(c - 1, pb_scr.at[nxt], alpha_prev[n], n)
        return cmax_next, m_new, alpha

    acc_scr[...] = jnp.zeros(acc_scr.shape, F32)
    cmax0 = []
    for n in range(nb):
        cmax0.append(scores(0, st_scr.at[0], n))
    carry = (cmax0, [jnp.full((1, cb), -jnp.inf, F32)] * nb, [jnp.ones((1, cb), F32)] * nb)
    carry = step(0, 0, 1, carry, first=True)

    n_loop = (nk - 2) // steps

    def body(i, carry):
        c0 = steps * i + 1
        for s in range(steps):
            carry = step(c0 + s, (1 + s) % 2, s % 2, carry)
        return carry

    if n_loop == 1:
        carry = body(0, carry)
    elif n_loop > 1:
        carry = lax.fori_loop(0, n_loop, body, carry)
    for c in range(n_loop * steps + 1, nk - 1):
        carry = step(c, c % 2, (c + 1) % 2, carry)
    _, _, alpha = step(nk - 1, (nk - 1) % 2, nk % 2, carry, last=True)
    for n in range(nb):
        values(nk - 1, pb_scr.at[(nk - 1) % 2], alpha[n], n)
        out_t = acc_scr[0:dv, cols[n]] / acc_scr[dv:dv + 1, cols[n]]
        o_ref[0, tok[n]:tok[n] + cb, grp[n] * dv:(grp[n] + 1) * dv] = out_t.T.astype(o_ref.dtype)


def _attention(qt, k, vt, *, kv_heads, dv, tq):
    batch, q_heads, n_qc, dq, cb = qt.shape
    seq = n_qc * cb
    groups = q_heads // kv_heads
    tk = TK_ATT
    assert tk == TV_CHUNK and vt.shape[-2:] == (dv + SUM_ROWS, tk) and tq % cb == 0
    nk = seq // tk
    mq = groups * tq
    kern = functools.partial(_attn_kernel, groups=groups, dq=dq, dv=dv, tq=tq, tk=tk, nk=nk,
                             steps=ATT_STEPS)
    return pl.pallas_call(
        kern,
        grid=(batch, kv_heads, seq // tq),
        in_specs=[
            pl.BlockSpec((1, groups, tq // cb, dq, cb), lambda b, h, i: (b, h, i, 0, 0)),
            pl.BlockSpec((1, seq, dq), lambda b, h, i: (b, 0, h)),
            pl.BlockSpec((1, 1, nk, dv + SUM_ROWS, tk), lambda b, h, i: (b, h, 0, 0, 0)),
        ],
        out_specs=pl.BlockSpec((1, tq, groups * dv), lambda b, h, i: (b, i, h)),
        out_shape=jax.ShapeDtypeStruct((batch, seq, q_heads * dv), BF16),
        scratch_shapes=[
            pltpu.VMEM((2, tk, mq), F32),
            pltpu.VMEM((2, tk, mq), BF16),
            pltpu.VMEM((dv + SUM_ROWS, mq), F32),
        ],
        compiler_params=pltpu.CompilerParams(
            dimension_semantics=("arbitrary", "arbitrary", "arbitrary"),
            vmem_limit_bytes=VMEM_LIMIT),
        name=f"attn_g{groups}",
    )(qt, k, vt)


def _merge_kernel(x_ref, mod_ref, h_ref, yg_ref, ym_ref, wga_ref, wgb_ref, bga_ref, bgb_ref,
                  wbg_ref, wbm_ref, wo_ref, g_ref, b_ref, o_ref, h_scr, acc_scr):
    j = pl.program_id(1)

    @pl.when(j == 0)
    def _():
        h_scr[...] = h_ref[...]
        acc_scr[...] = jnp.zeros(acc_scr.shape, F32)

    h = h_scr[...]
    la = jnp.dot(h, wga_ref[...], preferred_element_type=F32) + bga_ref[...]
    lb = jnp.dot(h, wgb_ref[...], preferred_element_type=F32) + bgb_ref[...]
    a = jnp.dot(yg_ref[...], wbg_ref[...], preferred_element_type=F32)
    b = jnp.dot(ym_ref[...], wbm_ref[...], preferred_element_type=F32)
    merged = jax.nn.sigmoid(la) * a + jax.nn.sigmoid(lb) * b
    acc_scr[...] += jnp.dot(merged.astype(BF16), wo_ref[...], preferred_element_type=F32)

    @pl.when(j == pl.num_programs(1) - 1)
    def _():
        gate = mod_ref[0, 2:3, :]
        r = DEEPNORM_ALPHA * x_ref[...] + gate * acc_scr[...]
        o_ref[...] = _ln(r) * g_ref[...] + b_ref[...]


def _merge(x2, mod, h, yg, ym, w_gate, b_gate, wbg, wbm, wo, ln_g, ln_b, seq):
    t = x2.shape[0]
    tm, tn = TM_MERGE, TN_MERGE
    nt = seq // tm
    nj = D_MODEL // tn
    tok = lambda i, j: (i, 0)
    return pl.pallas_call(
        _merge_kernel,
        grid=(t // tm, nj),
        in_specs=[
            pl.BlockSpec((tm, D_MODEL), tok),
            pl.BlockSpec((1, 6, D_MODEL), lambda i, j: (i // nt, 0, 0)),
            pl.BlockSpec((tm, D_MODEL), tok),
            pl.BlockSpec((tm, yg.shape[1]), tok),
            pl.BlockSpec((tm, ym.shape[1]), tok),
            pl.BlockSpec((D_MODEL, tn), lambda i, j: (0, j)),
            pl.BlockSpec((D_MODEL, tn), lambda i, j: (0, j + nj)),
            pl.BlockSpec((1, tn), lambda i, j: (0, j)),
            pl.BlockSpec((1, tn), lambda i, j: (0, j + nj)),
            pl.BlockSpec((wbg.shape[0], tn), lambda i, j: (0, j)),
            pl.BlockSpec((wbm.shape[0], tn), lambda i, j: (0, j)),
            pl.BlockSpec((tn, D_MODEL), lambda i, j: (j, 0)),
            pl.BlockSpec((1, D_MODEL), lambda i, j: (0, 0)),
            pl.BlockSpec((1, D_MODEL), lambda i, j: (0, 0)),
        ],
        out_specs=pl.BlockSpec((tm, D_MODEL), tok),
        out_shape=jax.ShapeDtypeStruct((t, D_MODEL), F32),
        scratch_shapes=[pltpu.VMEM((tm, D_MODEL), BF16), pltpu.VMEM((tm, D_MODEL), F32)],
        compiler_params=pltpu.CompilerParams(
            dimension_semantics=("arbitrary", "arbitrary"), vmem_limit_bytes=VMEM_LIMIT),
        name="merge",
    )(x2, mod, h, yg, ym, w_gate, w_gate, b_gate, b_gate, wbg, wbm, wo, ln_g, ln_b)


def _ffn_kernel(x_ref, mod_ref, wg_ref, wu_ref, wd_ref, g_ref, b_ref, o_ref, h_scr, acc_scr):
    j = pl.program_id(1)

    @pl.when(j == 0)
    def _():
        shift = mod_ref[0, 3:4, :]
        scale = mod_ref[0, 4:5, :]
        h_scr[...] = (_ln(x_ref[...]) * (1.0 + scale) + shift).astype(BF16)
        acc_scr[...] = jnp.zeros(acc_scr.shape, F32)

    h = h_scr[...]
    gt = jnp.dot(h, wg_ref[...], preferred_element_type=F32)
    up = jnp.dot(h, wu_ref[...], preferred_element_type=F32)
    a = (gt * jax.nn.sigmoid(gt) * up).astype(BF16)
    acc_scr[...] += jnp.dot(a, wd_ref[...], preferred_element_type=F32)

    @pl.when(j == pl.num_programs(1) - 1)
    def _():
        gate = mod_ref[0, 5:6, :]
        r = DEEPNORM_ALPHA * x_ref[...] + gate * acc_scr[...]
        o_ref[...] = _ln(r) * g_ref[...] + b_ref[...]


def _ffn(x1, mod, wg, wu, wd, ln_g, ln_b, seq):
    t = x1.shape[0]
    tm, tf = TM_FFN, TF_FFN
    nt = seq // tm
    tok = lambda i, j: (i, 0)
    return pl.pallas_call(
        _ffn_kernel,
        grid=(t // tm, D_FF // tf),
        in_specs=[
            pl.BlockSpec((tm, D_MODEL), tok),
            pl.BlockSpec((1, 6, D_MODEL), lambda i, j: (i // nt, 0, 0)),
            pl.BlockSpec((D_MODEL, tf), lambda i, j: (0, j)),
            pl.BlockSpec((D_MODEL, tf), lambda i, j: (0, j)),
            pl.BlockSpec((tf, D_MODEL), lambda i, j: (j, 0)),
            pl.BlockSpec((1, D_MODEL), lambda i, j: (0, 0)),
            pl.BlockSpec((1, D_MODEL), lambda i, j: (0, 0)),
        ],
        out_specs=pl.BlockSpec((tm, D_MODEL), tok),
        out_shape=jax.ShapeDtypeStruct((t, D_MODEL), F32),
        scratch_shapes=[pltpu.VMEM((tm, D_MODEL), BF16), pltpu.VMEM((tm, D_MODEL), F32)],
        compiler_params=pltpu.CompilerParams(
            dimension_semantics=("arbitrary", "arbitrary"), vmem_limit_bytes=VMEM_LIMIT),
        name="ffn",
    )(x1, mod, wg, wu, wd, ln_g, ln_b)


def _rope_tables(seq, dim):
    rows = seq // GRID_W
    quarter = dim // 4
    inv_freq = ROPE_THETA ** (-jnp.arange(quarter, dtype=F32) / quarter)
    row_ang = jnp.arange(rows, dtype=F32)[:, None] * inv_freq
    col_ang = jnp.arange(GRID_W, dtype=F32)[:, None] * inv_freq

    def table(fn):
        return jnp.concatenate([
            jnp.broadcast_to(fn(row_ang)[:, None, :], (rows, GRID_W, quarter)),
            jnp.broadcast_to(fn(col_ang)[None, :, :], (rows, GRID_W, quarter)),
        ], axis=-1).reshape(seq, 2 * quarter)

    half = dim // 2
    pad = jnp.zeros((seq, LANES // 2 - half), F32)
    cos, sin = table(jnp.cos), table(jnp.sin)
    c = jnp.concatenate([cos, pad, cos, pad], axis=-1)
    s = jnp.concatenate([-sin, pad, sin, pad], axis=-1)
    return c, s


def _deinterleave(n):
    return np.concatenate([np.arange(0, n, 2), np.arange(1, n, 2)])


def kernel(x, c, w_ada, b_ada, w_in, b_gates, gqa_q_gain, gqa_k_gain, mla_q_gain, mla_kv_gain,
           w_mla_uq, w_mla_ukv, w_branch_gqa, w_branch_mla, w_out, ln1_g, ln1_b,
           w_ffn_gate, w_ffn_up, w_ffn_down, ln2_g, ln2_b):
    batch, seq, d = x.shape
    assert d == D_MODEL and w_ada.shape[0] == DEPTH
    t = batch * seq
    x2 = x.reshape(t, d)

    cg, sg = _rope_tables(seq, HEAD_DIM)
    cm, sm = _rope_tables(seq, MLA_ROPE_DIM)
    perm_head = _deinterleave(HEAD_DIM)
    zeros32 = lambda rows: jnp.zeros((rows, LANES // 2 - MLA_ROPE_DIM // 2), BF16)

    c_pad = jnp.zeros((8, d), F32).at[:batch].set(c)

    for l in range(DEPTH):
        mod = _adaln(c_pad, w_ada[l], b_ada[l][None, :])[:batch].reshape(batch, 6, d)

        wl = w_in[l]
        o0 = 0
        wq = wl[:, o0:o0 + GQA_Q_COLS].reshape(d, GQA_Q_HEADS, HEAD_DIM)[:, :, perm_head]
        o0 += GQA_Q_COLS
        wk = wl[:, o0:o0 + GQA_KV_COLS].reshape(d, GQA_KV_HEADS, HEAD_DIM)[:, :, perm_head]
        o0 += GQA_KV_COLS
        wv = wl[:, o0:o0 + GQA_KV_COLS]
        o0 += GQA_KV_COLS
        wql = wl[:, o0:o0 + MLA_Q_RANK]
        o0 += MLA_Q_RANK
        wkvl = wl[:, o0:o0 + MLA_KV_RANK]
        o0 += MLA_KV_RANK
        wkr = wl[:, o0:o0 + MLA_ROPE_DIM].astype(BF16)
        o0 += MLA_ROPE_DIM
        w_gate = wl[:, o0:].astype(BF16)
        wkr_pad = jnp.concatenate(
            [wkr[:, 0::2], zeros32(d), wkr[:, 1::2], zeros32(d)], axis=-1)
        w_attn = jnp.concatenate([
            wq.reshape(d, GQA_Q_COLS).astype(BF16), wk.reshape(d, GQA_KV_COLS).astype(BF16),
            wv.astype(BF16), wql.astype(BF16), wkvl.astype(BF16), wkr_pad], axis=-1)

        uq = w_mla_uq[l].astype(BF16).reshape(MLA_Q_RANK, MLA_HEADS, MLA_NOPE_DIM + MLA_ROPE_DIM)
        uq_r = uq[:, :, MLA_NOPE_DIM:]
        z = jnp.zeros((MLA_Q_RANK, MLA_HEADS, LANES // 2 - MLA_ROPE_DIM // 2), BF16)
        wuq = jnp.concatenate(
            [uq[:, :, :MLA_NOPE_DIM], uq_r[:, :, 0::2], z, uq_r[:, :, 1::2], z],
            axis=-1).reshape(MLA_Q_RANK, MLA_HEADS * MLA_QK_PAD)
        ukv = w_mla_ukv[l].astype(BF16).reshape(MLA_KV_RANK, MLA_HEADS, MLA_NOPE_DIM + MLA_V_DIM)
        wuk = ukv[:, :, :MLA_NOPE_DIM].reshape(MLA_KV_RANK, MLA_HEADS * MLA_NOPE_DIM)
        wuv = ukv[:, :, MLA_NOPE_DIM:].reshape(MLA_KV_RANK, MLA_HEADS * MLA_V_DIM)

        gq = (gqa_q_gain[l][perm_head] * (LOG2E * HEAD_DIM ** -0.5))[None, :]
        gk = gqa_k_gain[l][perm_head][None, :]

        h1, qgt, kg, vgt, qmt, km, vmt = _inproj(
            x2, mod, cg, sg, cm, sm, w_attn, gq, gk, mla_q_gain[l][None, :],
            mla_kv_gain[l][None, :], wuq, wuk, wuv, batch, seq)

        y_gqa = _attention(
            qgt, kg.reshape(batch, seq, GQA_KV_COLS), vgt,
            kv_heads=GQA_KV_HEADS, dv=HEAD_DIM, tq=TQ_GQA)
        y_mla = _attention(
            qmt, km.reshape(batch, seq, MLA_HEADS * MLA_QK_PAD), vmt,
            kv_heads=MLA_HEADS, dv=MLA_V_DIM, tq=TQ_MLA)

        x2 = _merge(
            x2, mod, h1, y_gqa.reshape(t, GQA_Q_COLS), y_mla.reshape(t, MLA_HEADS * MLA_V_DIM),
            w_gate, b_gates[l][None, :], w_branch_gqa[l].astype(BF16),
            w_branch_mla[l].astype(BF16), w_out[l].astype(BF16),
            ln1_g[l][None, :], ln1_b[l][None, :], seq)

        x2 = _ffn(
            x2, mod, w_ffn_gate[l].astype(BF16), w_ffn_up[l].astype(BF16),
            w_ffn_down[l].astype(BF16), ln2_g[l][None, :], ln2_b[l][None, :], seq)

    return x2.reshape(batch, seq, d)
```

```python
import functools

import jax
import jax.numpy as jnp
import numpy as np
from jax import lax
from jax.experimental import pallas as pl
from jax.experimental.pallas import tpu as pltpu

D_MODEL = 2048
GRID_W = 64
ROPE_THETA = 10000.0
HEAD_DIM = 128
GQA_Q_HEADS = 8
GQA_KV_HEADS = 2
MLA_HEADS = 8
MLA_Q_RANK = 512
MLA_KV_RANK = 512
MLA_NOPE_DIM = 128
MLA_ROPE_DIM = 64
MLA_V_DIM = 128
D_FF = 5632
DEPTH = 1
DEEPNORM_ALPHA = (2.0 * DEPTH) ** 0.25
LN_EPS = 1e-5
RMS_EPS = 1e-6

GQA_Q_COLS = GQA_Q_HEADS * HEAD_DIM
GQA_KV_COLS = GQA_KV_HEADS * HEAD_DIM
MLA_QK_PAD = 256
LANES = 128
LOG2E = 1.4426950408889634

VMEM_LIMIT = 56 * 1024 * 1024

TM_IN = 256
TV_CHUNK = TM_IN
TK_ATT = 256
ATT_STEPS = 10
SUM_ROWS = 16
EXP_ROWS = 64
TQ_GQA = 1024
TQ_MLA = 4096
TM_MERGE = 512
TN_MERGE = 512
TM_FFN = 512
TF_FFN = 512

BF16 = jnp.bfloat16
F32 = jnp.float32


def _const_spec(shape):
    nd = len(shape)
    return pl.BlockSpec(shape, lambda *_: (0,) * nd, pipeline_mode=pl.Buffered(1))


def _ln(x):
    mu = jnp.mean(x, axis=-1, keepdims=True)
    xc = x - mu
    var = jnp.mean(xc * xc, axis=-1, keepdims=True)
    return xc * lax.rsqrt(var + LN_EPS)


def _rms(x, gain):
    ms = jnp.mean(x * x, axis=-1, keepdims=True)
    return x * lax.rsqrt(ms + RMS_EPS) * gain


def _rope(x, c, s):
    return x * c + pltpu.roll(x, LANES // 2, 1) * s


def _adaln_kernel(c_ref, w_ref, b_ref, o_ref):
    c = c_ref[...]
    act = (c * jax.nn.sigmoid(c)).astype(BF16)
    o_ref[...] = jnp.dot(act, w_ref[...].astype(BF16), preferred_element_type=F32) + b_ref[...]


def _adaln(c_pad, w_ada, b_ada):
    n = w_ada.shape[1]
    tn = 1024
    return pl.pallas_call(
        _adaln_kernel,
        grid=(n // tn,),
        in_specs=[
            pl.BlockSpec((8, D_MODEL), lambda j: (0, 0)),
            pl.BlockSpec((D_MODEL, tn), lambda j: (0, j)),
            pl.BlockSpec((1, tn), lambda j: (0, j)),
        ],
        out_specs=pl.BlockSpec((8, tn), lambda j: (0, j)),
        out_shape=jax.ShapeDtypeStruct((8, n), F32),
        compiler_params=pltpu.CompilerParams(
            dimension_semantics=("arbitrary",), vmem_limit_bytes=VMEM_LIMIT),
        name="adaln",
    )(c_pad, w_ada, b_ada)


def _inproj_kernel(x_ref, mod_ref, cg_ref, sg_ref, cm_ref, sm_ref, w_ref,
                   gq_ref, gk_ref, gql_ref, gkvl_ref, wuq_ref, wuk_ref, wuv_ref,
                   h_ref, qg_ref, kg_ref, vgt_ref, qm_ref, km_ref, vmt_ref, *, mla_scale):
    x = x_ref[...]
    shift = mod_ref[0, 0:1, :]
    scale = mod_ref[0, 1:2, :]
    h = (_ln(x) * (1.0 + scale) + shift).astype(BF16)
    h_ref[...] = h
    proj = jnp.dot(h, w_ref[...], preferred_element_type=F32)

    cg = cg_ref[...]
    sg = sg_ref[...]
    cm = cm_ref[...]
    sm = sm_ref[...]
    ones_rows = jnp.ones((SUM_ROWS, x.shape[0]), BF16)

    for hd in range(GQA_Q_HEADS):
        q = proj[:, hd * HEAD_DIM:(hd + 1) * HEAD_DIM]
        q = _rope(_rms(q, gq_ref[...]), cg, sg)
        qg_ref[0, hd, 0] = q.T.astype(BF16)
    off = GQA_Q_COLS
    for hd in range(GQA_KV_HEADS):
        k = proj[:, off + hd * HEAD_DIM: off + (hd + 1) * HEAD_DIM]
        k = _rope(_rms(k, gk_ref[...]), cg, sg)
        kg_ref[:, hd * HEAD_DIM:(hd + 1) * HEAD_DIM] = k.astype(BF16)
    off += GQA_KV_COLS
    for hd in range(GQA_KV_HEADS):
        v = proj[:, off + hd * HEAD_DIM: off + (hd + 1) * HEAD_DIM]
        vgt_ref[0, hd, 0, 0:HEAD_DIM, :] = v.T.astype(BF16)
        vgt_ref[0, hd, 0, HEAD_DIM:, :] = ones_rows
    off += GQA_KV_COLS

    q_lat = _rms(proj[:, off: off + MLA_Q_RANK], gql_ref[...]).astype(BF16)
    off += MLA_Q_RANK
    kv_lat = _rms(proj[:, off: off + MLA_KV_RANK], gkvl_ref[...]).astype(BF16)
    off += MLA_KV_RANK
    k_rope = _rope(proj[:, off: off + LANES], cm, sm).astype(BF16)

    q_m = jnp.dot(q_lat, wuq_ref[...], preferred_element_type=F32) * mla_scale
    k_n = jnp.dot(kv_lat, wuk_ref[...], preferred_element_type=F32)
    v_m = jnp.dot(kv_lat, wuv_ref[...], preferred_element_type=F32)
    for hd in range(MLA_HEADS):
        b0 = hd * MLA_QK_PAD
        qm_ref[0, hd, 0, 0:LANES, :] = q_m[:, b0: b0 + LANES].T.astype(BF16)
        qm_ref[0, hd, 0, LANES:, :] = _rope(
            q_m[:, b0 + LANES: b0 + 2 * LANES], cm, sm).T.astype(BF16)
        km_ref[:, b0: b0 + LANES] = k_n[:, hd * LANES:(hd + 1) * LANES].astype(BF16)
        km_ref[:, b0 + LANES: b0 + 2 * LANES] = k_rope
        vmt_ref[0, hd, 0, 0:MLA_V_DIM, :] = (
            v_m[:, hd * MLA_V_DIM:(hd + 1) * MLA_V_DIM].T.astype(BF16))
        vmt_ref[0, hd, 0, MLA_V_DIM:, :] = ones_rows


def _inproj(x2, mod, cg, sg, cm, sm, w_attn, gq, gk, gql, gkvl, wuq, wuk, wuv, batch, seq):
    t = x2.shape[0]
    tm = TM_IN
    nt = seq // tm
    n_attn = w_attn.shape[1]
    kern = functools.partial(
        _inproj_kernel, mla_scale=float(LOG2E * (MLA_NOPE_DIM + MLA_ROPE_DIM) ** -0.5))
    tok = lambda i: (i, 0)
    pos = lambda i: (i % nt, 0)
    in_specs = [
        pl.BlockSpec((tm, D_MODEL), tok),
        pl.BlockSpec((1, 6, D_MODEL), lambda i: (i // nt, 0, 0)),
        pl.BlockSpec((tm, LANES), pos),
        pl.BlockSpec((tm, LANES), pos),
        pl.BlockSpec((tm, LANES), pos),
        pl.BlockSpec((tm, LANES), pos),
        _const_spec((D_MODEL, n_attn)),
        _const_spec((1, HEAD_DIM)),
        _const_spec((1, HEAD_DIM)),
        _const_spec((1, MLA_Q_RANK)),
        _const_spec((1, MLA_KV_RANK)),
        _const_spec(wuq.shape),
        _const_spec(wuk.shape),
        _const_spec(wuv.shape),
    ]
    out_shape = [
        jax.ShapeDtypeStruct((t, D_MODEL), BF16),
        jax.ShapeDtypeStruct((batch, GQA_Q_HEADS, seq // tm, HEAD_DIM, tm), BF16),
        jax.ShapeDtypeStruct((t, GQA_KV_COLS), BF16),
        jax.ShapeDtypeStruct(
            (batch, GQA_KV_HEADS, seq // TV_CHUNK, HEAD_DIM + SUM_ROWS, TV_CHUNK), BF16),
        jax.ShapeDtypeStruct((batch, MLA_HEADS, seq // tm, MLA_QK_PAD, tm), BF16),
        jax.ShapeDtypeStruct((t, MLA_HEADS * MLA_QK_PAD), BF16),
        jax.ShapeDtypeStruct(
            (batch, MLA_HEADS, seq // TV_CHUNK, MLA_V_DIM + SUM_ROWS, TV_CHUNK), BF16),
    ]
    vt_map = lambda i: (i // nt, 0, i % nt, 0, 0)
    out_specs = [
        pl.BlockSpec((tm, D_MODEL), tok),
        pl.BlockSpec((1, GQA_Q_HEADS, 1, HEAD_DIM, tm), vt_map),
        pl.BlockSpec((tm, GQA_KV_COLS), tok),
        pl.BlockSpec((1, GQA_KV_HEADS, 1, HEAD_DIM + SUM_ROWS, TV_CHUNK), vt_map),
        pl.BlockSpec((1, MLA_HEADS, 1, MLA_QK_PAD, tm), vt_map),
        pl.BlockSpec((tm, MLA_HEADS * MLA_QK_PAD), tok),
        pl.BlockSpec((1, MLA_HEADS, 1, MLA_V_DIM + SUM_ROWS, TV_CHUNK), vt_map),
    ]
    return pl.pallas_call(
        kern,
        grid=(t // tm,),
        in_specs=in_specs,
        out_specs=out_specs,
        out_shape=out_shape,
        compiler_params=pltpu.CompilerParams(
            dimension_semantics=("arbitrary",), vmem_limit_bytes=VMEM_LIMIT),
        name="inproj",
    )(x2, mod, cg, sg, cm, sm, w_attn, gq, gk, gql, gkvl, wuq, wuk, wuv)


def _attn_kernel(qt_ref, k_ref, vt_ref, o_ref, st_scr, pb_scr, acc_scr,
                 *, groups, dq, dv, tq, tk, nk, steps):
    mq = groups * tq
    cb = qt_ref.shape[-1]
    nb = mq // cb
    cols = [slice(n * cb, (n + 1) * cb) for n in range(nb)]
    grp = [(n * cb) // tq for n in range(nb)]
    tok = [(n * cb) % tq for n in range(nb)]

    def scores(c, st_ref, n):
        kc = k_ref[0, pl.ds(pl.multiple_of(c * tk, tk), tk), :]
        st = jnp.dot(kc, qt_ref[0, grp[n], tok[n] // cb],
                     preferred_element_type=F32)
        st_ref[:, cols[n]] = st
        return jnp.max(st, axis=0, keepdims=True)

    def values(c, pb_ref, alpha, n):
        acc_scr[:, cols[n]] = alpha * acc_scr[:, cols[n]] + jnp.dot(
            vt_ref[0, 0, c], pb_ref[:, cols[n]], preferred_element_type=F32)

    def softmax(st_ref, pb_ref, cmax, m_prev, n):
        m_new = jnp.maximum(m_prev, cmax)
        alpha = jnp.exp2(m_prev - m_new)
        for r in range(tk // EXP_ROWS):
            rows = slice(r * EXP_ROWS, (r + 1) * EXP_ROWS)
            pb_ref[rows, cols[n]] = jnp.exp2((st_ref[rows, cols[n]] - m_new).astype(BF16))
        return m_new, alpha

    def step(c, cur, nxt, carry, first=False, last=False):
        cmax, m_prev, alpha_prev = carry
        cmax_next, m_new, alpha = list(cmax), [], []
        for n in range(nb):
            if not last:
                cmax_next[n] = scores(c + 1, st_scr.at[nxt], n)
        for n in range(nb):
            m_n, a_n = softmax(st_scr.at[cur], pb_scr.at[cur], cmax[n], m_prev[n], n)
            m_new.append(m_n)
            alpha.append(a_n)
            if not first:
                values(c - 1, pb_scr.at[nxt], alpha_prev[n], n)
        return cmax_next, m_new, alpha

    acc_scr[...] = jnp.zeros(acc_scr.shape, F32)
    cmax0 = []
    for n in range(nb):
        cmax0.append(scores(0, st_scr.at[0], n))
    carry = (cmax0, [jnp.full((1, cb), -jnp.inf, F32)] * nb, [jnp.ones((1, cb), F32)] * nb)
    carry = step(0, 0, 1, carry, first=True)

    n_loop = (nk - 2) // steps

    def body(i, carry):
        c0 = steps * i + 1
        for s in range(steps):
            carry = step(c0 + s, (1 + s) % 2, s % 2, carry)
        return carry

    if n_loop == 1:
        carry = body(0, carry)
    elif n_loop > 1:
        carry = lax.fori_loop(0, n_loop, body, carry)
    for c in range(n_loop * steps + 1, nk - 1):
        carry = step(c, c % 2, (c + 1) % 2, carry)
    _, _, alpha = step(nk - 1, (nk - 1) % 2, nk % 2, carry, last=True)
    for n in range(nb):
        values(nk - 1, pb_scr.at[(nk - 1) % 2], alpha[n], n)
        out_t = acc_scr[0:dv, cols[n]] / acc_scr[dv:dv + 1, cols[n]]
        o_ref[0, grp[n], :, tok[n]:tok[n] + cb] = out_t.astype(o_ref.dtype)


def _attention(qt, k, vt, *, kv_heads, dv, tq):
    batch, q_heads, n_qc, dq, cb = qt.shape
    seq = n_qc * cb
    groups = q_heads // kv_heads
    tk = TK_ATT
    assert tk == TV_CHUNK and vt.shape[-2:] == (dv + SUM_ROWS, tk) and tq % cb == 0
    nk = seq // tk
    mq = groups * tq
    kern = functools.partial(_attn_kernel, groups=groups, dq=dq, dv=dv, tq=tq, tk=tk, nk=nk,
                             steps=ATT_STEPS)
    return pl.pallas_call(
        kern,
        grid=(batch, kv_heads, seq // tq),
        in_specs=[
            pl.BlockSpec((1, groups, tq // cb, dq, cb), lambda b, h, i: (b, h, i, 0, 0)),
            pl.BlockSpec((1, seq, dq), lambda b, h, i: (b, 0, h)),
            pl.BlockSpec((1, 1, nk, dv + SUM_ROWS, tk), lambda b, h, i: (b, h, 0, 0, 0)),
        ],
        out_specs=pl.BlockSpec((1, groups, dv, tq), lambda b, h, i: (b, h, 0, i)),
        out_shape=jax.ShapeDtypeStruct((batch, q_heads, dv, seq), BF16),
        scratch_shapes=[
            pltpu.VMEM((2, tk, mq), F32),
            pltpu.VMEM((2, tk, mq), BF16),
            pltpu.VMEM((dv + SUM_ROWS, mq), F32),
        ],
        compiler_params=pltpu.CompilerParams(
            dimension_semantics=("arbitrary", "arbitrary", "arbitrary"),
            vmem_limit_bytes=VMEM_LIMIT),
        name=f"attn_g{groups}",
    )(qt, k, vt)


def _merge_kernel(x_ref, mod_ref, h_ref, ygt_ref, ymt_ref, wga_ref, wgb_ref, bga_ref, bgb_ref,
                  wbg_ref, wbm_ref, wo_ref, g_ref, b_ref, o_ref, h_scr, acc_scr):
    j = pl.program_id(1)

    @pl.when(j == 0)
    def _():
        h_scr[...] = h_ref[...]
        acc_scr[...] = jnp.zeros(acc_scr.shape, F32)

    h = h_scr[...]
    la = jnp.dot(h, wga_ref[...], preferred_element_type=F32) + bga_ref[...]
    lb = jnp.dot(h, wgb_ref[...], preferred_element_type=F32) + bgb_ref[...]
    tn_dims = (((0,), (0,)), ((), ()))
    ygt = ygt_ref[0].reshape(-1, ygt_ref.shape[-1])
    ymt = ymt_ref[0].reshape(-1, ymt_ref.shape[-1])
    a = lax.dot_general(ygt, wbg_ref[...], tn_dims, preferred_element_type=F32)
    b = lax.dot_general(ymt, wbm_ref[...], tn_dims, preferred_element_type=F32)
    merged = jax.nn.sigmoid(la) * a + jax.nn.sigmoid(lb) * b
    acc_scr[...] += jnp.dot(merged.astype(BF16), wo_ref[...], preferred_element_type=F32)

    @pl.when(j == pl.num_programs(1) - 1)
    def _():
        gate = mod_ref[0, 2:3, :]
        r = DEEPNORM_ALPHA * x_ref[...] + gate * acc_scr[...]
        o_ref[...] = _ln(r) * g_ref[...] + b_ref[...]


def _merge(x2, mod, h, yg, ym, w_gate, b_gate, wbg, wbm, wo, ln_g, ln_b, seq):
    t = x2.shape[0]
    tm, tn = TM_MERGE, TN_MERGE
    nt = seq // tm
    nj = D_MODEL // tn
    tok = lambda i, j: (i, 0)
    return pl.pallas_call(
        _merge_kernel,
        grid=(t // tm, nj),
        in_specs=[
            pl.BlockSpec((tm, D_MODEL), tok),
            pl.BlockSpec((1, 6, D_MODEL), lambda i, j: (i // nt, 0, 0)),
            pl.BlockSpec((tm, D_MODEL), tok),
            pl.BlockSpec((1,) + yg.shape[1:3] + (tm,), lambda i, j: (i // nt, 0, 0, i % nt)),
            pl.BlockSpec((1,) + ym.shape[1:3] + (tm,), lambda i, j: (i // nt, 0, 0, i % nt)),
            pl.BlockSpec((D_MODEL, tn), lambda i, j: (0, j)),
            pl.BlockSpec((D_MODEL, tn), lambda i, j: (0, j + nj)),
            pl.BlockSpec((1, tn), lambda i, j: (0, j)),
            pl.BlockSpec((1, tn), lambda i, j: (0, j + nj)),
            pl.BlockSpec((wbg.shape[0], tn), lambda i, j: (0, j)),
            pl.BlockSpec((wbm.shape[0], tn), lambda i, j: (0, j)),
            pl.BlockSpec((tn, D_MODEL), lambda i, j: (j, 0)),
            pl.BlockSpec((1, D_MODEL), lambda i, j: (0, 0)),
            pl.BlockSpec((1, D_MODEL), lambda i, j: (0, 0)),
        ],
        out_specs=pl.BlockSpec((tm, D_MODEL), tok),
        out_shape=jax.ShapeDtypeStruct((t, D_MODEL), F32),
        scratch_shapes=[pltpu.VMEM((tm, D_MODEL), BF16), pltpu.VMEM((tm, D_MODEL), F32)],
        compiler_params=pltpu.CompilerParams(
            dimension_semantics=("arbitrary", "arbitrary"), vmem_limit_bytes=VMEM_LIMIT),
        name="merge",
    )(x2, mod, h, yg, ym, w_gate, w_gate, b_gate, b_gate, wbg, wbm, wo, ln_g, ln_b)


def _ffn_kernel(x_ref, mod_ref, wg_ref, wu_ref, wd_ref, g_ref, b_ref, o_ref, h_scr, acc_scr):
    j = pl.program_id(1)

    @pl.when(j == 0)
    def _():
        shift = mod_ref[0, 3:4, :]
        scale = mod_ref[0, 4:5, :]
        h_scr[...] = (_ln(x_ref[...]) * (1.0 + scale) + shift).astype(BF16)
        acc_scr[...] = jnp.zeros(acc_scr.shape, F32)

    h = h_scr[...]
    gt = jnp.dot(h, wg_ref[...], preferred_element_type=F32)
    up = jnp.dot(h, wu_ref[...], preferred_element_type=F32)
    a = (gt * jax.nn.sigmoid(gt) * up).astype(BF16)
    acc_scr[...] += jnp.dot(a, wd_ref[...], preferred_element_type=F32)

    @pl.when(j == pl.num_programs(1) - 1)
    def _():
        gate = mod_ref[0, 5:6, :]
        r = DEEPNORM_ALPHA * x_ref[...] + gate * acc_scr[...]
        o_ref[...] = _ln(r) * g_ref[...] + b_ref[...]


def _ffn(x1, mod, wg, wu, wd, ln_g, ln_b, seq):
    t = x1.shape[0]
    tm, tf = TM_FFN, TF_FFN
    nt = seq // tm
    tok = lambda i, j: (i, 0)
    return pl.pallas_call(
        _ffn_kernel,
        grid=(t // tm, D_FF // tf),
        in_specs=[
            pl.BlockSpec((tm, D_MODEL), tok),
            pl.BlockSpec((1, 6, D_MODEL), lambda i, j: (i // nt, 0, 0)),
            pl.BlockSpec((D_MODEL, tf), lambda i, j: (0, j)),
            pl.BlockSpec((D_MODEL, tf), lambda i, j: (0, j)),
            pl.BlockSpec((tf, D_MODEL), lambda i, j: (j, 0)),
            pl.BlockSpec((1, D_MODEL), lambda i, j: (0, 0)),
            pl.BlockSpec((1, D_MODEL), lambda i, j: (0, 0)),
        ],
        out_specs=pl.BlockSpec((tm, D_MODEL), tok),
        out_shape=jax.ShapeDtypeStruct((t, D_MODEL), F32),
        scratch_shapes=[pltpu.VMEM((tm, D_MODEL), BF16), pltpu.VMEM((tm, D_MODEL), F32)],
        compiler_params=pltpu.CompilerParams(
            dimension_semantics=("arbitrary", "arbitrary"), vmem_limit_bytes=VMEM_LIMIT),
        name="ffn",
    )(x1, mod, wg, wu, wd, ln_g, ln_b)


def _rope_tables(seq, dim):
    rows = seq // GRID_W
    quarter = dim // 4
    inv_freq = ROPE_THETA ** (-jnp.arange(quarter, dtype=F32) / quarter)
    row_ang = jnp.arange(rows, dtype=F32)[:, None] * inv_freq
    col_ang = jnp.arange(GRID_W, dtype=F32)[:, None] * inv_freq

    def table(fn):
        return jnp.concatenate([
            jnp.broadcast_to(fn(row_ang)[:, None, :], (rows, GRID_W, quarter)),
            jnp.broadcast_to(fn(col_ang)[None, :, :], (rows, GRID_W, quarter)),
        ], axis=-1).reshape(seq, 2 * quarter)

    half = dim // 2
    pad = jnp.zeros((seq, LANES // 2 - half), F32)
    cos, sin = table(jnp.cos), table(jnp.sin)
    c = jnp.concatenate([cos, pad, cos, pad], axis=-1)
    s = jnp.concatenate([-sin, pad, sin, pad], axis=-1)
    return c, s


def _deinterleave(n):
    return np.concatenate([np.arange(0, n, 2), np.arange(1, n, 2)])


def kernel(x, c, w_ada, b_ada, w_in, b_gates, gqa_q_gain, gqa_k_gain, mla_q_gain, mla_kv_gain,
           w_mla_uq, w_mla_ukv, w_branch_gqa, w_branch_mla, w_out, ln1_g, ln1_b,
           w_ffn_gate, w_ffn_up, w_ffn_down, ln2_g, ln2_b):
    batch, seq, d = x.shape
    assert d == D_MODEL and w_ada.shape[0] == DEPTH
    t = batch * seq
    x2 = x.reshape(t, d)

    cg, sg = _rope_tables(seq, HEAD_DIM)
    cm, sm = _rope_tables(seq, MLA_ROPE_DIM)
    perm_head = _deinterleave(HEAD_DIM)
    zeros32 = lambda rows: jnp.zeros((rows, LANES // 2 - MLA_ROPE_DIM // 2), BF16)

    c_pad = jnp.zeros((8, d), F32).at[:batch].set(c)

    for l in range(DEPTH):
        mod = _adaln(c_pad, w_ada[l], b_ada[l][None, :])[:batch].reshape(batch, 6, d)

        wl = w_in[l]
        o0 = 0
        wq = wl[:, o0:o0 + GQA_Q_COLS].reshape(d, GQA_Q_HEADS, HEAD_DIM)[:, :, perm_head]
        o0 += GQA_Q_COLS
        wk = wl[:, o0:o0 + GQA_KV_COLS].reshape(d, GQA_KV_HEADS, HEAD_DIM)[:, :, perm_head]
        o0 += GQA_KV_COLS
        wv = wl[:, o0:o0 + GQA_KV_COLS]
        o0 += GQA_KV_COLS
        wql = wl[:, o0:o0 + MLA_Q_RANK]
        o0 += MLA_Q_RANK
        wkvl = wl[:, o0:o0 + MLA_KV_RANK]
        o0 += MLA_KV_RANK
        wkr = wl[:, o0:o0 + MLA_ROPE_DIM].astype(BF16)
        o0 += MLA_ROPE_DIM
        w_gate = wl[:, o0:].astype(BF16)
        wkr_pad = jnp.concatenate(
            [wkr[:, 0::2], zeros32(d), wkr[:, 1::2], zeros32(d)], axis=-1)
        w_attn = jnp.concatenate([
            wq.reshape(d, GQA_Q_COLS).astype(BF16), wk.reshape(d, GQA_KV_COLS).astype(BF16),
            wv.astype(BF16), wql.astype(BF16), wkvl.astype(BF16), wkr_pad], axis=-1)

        uq = w_mla_uq[l].astype(BF16).reshape(MLA_Q_RANK, MLA_HEADS, MLA_NOPE_DIM + MLA_ROPE_DIM)
        uq_r = uq[:, :, MLA_NOPE_DIM:]
        z = jnp.zeros((MLA_Q_RANK, MLA_HEADS, LANES // 2 - MLA_ROPE_DIM // 2), BF16)
        wuq = jnp.concatenate(
            [uq[:, :, :MLA_NOPE_DIM], uq_r[:, :, 0::2], z, uq_r[:, :, 1::2], z],
            axis=-1).reshape(MLA_Q_RANK, MLA_HEADS * MLA_QK_PAD)
        ukv = w_mla_ukv[l].astype(BF16).reshape(MLA_KV_RANK, MLA_HEADS, MLA_NOPE_DIM + MLA_V_DIM)
        wuk = ukv[:, :, :MLA_NOPE_DIM].reshape(MLA_KV_RANK, MLA_HEADS * MLA_NOPE_DIM)
        wuv = ukv[:, :, MLA_NOPE_DIM:].reshape(MLA_KV_RANK, MLA_HEADS * MLA_V_DIM)

        gq = (gqa_q_gain[l][perm_head] * (LOG2E * HEAD_DIM ** -0.5))[None, :]
        gk = gqa_k_gain[l][perm_head][None, :]

        h1, qgt, kg, vgt, qmt, km, vmt = _inproj(
            x2, mod, cg, sg, cm, sm, w_attn, gq, gk, mla_q_gain[l][None, :],
            mla_kv_gain[l][None, :], wuq, wuk, wuv, batch, seq)

        y_gqa = _attention(
            qgt, kg.reshape(batch, seq, GQA_KV_COLS), vgt,
            kv_heads=GQA_KV_HEADS, dv=HEAD_DIM, tq=TQ_GQA)
        y_mla = _attention(
            qmt, km.reshape(batch, seq, MLA_HEADS * MLA_QK_PAD), vmt,
            kv_heads=MLA_HEADS, dv=MLA_V_DIM, tq=TQ_MLA)

        x2 = _merge(
            x2, mod, h1, y_gqa, y_mla,
            w_gate, b_gates[l][None, :], w_branch_gqa[l].astype(BF16),
            w_branch_mla[l].astype(BF16), w_out[l].astype(BF16),
            ln1_g[l][None, :], ln1_b[l][None, :], seq)

        x2 = _ffn(
            x2, mod, w_ffn_gate[l].astype(BF16), w_ffn_up[l].astype(BF16),
            w_ffn_down[l].astype(BF16), ln2_g[l][None, :], ln2_b[l][None, :], seq)

    return x2.reshape(batch, seq, d)
```

```python
import functools

import jax
import jax.numpy as jnp
import numpy as np
from jax import lax
from jax.experimental import pallas as pl
from jax.experimental.pallas import tpu as pltpu

D_MODEL = 2048
GRID_W = 64
ROPE_THETA = 10000.0
HEAD_DIM = 128
GQA_Q_HEADS = 8
GQA_KV_HEADS = 2
MLA_HEADS = 8
MLA_Q_RANK = 512
MLA_KV_RANK = 512
MLA_NOPE_DIM = 128
MLA_ROPE_DIM = 64
MLA_V_DIM = 128
D_FF = 5632
DEPTH = 1
DEEPNORM_ALPHA = (2.0 * DEPTH) ** 0.25
LN_EPS = 1e-5
RMS_EPS = 1e-6

GQA_Q_COLS = GQA_Q_HEADS * HEAD_DIM
GQA_KV_COLS = GQA_KV_HEADS * HEAD_DIM
MLA_QK_PAD = 256
LANES = 128
LOG2E = 1.4426950408889634

VMEM_LIMIT = 56 * 1024 * 1024

TM_IN = 256
TV_CHUNK = TM_IN
TK_ATT = 256
ATT_STEPS = 14
SUM_ROWS = 16
EXP_ROWS = 64
TQ_GQA = 1024
TQ_MLA = 4096
TM_MERGE = 512
TN_MERGE = 512
TM_FFN = 512
TF_FFN = 512

BF16 = jnp.bfloat16
F32 = jnp.float32


def _const_spec(shape):
    nd = len(shape)
    return pl.BlockSpec(shape, lambda *_: (0,) * nd, pipeline_mode=pl.Buffered(1))


def _ln(x):
    mu = jnp.mean(x, axis=-1, keepdims=True)
    xc = x - mu
    var = jnp.mean(xc * xc, axis=-1, keepdims=True)
    return xc * lax.rsqrt(var + LN_EPS)


def _rms(x, gain):
    ms = jnp.mean(x * x, axis=-1, keepdims=True)
    return x * lax.rsqrt(ms + RMS_EPS) * gain


def _rope(x, c, s):
    return x * c + pltpu.roll(x, LANES // 2, 1) * s


def _adaln_kernel(c_ref, w_ref, b_ref, o_ref):
    c = c_ref[...]
    act = (c * jax.nn.sigmoid(c)).astype(BF16)
    o_ref[...] = jnp.dot(act, w_ref[...].astype(BF16), preferred_element_type=F32) + b_ref[...]


def _adaln(c_pad, w_ada, b_ada):
    n = w_ada.shape[1]
    tn = 1024
    return pl.pallas_call(
        _adaln_kernel,
        grid=(n // tn,),
        in_specs=[
            pl.BlockSpec((8, D_MODEL), lambda j: (0, 0)),
            pl.BlockSpec((D_MODEL, tn), lambda j: (0, j)),
            pl.BlockSpec((1, tn), lambda j: (0, j)),
        ],
        out_specs=pl.BlockSpec((8, tn), lambda j: (0, j)),
        out_shape=jax.ShapeDtypeStruct((8, n), F32),
        compiler_params=pltpu.CompilerParams(
            dimension_semantics=("arbitrary",), vmem_limit_bytes=VMEM_LIMIT),
        name="adaln",
    )(c_pad, w_ada, b_ada)


def _inproj_kernel(x_ref, mod_ref, cg_ref, sg_ref, cm_ref, sm_ref, w_ref,
                   gq_ref, gk_ref, gql_ref, gkvl_ref, wuq_ref, wuk_ref, wuv_ref,
                   h_ref, qg_ref, kg_ref, vgt_ref, qm_ref, km_ref, vmt_ref, *, mla_scale):
    x = x_ref[...]
    shift = mod_ref[0, 0:1, :]
    scale = mod_ref[0, 1:2, :]
    h = (_ln(x) * (1.0 + scale) + shift).astype(BF16)
    h_ref[...] = h
    proj = jnp.dot(h, w_ref[...], preferred_element_type=F32)

    cg = cg_ref[...]
    sg = sg_ref[...]
    cm = cm_ref[...]
    sm = sm_ref[...]
    ones_rows = jnp.ones((SUM_ROWS, x.shape[0]), BF16)

    for hd in range(GQA_Q_HEADS):
        q = proj[:, hd * HEAD_DIM:(hd + 1) * HEAD_DIM]
        q = _rope(_rms(q, gq_ref[...]), cg, sg)
        qg_ref[0, hd, 0] = q.T.astype(BF16)
    off = GQA_Q_COLS
    for hd in range(GQA_KV_HEADS):
        k = proj[:, off + hd * HEAD_DIM: off + (hd + 1) * HEAD_DIM]
        k = _rope(_rms(k, gk_ref[...]), cg, sg)
        kg_ref[:, hd * HEAD_DIM:(hd + 1) * HEAD_DIM] = k.astype(BF16)
    off += GQA_KV_COLS
    for hd in range(GQA_KV_HEADS):
        v = proj[:, off + hd * HEAD_DIM: off + (hd + 1) * HEAD_DIM]
        vgt_ref[0, hd, 0, 0:HEAD_DIM, :] = v.T.astype(BF16)
        vgt_ref[0, hd, 0, HEAD_DIM:, :] = ones_rows
    off += GQA_KV_COLS

    q_lat = _rms(proj[:, off: off + MLA_Q_RANK], gql_ref[...]).astype(BF16)
    off += MLA_Q_RANK
    kv_lat = _rms(proj[:, off: off + MLA_KV_RANK], gkvl_ref[...]).astype(BF16)
    off += MLA_KV_RANK
    k_rope = _rope(proj[:, off: off + LANES], cm, sm).astype(BF16)

    q_m = jnp.dot(q_lat, wuq_ref[...], preferred_element_type=F32) * mla_scale
    k_n = jnp.dot(kv_lat, wuk_ref[...], preferred_element_type=F32)
    v_m = jnp.dot(kv_lat, wuv_ref[...], preferred_element_type=F32)
    for hd in range(MLA_HEADS):
        b0 = hd * MLA_QK_PAD
        qm_ref[0, hd, 0, 0:LANES, :] = q_m[:, b0: b0 + LANES].T.astype(BF16)
        qm_ref[0, hd, 0, LANES:, :] = _rope(
            q_m[:, b0 + LANES: b0 + 2 * LANES], cm, sm).T.astype(BF16)
        km_ref[:, b0: b0 + LANES] = k_n[:, hd * LANES:(hd + 1) * LANES].astype(BF16)
        km_ref[:, b0 + LANES: b0 + 2 * LANES] = k_rope
        vmt_ref[0, hd, 0, 0:MLA_V_DIM, :] = (
            v_m[:, hd * MLA_V_DIM:(hd + 1) * MLA_V_DIM].T.astype(BF16))
        vmt_ref[0, hd, 0, MLA_V_DIM:, :] = ones_rows


def _inproj(x2, mod, cg, sg, cm, sm, w_attn, gq, gk, gql, gkvl, wuq, wuk, wuv, batch, seq):
    t = x2.shape[0]
    tm = TM_IN
    nt = seq // tm
    n_attn = w_attn.shape[1]
    kern = functools.partial(
        _inproj_kernel, mla_scale=float(LOG2E * (MLA_NOPE_DIM + MLA_ROPE_DIM) ** -0.5))
    tok = lambda i: (i, 0)
    pos = lambda i: (i % nt, 0)
    in_specs = [
        pl.BlockSpec((tm, D_MODEL), tok),
        pl.BlockSpec((1, 6, D_MODEL), lambda i: (i // nt, 0, 0)),
        pl.BlockSpec((tm, LANES), pos),
        pl.BlockSpec((tm, LANES), pos),
        pl.BlockSpec((tm, LANES), pos),
        pl.BlockSpec((tm, LANES), pos),
        _const_spec((D_MODEL, n_attn)),
        _const_spec((1, HEAD_DIM)),
        _const_spec((1, HEAD_DIM)),
        _const_spec((1, MLA_Q_RANK)),
        _const_spec((1, MLA_KV_RANK)),
        _const_spec(wuq.shape),
        _const_spec(wuk.shape),
        _const_spec(wuv.shape),
    ]
    out_shape = [
        jax.ShapeDtypeStruct((t, D_MODEL), BF16),
        jax.ShapeDtypeStruct((batch, GQA_Q_HEADS, seq // tm, HEAD_DIM, tm), BF16),
        jax.ShapeDtypeStruct((t, GQA_KV_COLS), BF16),
        jax.ShapeDtypeStruct(
            (batch, GQA_KV_HEADS, seq // TV_CHUNK, HEAD_DIM + SUM_ROWS, TV_CHUNK), BF16),
        jax.ShapeDtypeStruct((batch, MLA_HEADS, seq // tm, MLA_QK_PAD, tm), BF16),
        jax.ShapeDtypeStruct((t, MLA_HEADS * MLA_QK_PAD), BF16),
        jax.ShapeDtypeStruct(
            (batch, MLA_HEADS, seq // TV_CHUNK, MLA_V_DIM + SUM_ROWS, TV_CHUNK), BF16),
    ]
    vt_map = lambda i: (i // nt, 0, i % nt, 0, 0)
    out_specs = [
        pl.BlockSpec((tm, D_MODEL), tok),
        pl.BlockSpec((1, GQA_Q_HEADS, 1, HEAD_DIM, tm), vt_map),
        pl.BlockSpec((tm, GQA_KV_COLS), tok),
        pl.BlockSpec((1, GQA_KV_HEADS, 1, HEAD_DIM + SUM_ROWS, TV_CHUNK), vt_map),
        pl.BlockSpec((1, MLA_HEADS, 1, MLA_QK_PAD, tm), vt_map),
        pl.BlockSpec((tm, MLA_HEADS * MLA_QK_PAD), tok),
        pl.BlockSpec((1, MLA_HEADS, 1, MLA_V_DIM + SUM_ROWS, TV_CHUNK), vt_map),
    ]
    return pl.pallas_call(
        kern,
        grid=(t // tm,),
        in_specs=in_specs,
        out_specs=out_specs,
        out_shape=out_shape,
        compiler_params=pltpu.CompilerParams(
            dimension_semantics=("arbitrary",), vmem_limit_bytes=VMEM_LIMIT),
        name="inproj",
    )(x2, mod, cg, sg, cm, sm, w_attn, gq, gk, gql, gkvl, wuq, wuk, wuv)


def _attn_kernel(qt_ref, k_ref, vt_ref, o_ref, st_scr, pb_scr, acc_scr,
                 *, groups, dq, dv, tq, tk, nk, steps):
    mq = groups * tq
    cb = qt_ref.shape[-1]
    nb = mq // cb
    cols = [slice(n * cb, (n + 1) * cb) for n in range(nb)]
    grp = [(n * cb) // tq for n in range(nb)]
    tok = [(n * cb) % tq for n in range(nb)]

    def scores(c, st_ref, n):
        kc = k_ref[0, pl.ds(pl.multiple_of(c * tk, tk), tk), :]
        st = jnp.dot(kc, qt_ref[0, grp[n], tok[n] // cb],
                     preferred_element_type=F32)
        st_ref[:, cols[n]] = st
        return jnp.max(st, axis=0, keepdims=True)

    def values(c, pb_ref, alpha, n):
        acc_scr[:, cols[n]] = alpha * acc_scr[:, cols[n]] + jnp.dot(
            vt_ref[0, 0, c], pb_ref[:, cols[n]], preferred_element_type=F32)

    def softmax(st_ref, pb_ref, cmax, m_prev, n):
        m_new = jnp.maximum(m_prev, cmax)
        alpha = jnp.exp2(m_prev - m_new)
        for r in range(tk // EXP_ROWS):
            rows = slice(r * EXP_ROWS, (r + 1) * EXP_ROWS)
            pb_ref[rows, cols[n]] = jnp.exp2((st_ref[rows, cols[n]] - m_new).astype(BF16))
        return m_new, alpha

    def step(c, cur, nxt, carry, first=False, last=False):
        cmax, m_prev, alpha_prev = carry
        cmax_next, m_new, alpha = list(cmax), [], []
        for n in range(nb):
            if not last:
                cmax_next[n] = scores(c + 1, st_scr.at[nxt], n)
        for n in range(nb):
            m_n, a_n = softmax(st_scr.at[cur], pb_scr.at[cur], cmax[n], m_prev[n], n)
            m_new.append(m_n)
            alpha.append(a_n)
            if not first:
                values(c - 1, pb_scr.at[nxt], alpha_prev[n], n)
        return cmax_next, m_new, alpha

    acc_scr[...] = jnp.zeros(acc_scr.shape, F32)
    cmax0 = []
    for n in range(nb):
        cmax0.append(scores(0, st_scr.at[0], n))
    carry = (cmax0, [jnp.full((1, cb), -jnp.inf, F32)] * nb, [jnp.ones((1, cb), F32)] * nb)
    carry = step(0, 0, 1, carry, first=True)

    n_loop = (nk - 2) // steps

    def body(i, carry):
        c0 = steps * i + 1
        for s in range(steps):
            carry = step(c0 + s, (1 + s) % 2, s % 2, carry)
        return carry

    if n_loop == 1:
        carry = body(0, carry)
    elif n_loop > 1:
        carry = lax.fori_loop(0, n_loop, body, carry)
    for c in range(n_loop * steps + 1, nk - 1):
        carry = step(c, c % 2, (c + 1) % 2, carry)
    _, _, alpha = step(nk - 1, (nk - 1) % 2, nk % 2, carry, last=True)
    for n in range(nb):
        values(nk - 1, pb_scr.at[(nk - 1) % 2], alpha[n], n)
        out_t = acc_scr[0:dv, cols[n]] / acc_scr[dv:dv + 1, cols[n]]
        o_ref[0, tok[n]:tok[n] + cb, grp[n] * dv:(grp[n] + 1) * dv] = out_t.T.astype(o_ref.dtype)


def _attention(qt, k, vt, *, kv_heads, dv, tq):
    batch, q_heads, n_qc, dq, cb = qt.shape
    seq = n_qc * cb
    groups = q_heads // kv_heads
    tk = TK_ATT
    assert tk == TV_CHUNK and vt.shape[-2:] == (dv + SUM_ROWS, tk) and tq % cb == 0
    nk = seq // tk
    mq = groups * tq
    kern = functools.partial(_attn_kernel, groups=groups, dq=dq, dv=dv, tq=tq, tk=tk, nk=nk,
                             steps=ATT_STEPS)
    return pl.pallas_call(
        kern,
        grid=(batch, kv_heads, seq // tq),
        in_specs=[
            pl.BlockSpec((1, groups, tq // cb, dq, cb), lambda b, h, i: (b, h, i, 0, 0)),
            pl.BlockSpec((1, seq, dq), lambda b, h, i: (b, 0, h)),
            pl.BlockSpec((1, 1, nk, dv + SUM_ROWS, tk), lambda b, h, i: (b, h, 0, 0, 0)),
        ],
        out_specs=pl.BlockSpec((1, tq, groups * dv), lambda b, h, i: (b, i, h)),
        out_shape=jax.ShapeDtypeStruct((batch, seq, q_heads * dv), BF16),
        scratch_shapes=[
            pltpu.VMEM((2, tk, mq), F32),
            pltpu.VMEM((2, tk, mq), BF16),
            pltpu.VMEM((dv + SUM_ROWS, mq), F32),
        ],
        compiler_params=pltpu.CompilerParams(
            dimension_semantics=("arbitrary", "arbitrary", "arbitrary"),
            vmem_limit_bytes=VMEM_LIMIT),
        name=f"attn_g{groups}",
    )(qt, k, vt)


def _merge_kernel(x_ref, mod_ref, h_ref, yg_ref, ym_ref, wga_ref, wgb_ref, bga_ref, bgb_ref,
                  wbg_ref, wbm_ref, wo_ref, g_ref, b_ref, o_ref, h_scr, acc_scr):
    j = pl.program_id(1)

    @pl.when(j == 0)
    def _():
        h_scr[...] = h_ref[...]
        acc_scr[...] = jnp.zeros(acc_scr.shape, F32)

    h = h_scr[...]
    la = jnp.dot(h, wga_ref[...], preferred_element_type=F32) + bga_ref[...]
    lb = jnp.dot(h, wgb_ref[...], preferred_element_type=F32) + bgb_ref[...]
    a = jnp.dot(yg_ref[...], wbg_ref[...], preferred_element_type=F32)
    b = jnp.dot(ym_ref[...], wbm_ref[...], preferred_element_type=F32)
    merged = jax.nn.sigmoid(la) * a + jax.nn.sigmoid(lb) * b
    acc_scr[...] += jnp.dot(merged.astype(BF16), wo_ref[...], preferred_element_type=F32)

    @pl.when(j == pl.num_programs(1) - 1)
    def _():
        gate = mod_ref[0, 2:3, :]
        r = DEEPNORM_ALPHA * x_ref[...] + gate * acc_scr[...]
        o_ref[...] = _ln(r) * g_ref[...] + b_ref[...]


def _merge(x2, mod, h, yg, ym, w_gate, b_gate, wbg, wbm, wo, ln_g, ln_b, seq):
    t = x2.shape[0]
    tm, tn = TM_MERGE, TN_MERGE
    nt = seq // tm
    nj = D_MODEL // tn
    tok = lambda i, j: (i, 0)
    return pl.pallas_call(
        _merge_kernel,
        grid=(t // tm, nj),
        in_specs=[
            pl.BlockSpec((tm, D_MODEL), tok),
            pl.BlockSpec((1, 6, D_MODEL), lambda i, j: (i // nt, 0, 0)),
            pl.BlockSpec((tm, D_MODEL), tok),
            pl.BlockSpec((tm, yg.shape[1]), tok),
            pl.BlockSpec((tm, ym.shape[1]), tok),
            pl.BlockSpec((D_MODEL, tn), lambda i, j: (0, j)),
            pl.BlockSpec((D_MODEL, tn), lambda i, j: (0, j + nj)),
            pl.BlockSpec((1, tn), lambda i, j: (0, j)),
            pl.BlockSpec((1, tn), lambda i, j: (0, j + nj)),
            pl.BlockSpec((wbg.shape[0], tn), lambda i, j: (0, j)),
            pl.BlockSpec((wbm.shape[0], tn), lambda i, j: (0, j)),
            pl.BlockSpec((tn, D_MODEL), lambda i, j: (j, 0)),
            pl.BlockSpec((1, D_MODEL), lambda i, j: (0, 0)),
            pl.BlockSpec((1, D_MODEL), lambda i, j: (0, 0)),
        ],
        out_specs=pl.BlockSpec((tm, D_MODEL), tok),
        out_shape=jax.ShapeDtypeStruct((t, D_MODEL), F32),
        scratch_shapes=[pltpu.VMEM((tm, D_MODEL), BF16), pltpu.VMEM((tm, D_MODEL), F32)],
        compiler_params=pltpu.CompilerParams(
            dimension_semantics=("arbitrary", "arbitrary"), vmem_limit_bytes=VMEM_LIMIT),
        name="merge",
    )(x2, mod, h, yg, ym, w_gate, w_gate, b_gate, b_gate, wbg, wbm, wo, ln_g, ln_b)


def _ffn_kernel(x_ref, mod_ref, wg_ref, wu_ref, wd_ref, g_ref, b_ref, o_ref, h_scr, acc_scr):
    j = pl.program_id(1)

    @pl.when(j == 0)
    def _():
        shift = mod_ref[0, 3:4, :]
        scale = mod_ref[0, 4:5, :]
        h_scr[...] = (_ln(x_ref[...]) * (1.0 + scale) + shift).astype(BF16)
        acc_scr[...] = jnp.zeros(acc_scr.shape, F32)

    h = h_scr[...]
    gt = jnp.dot(h, wg_ref[...], preferred_element_type=F32)
    up = jnp.dot(h, wu_ref[...], preferred_element_type=F32)
    a = (gt * jax.nn.sigmoid(gt) * up).astype(BF16)
    acc_scr[...] += jnp.dot(a, wd_ref[...], preferred_element_type=F32)

    @pl.when(j == pl.num_programs(1) - 1)
    def _():
        gate = mod_ref[0, 5:6, :]
        r = DEEPNORM_ALPHA * x_ref[...] + gate * acc_scr[...]
        o_ref[...] = _ln(r) * g_ref[...] + b_ref[...]


def _ffn(x1, mod, wg, wu, wd, ln_g, ln_b, seq):
    t = x1.shape[0]
    tm, tf = TM_FFN, TF_FFN
    nt = seq // tm
    tok = lambda i, j: (i, 0)
    return pl.pallas_call(
        _ffn_kernel,
        grid=(t // tm, D_FF // tf),
        in_specs=[
            pl.BlockSpec((tm, D_MODEL), tok),
            pl.BlockSpec((1, 6, D_MODEL), lambda i, j: (i // nt, 0, 0)),
            pl.BlockSpec((D_MODEL, tf), lambda i, j: (0, j)),
            pl.BlockSpec((D_MODEL, tf), lambda i, j: (0, j)),
            pl.BlockSpec((tf, D_MODEL), lambda i, j: (j, 0)),
            pl.BlockSpec((1, D_MODEL), lambda i, j: (0, 0)),
            pl.BlockSpec((1, D_MODEL), lambda i, j: (0, 0)),
        ],
        out_specs=pl.BlockSpec((tm, D_MODEL), tok),
        out_shape=jax.ShapeDtypeStruct((t, D_MODEL), F32),
        scratch_shapes=[pltpu.VMEM((tm, D_MODEL), BF16), pltpu.VMEM((tm, D_MODEL), F32)],
        compiler_params=pltpu.CompilerParams(
            dimension_semantics=("arbitrary", "arbitrary"), vmem_limit_bytes=VMEM_LIMIT),
        name="ffn",
    )(x1, mod, wg, wu, wd, ln_g, ln_b)


def _rope_tables(seq, dim):
    rows = seq // GRID_W
    quarter = dim // 4
    inv_freq = ROPE_THETA ** (-jnp.arange(quarter, dtype=F32) / quarter)
    row_ang = jnp.arange(rows, dtype=F32)[:, None] * inv_freq
    col_ang = jnp.arange(GRID_W, dtype=F32)[:, None] * inv_freq

    def table(fn):
        return jnp.concatenate([
            jnp.broadcast_to(fn(row_ang)[:, None, :], (rows, GRID_W, quarter)),
            jnp.broadcast_to(fn(col_ang)[None, :, :], (rows, GRID_W, quarter)),
        ], axis=-1).reshape(seq, 2 * quarter)

    half = dim // 2
    pad = jnp.zeros((seq, LANES // 2 - half), F32)
    cos, sin = table(jnp.cos), table(jnp.sin)
    c = jnp.concatenate([cos, pad, cos, pad], axis=-1)
    s = jnp.concatenate([-sin, pad, sin, pad], axis=-1)
    return c, s


def _deinterleave(n):
    return np.concatenate([np.arange(0, n, 2), np.arange(1, n, 2)])


def kernel(x, c, w_ada, b_ada, w_in, b_gates, gqa_q_gain, gqa_k_gain, mla_q_gain, mla_kv_gain,
           w_mla_uq, w_mla_ukv, w_branch_gqa, w_branch_mla, w_out, ln1_g, ln1_b,
           w_ffn_gate, w_ffn_up, w_ffn_down, ln2_g, ln2_b):
    batch, seq, d = x.shape
    assert d == D_MODEL and w_ada.shape[0] == DEPTH
    t = batch * seq
    x2 = x.reshape(t, d)

    cg, sg = _rope_tables(seq, HEAD_DIM)
    cm, sm = _rope_tables(seq, MLA_ROPE_DIM)
    perm_head = _deinterleave(HEAD_DIM)
    zeros32 = lambda rows: jnp.zeros((rows, LANES // 2 - MLA_ROPE_DIM // 2), BF16)

    c_pad = jnp.zeros((8, d), F32).at[:batch].set(c)

    for l in range(DEPTH):
        mod = _adaln(c_pad, w_ada[l], b_ada[l][None, :])[:batch].reshape(batch, 6, d)

        wl = w_in[l]
        o0 = 0
        wq = wl[:, o0:o0 + GQA_Q_COLS].reshape(d, GQA_Q_HEADS, HEAD_DIM)[:, :, perm_head]
        o0 += GQA_Q_COLS
        wk = wl[:, o0:o0 + GQA_KV_COLS].reshape(d, GQA_KV_HEADS, HEAD_DIM)[:, :, perm_head]
        o0 += GQA_KV_COLS
        wv = wl[:, o0:o0 + GQA_KV_COLS]
        o0 += GQA_KV_COLS
        wql = wl[:, o0:o0 + MLA_Q_RANK]
        o0 += MLA_Q_RANK
        wkvl = wl[:, o0:o0 + MLA_KV_RANK]
        o0 += MLA_KV_RANK
        wkr = wl[:, o0:o0 + MLA_ROPE_DIM].astype(BF16)
        o0 += MLA_ROPE_DIM
        w_gate = wl[:, o0:].astype(BF16)
        wkr_pad = jnp.concatenate(
            [wkr[:, 0::2], zeros32(d), wkr[:, 1::2], zeros32(d)], axis=-1)
        w_attn = jnp.concatenate([
            wq.reshape(d, GQA_Q_COLS).astype(BF16), wk.reshape(d, GQA_KV_COLS).astype(BF16),
            wv.astype(BF16), wql.astype(BF16), wkvl.astype(BF16), wkr_pad], axis=-1)

        uq = w_mla_uq[l].astype(BF16).reshape(MLA_Q_RANK, MLA_HEADS, MLA_NOPE_DIM + MLA_ROPE_DIM)
        uq_r = uq[:, :, MLA_NOPE_DIM:]
        z = jnp.zeros((MLA_Q_RANK, MLA_HEADS, LANES // 2 - MLA_ROPE_DIM // 2), BF16)
        wuq = jnp.concatenate(
            [uq[:, :, :MLA_NOPE_DIM], uq_r[:, :, 0::2], z, uq_r[:, :, 1::2], z],
            axis=-1).reshape(MLA_Q_RANK, MLA_HEADS * MLA_QK_PAD)
        ukv = w_mla_ukv[l].astype(BF16).reshape(MLA_KV_RANK, MLA_HEADS, MLA_NOPE_DIM + MLA_V_DIM)
        wuk = ukv[:, :, :MLA_NOPE_DIM].reshape(MLA_KV_RANK, MLA_HEADS * MLA_NOPE_DIM)
        wuv = ukv[:, :, MLA_NOPE_DIM:].reshape(MLA_KV_RANK, MLA_HEADS * MLA_V_DIM)

        gq = (gqa_q_gain[l][perm_head] * (LOG2E * HEAD_DIM ** -0.5))[None, :]
        gk = gqa_k_gain[l][perm_head][None, :]

        h1, qgt, kg, vgt, qmt, km, vmt = _inproj(
            x2, mod, cg, sg, cm, sm, w_attn, gq, gk, mla_q_gain[l][None, :],
            mla_kv_gain[l][None, :], wuq, wuk, wuv, batch, seq)

        y_gqa = _attention(
            qgt, kg.reshape(batch, seq, GQA_KV_COLS), vgt,
            kv_heads=GQA_KV_HEADS, dv=HEAD_DIM, tq=TQ_GQA)
        y_mla = _attention(
            qmt, km.reshape(batch, seq, MLA_HEADS * MLA_QK_PAD), vmt,
            kv_heads=MLA_HEADS, dv=MLA_V_DIM, tq=TQ_MLA)

        x2 = _merge(
            x2, mod, h1, y_gqa.reshape(t, GQA_Q_COLS), y_mla.reshape(t, MLA_HEADS * MLA_V_DIM),
            w_gate, b_gates[l][None, :], w_branch_gqa[l].astype(BF16),
            w_branch_mla[l].astype(BF16), w_out[l].astype(BF16),
            ln1_g[l][None, :], ln1_b[l][None, :], seq)

        x2 = _ffn(
            x2, mod, w_ffn_gate[l].astype(BF16), w_ffn_up[l].astype(BF16),
            w_ffn_down[l].astype(BF16), ln2_g[l][None, :], ln2_b[l][None, :], seq)

    return x2.reshape(batch, seq, d)
```

```python
import functools

import jax
import jax.numpy as jnp
import numpy as np
from jax import lax
from jax.experimental import pallas as pl
from jax.experimental.pallas import tpu as pltpu

D_MODEL = 2048
GRID_W = 64
ROPE_THETA = 10000.0
HEAD_DIM = 128
GQA_Q_HEADS = 8
GQA_KV_HEADS = 2
MLA_HEADS = 8
MLA_Q_RANK = 512
MLA_KV_RANK = 512
MLA_NOPE_DIM = 128
MLA_ROPE_DIM = 64
MLA_V_DIM = 128
D_FF = 5632
DEPTH = 1
DEEPNORM_ALPHA = (2.0 * DEPTH) ** 0.25
LN_EPS = 1e-5
RMS_EPS = 1e-6

GQA_Q_COLS = GQA_Q_HEADS * HEAD_DIM
GQA_KV_COLS = GQA_KV_HEADS * HEAD_DIM
MLA_QK_PAD = 256
LANES = 128
LOG2E = 1.4426950408889634

VMEM_LIMIT = 56 * 1024 * 1024

TM_IN = 256
TV_CHUNK = TM_IN
TK_ATT = 256
ATT_STEPS = 14
SUM_ROWS = 16
EXP_ROWS = 64
TQ_GQA = 1024
TQ_MLA = 4096
TM_MERGE = 512
TN_MERGE = 512
TM_FFN = 512
TF_FFN = 512

BF16 = jnp.bfloat16
F32 = jnp.float32


def _const_spec(shape):
    nd = len(shape)
    return pl.BlockSpec(shape, lambda *_: (0,) * nd, pipeline_mode=pl.Buffered(1))


def _ln(x):
    mu = jnp.mean(x, axis=-1, keepdims=True)
    xc = x - mu
    var = jnp.mean(xc * xc, axis=-1, keepdims=True)
    return xc * lax.rsqrt(var + LN_EPS)


def _rms(x, gain):
    ms = jnp.mean(x * x, axis=-1, keepdims=True)
    return x * lax.rsqrt(ms + RMS_EPS) * gain


def _rope(x, c, s):
    return x * c + pltpu.roll(x, LANES // 2, 1) * s


def _adaln_kernel(c_ref, w_ref, b_ref, o_ref):
    c = c_ref[...]
    act = (c * jax.nn.sigmoid(c)).astype(BF16)
    o_ref[...] = jnp.dot(act, w_ref[...].astype(BF16), preferred_element_type=F32) + b_ref[...]


def _adaln(c_pad, w_ada, b_ada):
    n = w_ada.shape[1]
    tn = 1024
    return pl.pallas_call(
        _adaln_kernel,
        grid=(n // tn,),
        in_specs=[
            pl.BlockSpec((8, D_MODEL), lambda j: (0, 0)),
            pl.BlockSpec((D_MODEL, tn), lambda j: (0, j)),
            pl.BlockSpec((1, tn), lambda j: (0, j)),
        ],
        out_specs=pl.BlockSpec((8, tn), lambda j: (0, j)),
        out_shape=jax.ShapeDtypeStruct((8, n), F32),
        compiler_params=pltpu.CompilerParams(
            dimension_semantics=("arbitrary",), vmem_limit_bytes=VMEM_LIMIT),
        name="adaln",
    )(c_pad, w_ada, b_ada)


def _inproj_kernel(x_ref, mod_ref, cg_ref, sg_ref, cm_ref, sm_ref, w_ref,
                   gq_ref, gk_ref, gql_ref, gkvl_ref, wuq_ref, wuk_ref, wuv_ref,
                   h_ref, qg_ref, kg_ref, vgt_ref, qm_ref, km_ref, vmt_ref, *, mla_scale):
    x = x_ref[...]
    shift = mod_ref[0, 0:1, :]
    scale = mod_ref[0, 1:2, :]
    h = (_ln(x) * (1.0 + scale) + shift).astype(BF16)
    h_ref[...] = h
    proj = jnp.dot(h, w_ref[...], preferred_element_type=F32)

    cg = cg_ref[...]
    sg = sg_ref[...]
    cm = cm_ref[...]
    sm = sm_ref[...]
    ones_rows = jnp.ones((SUM_ROWS, x.shape[0]), BF16)

    for hd in range(GQA_Q_HEADS):
        q = proj[:, hd * HEAD_DIM:(hd + 1) * HEAD_DIM]
        q = _rope(_rms(q, gq_ref[...]), cg, sg)
        qg_ref[0, hd, 0] = q.T.astype(BF16)
    off = GQA_Q_COLS
    for hd in range(GQA_KV_HEADS):
        k = proj[:, off + hd * HEAD_DIM: off + (hd + 1) * HEAD_DIM]
        k = _rope(_rms(k, gk_ref[...]), cg, sg)
        kg_ref[:, hd * HEAD_DIM:(hd + 1) * HEAD_DIM] = k.astype(BF16)
    off += GQA_KV_COLS
    for hd in range(GQA_KV_HEADS):
        v = proj[:, off + hd * HEAD_DIM: off + (hd + 1) * HEAD_DIM]
        vgt_ref[0, hd, 0, 0:HEAD_DIM, :] = v.T.astype(BF16)
        vgt_ref[0, hd, 0, HEAD_DIM:, :] = ones_rows
    off += GQA_KV_COLS

    q_lat = _rms(proj[:, off: off + MLA_Q_RANK], gql_ref[...]).astype(BF16)
    off += MLA_Q_RANK
    kv_lat = _rms(proj[:, off: off + MLA_KV_RANK], gkvl_ref[...]).astype(BF16)
    off += MLA_KV_RANK
    k_rope = _rope(proj[:, off: off + LANES], cm, sm).astype(BF16)

    q_m = jnp.dot(q_lat, wuq_ref[...], preferred_element_type=F32) * mla_scale
    k_n = jnp.dot(kv_lat, wuk_ref[...], preferred_element_type=F32)
    v_m = jnp.dot(kv_lat, wuv_ref[...], preferred_element_type=F32)
    for hd in range(MLA_HEADS):
        b0 = hd * MLA_QK_PAD
        qm_ref[0, hd, 0, 0:LANES, :] = q_m[:, b0: b0 + LANES].T.astype(BF16)
        qm_ref[0, hd, 0, LANES:, :] = _rope(
            q_m[:, b0 + LANES: b0 + 2 * LANES], cm, sm).T.astype(BF16)
        km_ref[:, b0: b0 + LANES] = k_n[:, hd * LANES:(hd + 1) * LANES].astype(BF16)
        km_ref[:, b0 + LANES: b0 + 2 * LANES] = k_rope
        vmt_ref[0, hd, 0, 0:MLA_V_DIM, :] = (
            v_m[:, hd * MLA_V_DIM:(hd + 1) * MLA_V_DIM].T.astype(BF16))
        vmt_ref[0, hd, 0, MLA_V_DIM:, :] = ones_rows


def _inproj(x2, mod, cg, sg, cm, sm, w_attn, gq, gk, gql, gkvl, wuq, wuk, wuv, batch, seq):
    t = x2.shape[0]
    tm = TM_IN
    nt = seq // tm
    n_attn = w_attn.shape[1]
    kern = functools.partial(
        _inproj_kernel, mla_scale=float(LOG2E * (MLA_NOPE_DIM + MLA_ROPE_DIM) ** -0.5))
    tok = lambda i: (i, 0)
    pos = lambda i: (i % nt, 0)
    in_specs = [
        pl.BlockSpec((tm, D_MODEL), tok),
        pl.BlockSpec((1, 6, D_MODEL), lambda i: (i // nt, 0, 0)),
        pl.BlockSpec((tm, LANES), pos),
        pl.BlockSpec((tm, LANES), pos),
        pl.BlockSpec((tm, LANES), pos),
        pl.BlockSpec((tm, LANES), pos),
        _const_spec((D_MODEL, n_attn)),
        _const_spec((1, HEAD_DIM)),
        _const_spec((1, HEAD_DIM)),
        _const_spec((1, MLA_Q_RANK)),
        _const_spec((1, MLA_KV_RANK)),
        _const_spec(wuq.shape),
        _const_spec(wuk.shape),
        _const_spec(wuv.shape),
    ]
    out_shape = [
        jax.ShapeDtypeStruct((t, D_MODEL), BF16),
        jax.ShapeDtypeStruct((batch, GQA_Q_HEADS, seq // tm, HEAD_DIM, tm), BF16),
        jax.ShapeDtypeStruct((t, GQA_KV_COLS), BF16),
        jax.ShapeDtypeStruct(
            (batch, GQA_KV_HEADS, seq // TV_CHUNK, HEAD_DIM + SUM_ROWS, TV_CHUNK), BF16),
        jax.ShapeDtypeStruct((batch, MLA_HEADS, seq // tm, MLA_QK_PAD, tm), BF16),
        jax.ShapeDtypeStruct((t, MLA_HEADS * MLA_QK_PAD), BF16),
        jax.ShapeDtypeStruct(
            (batch, MLA_HEADS, seq // TV_CHUNK, MLA_V_DIM + SUM_ROWS, TV_CHUNK), BF16),
    ]
    vt_map = lambda i: (i // nt, 0, i % nt, 0, 0)
    out_specs = [
        pl.BlockSpec((tm, D_MODEL), tok),
        pl.BlockSpec((1, GQA_Q_HEADS, 1, HEAD_DIM, tm), vt_map),
        pl.BlockSpec((tm, GQA_KV_COLS), tok),
        pl.BlockSpec((1, GQA_KV_HEADS, 1, HEAD_DIM + SUM_ROWS, TV_CHUNK), vt_map),
        pl.BlockSpec((1, MLA_HEADS, 1, MLA_QK_PAD, tm), vt_map),
        pl.BlockSpec((tm, MLA_HEADS * MLA_QK_PAD), tok),
        pl.BlockSpec((1, MLA_HEADS, 1, MLA_V_DIM + SUM_ROWS, TV_CHUNK), vt_map),
    ]
    return pl.pallas_call(
        kern,
        grid=(t // tm,),
        in_specs=in_specs,
        out_specs=out_specs,
        out_shape=out_shape,
        compiler_params=pltpu.CompilerParams(
            dimension_semantics=("arbitrary",), vmem_limit_bytes=VMEM_LIMIT),
        name="inproj",
    )(x2, mod, cg, sg, cm, sm, w_attn, gq, gk, gql, gkvl, wuq, wuk, wuv)


def _attn_kernel(qt_ref, k_ref, vt_ref, o_ref, st_scr, pb_scr, acc_scr,
                 *, groups, dq, dv, tq, tk, nk, steps):
    mq = groups * tq
    cb = qt_ref.shape[-1]
    nb = mq // cb
    cols = [slice(n * cb, (n + 1) * cb) for n in range(nb)]
    grp = [(n * cb) // tq for n in range(nb)]
    tok = [(n * cb) % tq for n in range(nb)]

    def scores(c, st_ref, n):
        kc = k_ref[0, pl.ds(pl.multiple_of(c * tk, tk), tk), :]
        st = jnp.dot(kc, qt_ref[0, grp[n], tok[n] // cb],
                     preferred_element_type=F32)
        st_ref[:, cols[n]] = st
        return jnp.max(st, axis=0, keepdims=True)

    def values(c, pb_ref, alpha, n, init=False):
        pv = jnp.dot(vt_ref[0, 0, c], pb_ref[:, cols[n]], preferred_element_type=F32)
        acc_scr[:, cols[n]] = pv if init else alpha * acc_scr[:, cols[n]] + pv

    def softmax(st_ref, pb_ref, cmax, m_prev, n):
        m_new = jnp.maximum(m_prev, cmax)
        alpha = jnp.exp2(m_prev - m_new)
        for r in range(tk // EXP_ROWS):
            rows = slice(r * EXP_ROWS, (r + 1) * EXP_ROWS)
            pb_ref[rows, cols[n]] = jnp.exp2((st_ref[rows, cols[n]] - m_new).astype(BF16))
        return m_new, alpha

    def step(c, cur, nxt, carry, first=False, second=False, last=False):
        cmax, m_prev, alpha_prev = carry
        cmax_next, m_new, alpha = list(cmax), [], []
        for n in range(nb):
            if not last:
                cmax_next[n] = scores(c + 1, st_scr.at[nxt], n)
        for n in range(nb):
            m_n, a_n = softmax(st_scr.at[cur], pb_scr.at[cur], cmax[n], m_prev[n], n)
            m_new.append(m_n)
            alpha.append(a_n)
            if not first:
                values(c - 1, pb_scr.at[nxt], alpha_prev[n], n, init=second)
        return cmax_next, m_new, alpha

    cmax0 = []
    for n in range(nb):
        cmax0.append(scores(0, st_scr.at[0], n))
    carry = (cmax0, [jnp.full((1, cb), -jnp.inf, F32)] * nb, [jnp.ones((1, cb), F32)] * nb)
    carry = step(0, 0, 1, carry, first=True)
    carry = step(1, 1, 0, carry, second=True)

    n_loop = (nk - 3) // steps

    def body(i, carry):
        c0 = steps * i + 2
        for s in range(steps):
            carry = step(c0 + s, s % 2, (1 + s) % 2, carry)
        return carry

    if n_loop == 1:
        carry = body(0, carry)
    elif n_loop > 1:
        carry = lax.fori_loop(0, n_loop, body, carry)
    for c in range(n_loop * steps + 2, nk - 1):
        carry = step(c, c % 2, (c + 1) % 2, carry)
    _, _, alpha = step(nk - 1, (nk - 1) % 2, nk % 2, carry, last=True)
    for n in range(nb):
        values(nk - 1, pb_scr.at[(nk - 1) % 2], alpha[n], n)
        out_t = acc_scr[0:dv, cols[n]] / acc_scr[dv:dv + 1, cols[n]]
        o_ref[0, tok[n]:tok[n] + cb, grp[n] * dv:(grp[n] + 1) * dv] = out_t.T.astype(o_ref.dtype)


def _attention(qt, k, vt, *, kv_heads, dv, tq):
    batch, q_heads, n_qc, dq, cb = qt.shape
    seq = n_qc * cb
    groups = q_heads // kv_heads
    tk = TK_ATT
    assert tk == TV_CHUNK and vt.shape[-2:] == (dv + SUM_ROWS, tk) and tq % cb == 0
    nk = seq // tk
    mq = groups * tq
    kern = functools.partial(_attn_kernel, groups=groups, dq=dq, dv=dv, tq=tq, tk=tk, nk=nk,
                             steps=ATT_STEPS)
    return pl.pallas_call(
        kern,
        grid=(batch, kv_heads, seq // tq),
        in_specs=[
            pl.BlockSpec((1, groups, tq // cb, dq, cb), lambda b, h, i: (b, h, i, 0, 0)),
            pl.BlockSpec((1, seq, dq), lambda b, h, i: (b, 0, h)),
            pl.BlockSpec((1, 1, nk, dv + SUM_ROWS, tk), lambda b, h, i: (b, h, 0, 0, 0)),
        ],
        out_specs=pl.BlockSpec((1, tq, groups * dv), lambda b, h, i: (b, i, h)),
        out_shape=jax.ShapeDtypeStruct((batch, seq, q_heads * dv), BF16),
        scratch_shapes=[
            pltpu.VMEM((2, tk, mq), F32),
            pltpu.VMEM((2, tk, mq), BF16),
            pltpu.VMEM((dv + SUM_ROWS, mq), F32),
        ],
        compiler_params=pltpu.CompilerParams(
            dimension_semantics=("arbitrary", "arbitrary", "arbitrary"),
            vmem_limit_bytes=VMEM_LIMIT),
        name=f"attn_g{groups}",
    )(qt, k, vt)


def _merge_kernel(x_ref, mod_ref, h_ref, yg_ref, ym_ref, wga_ref, wgb_ref, bga_ref, bgb_ref,
                  wbg_ref, wbm_ref, wo_ref, g_ref, b_ref, o_ref, h_scr, acc_scr):
    j = pl.program_id(1)

    @pl.when(j == 0)
    def _():
        h_scr[...] = h_ref[...]
        acc_scr[...] = jnp.zeros(acc_scr.shape, F32)

    h = h_scr[...]
    la = jnp.dot(h, wga_ref[...], preferred_element_type=F32) + bga_ref[...]
    lb = jnp.dot(h, wgb_ref[...], preferred_element_type=F32) + bgb_ref[...]
    a = jnp.dot(yg_ref[...], wbg_ref[...], preferred_element_type=F32)
    b = jnp.dot(ym_ref[...], wbm_ref[...], preferred_element_type=F32)
    merged = jax.nn.sigmoid(la) * a + jax.nn.sigmoid(lb) * b
    acc_scr[...] += jnp.dot(merged.astype(BF16), wo_ref[...], preferred_element_type=F32)

    @pl.when(j == pl.num_programs(1) - 1)
    def _():
        gate = mod_ref[0, 2:3, :]
        r = DEEPNORM_ALPHA * x_ref[...] + gate * acc_scr[...]
        o_ref[...] = _ln(r) * g_ref[...] + b_ref[...]


def _merge(x2, mod, h, yg, ym, w_gate, b_gate, wbg, wbm, wo, ln_g, ln_b, seq):
    t = x2.shape[0]
    tm, tn = TM_MERGE, TN_MERGE
    nt = seq // tm
    nj = D_MODEL // tn
    tok = lambda i, j: (i, 0)
    return pl.pallas_call(
        _merge_kernel,
        grid=(t // tm, nj),
        in_specs=[
            pl.BlockSpec((tm, D_MODEL), tok),
            pl.BlockSpec((1, 6, D_MODEL), lambda i, j: (i // nt, 0, 0)),
            pl.BlockSpec((tm, D_MODEL), tok),
            pl.BlockSpec((tm, yg.shape[1]), tok),
            pl.BlockSpec((tm, ym.shape[1]), tok),
            pl.BlockSpec((D_MODEL, tn), lambda i, j: (0, j)),
            pl.BlockSpec((D_MODEL, tn), lambda i, j: (0, j + nj)),
            pl.BlockSpec((1, tn), lambda i, j: (0, j)),
            pl.BlockSpec((1, tn), lambda i, j: (0, j + nj)),
            pl.BlockSpec((wbg.shape[0], tn), lambda i, j: (0, j)),
            pl.BlockSpec((wbm.shape[0], tn), lambda i, j: (0, j)),
            pl.BlockSpec((tn, D_MODEL), lambda i, j: (j, 0)),
            pl.BlockSpec((1, D_MODEL), lambda i, j: (0, 0)),
            pl.BlockSpec((1, D_MODEL), lambda i, j: (0, 0)),
        ],
        out_specs=pl.BlockSpec((tm, D_MODEL), tok),
        out_shape=jax.ShapeDtypeStruct((t, D_MODEL), F32),
        scratch_shapes=[pltpu.VMEM((tm, D_MODEL), BF16), pltpu.VMEM((tm, D_MODEL), F32)],
        compiler_params=pltpu.CompilerParams(
            dimension_semantics=("arbitrary", "arbitrary"), vmem_limit_bytes=VMEM_LIMIT),
        name="merge",
    )(x2, mod, h, yg, ym, w_gate, w_gate, b_gate, b_gate, wbg, wbm, wo, ln_g, ln_b)


def _ffn_kernel(x_ref, mod_ref, wg_ref, wu_ref, wd_ref, g_ref, b_ref, o_ref, h_scr, acc_scr):
    j = pl.program_id(1)

    @pl.when(j == 0)
    def _():
        shift = mod_ref[0, 3:4, :]
        scale = mod_ref[0, 4:5, :]
        h_scr[...] = (_ln(x_ref[...]) * (1.0 + scale) + shift).astype(BF16)
        acc_scr[...] = jnp.zeros(acc_scr.shape, F32)

    h = h_scr[...]
    gt = jnp.dot(h, wg_ref[...], preferred_element_type=F32)
    up = jnp.dot(h, wu_ref[...], preferred_element_type=F32)
    a = (gt * jax.nn.sigmoid(gt) * up).astype(BF16)
    acc_scr[...] += jnp.dot(a, wd_ref[...], preferred_element_type=F32)

    @pl.when(j == pl.num_programs(1) - 1)
    def _():
        gate = mod_ref[0, 5:6, :]
        r = DEEPNORM_ALPHA * x_ref[...] + gate * acc_scr[...]
        o_ref[...] = _ln(r) * g_ref[...] + b_ref[...]


def _ffn(x1, mod, wg, wu, wd, ln_g, ln_b, seq):
    t = x1.shape[0]
    tm, tf = TM_FFN, TF_FFN
    nt = seq // tm
    tok = lambda i, j: (i, 0)
    return pl.pallas_call(
        _ffn_kernel,
        grid=(t // tm, D_FF // tf),
        in_specs=[
            pl.BlockSpec((tm, D_MODEL), tok),
            pl.BlockSpec((1, 6, D_MODEL), lambda i, j: (i // nt, 0, 0)),
            pl.BlockSpec((D_MODEL, tf), lambda i, j: (0, j)),
            pl.BlockSpec((D_MODEL, tf), lambda i, j: (0, j)),
            pl.BlockSpec((tf, D_MODEL), lambda i, j: (j, 0)),
            pl.BlockSpec((1, D_MODEL), lambda i, j: (0, 0)),
            pl.BlockSpec((1, D_MODEL), lambda i, j: (0, 0)),
        ],
        out_specs=pl.BlockSpec((tm, D_MODEL), tok),
        out_shape=jax.ShapeDtypeStruct((t, D_MODEL), F32),
        scratch_shapes=[pltpu.VMEM((tm, D_MODEL), BF16), pltpu.VMEM((tm, D_MODEL), F32)],
        compiler_params=pltpu.CompilerParams(
            dimension_semantics=("arbitrary", "arbitrary"), vmem_limit_bytes=VMEM_LIMIT),
        name="ffn",
    )(x1, mod, wg, wu, wd, ln_g, ln_b)


def _rope_tables(seq, dim):
    rows = seq // GRID_W
    quarter = dim // 4
    inv_freq = ROPE_THETA ** (-jnp.arange(quarter, dtype=F32) / quarter)
    row_ang = jnp.arange(rows, dtype=F32)[:, None] * inv_freq
    col_ang = jnp.arange(GRID_W, dtype=F32)[:, None] * inv_freq

    def table(fn):
        return jnp.concatenate([
            jnp.broadcast_to(fn(row_ang)[:, None, :], (rows, GRID_W, quarter)),
            jnp.broadcast_to(fn(col_ang)[None, :, :], (rows, GRID_W, quarter)),
        ], axis=-1).reshape(seq, 2 * quarter)

    half = dim // 2
    pad = jnp.zeros((seq, LANES // 2 - half), F32)
    cos, sin = table(jnp.cos), table(jnp.sin)
    c = jnp.concatenate([cos, pad, cos, pad], axis=-1)
    s = jnp.concatenate([-sin, pad, sin, pad], axis=-1)
    return c, s


def _deinterleave(n):
    return np.concatenate([np.arange(0, n, 2), np.arange(1, n, 2)])


def kernel(x, c, w_ada, b_ada, w_in, b_gates, gqa_q_gain, gqa_k_gain, mla_q_gain, mla_kv_gain,
           w_mla_uq, w_mla_ukv, w_branch_gqa, w_branch_mla, w_out, ln1_g, ln1_b,
           w_ffn_gate, w_ffn_up, w_ffn_down, ln2_g, ln2_b):
    batch, seq, d = x.shape
    assert d == D_MODEL and w_ada.shape[0] == DEPTH
    t = batch * seq
    x2 = x.reshape(t, d)

    cg, sg = _rope_tables(seq, HEAD_DIM)
    cm, sm = _rope_tables(seq, MLA_ROPE_DIM)
    perm_head = _deinterleave(HEAD_DIM)
    zeros32 = lambda rows: jnp.zeros((rows, LANES // 2 - MLA_ROPE_DIM // 2), BF16)

    c_pad = jnp.zeros((8, d), F32).at[:batch].set(c)

    for l in range(DEPTH):
        mod = _adaln(c_pad, w_ada[l], b_ada[l][None, :])[:batch].reshape(batch, 6, d)

        wl = w_in[l]
        o0 = 0
        wq = wl[:, o0:o0 + GQA_Q_COLS].reshape(d, GQA_Q_HEADS, HEAD_DIM)[:, :, perm_head]
        o0 += GQA_Q_COLS
        wk = wl[:, o0:o0 + GQA_KV_COLS].reshape(d, GQA_KV_HEADS, HEAD_DIM)[:, :, perm_head]
        o0 += GQA_KV_COLS
        wv = wl[:, o0:o0 + GQA_KV_COLS]
        o0 += GQA_KV_COLS
        wql = wl[:, o0:o0 + MLA_Q_RANK]
        o0 += MLA_Q_RANK
        wkvl = wl[:, o0:o0 + MLA_KV_RANK]
        o0 += MLA_KV_RANK
        wkr = wl[:, o0:o0 + MLA_ROPE_DIM].astype(BF16)
        o0 += MLA_ROPE_DIM
        w_gate = wl[:, o0:].astype(BF16)
        wkr_pad = jnp.concatenate(
            [wkr[:, 0::2], zeros32(d), wkr[:, 1::2], zeros32(d)], axis=-1)
        w_attn = jnp.concatenate([
            wq.reshape(d, GQA_Q_COLS).astype(BF16), wk.reshape(d, GQA_KV_COLS).astype(BF16),
            wv.astype(BF16), wql.astype(BF16), wkvl.astype(BF16), wkr_pad], axis=-1)

        uq = w_mla_uq[l].astype(BF16).reshape(MLA_Q_RANK, MLA_HEADS, MLA_NOPE_DIM + MLA_ROPE_DIM)
        uq_r = uq[:, :, MLA_NOPE_DIM:]
        z = jnp.zeros((MLA_Q_RANK, MLA_HEADS, LANES // 2 - MLA_ROPE_DIM // 2), BF16)
        wuq = jnp.concatenate(
            [uq[:, :, :MLA_NOPE_DIM], uq_r[:, :, 0::2], z, uq_r[:, :, 1::2], z],
            axis=-1).reshape(MLA_Q_RANK, MLA_HEADS * MLA_QK_PAD)
        ukv = w_mla_ukv[l].astype(BF16).reshape(MLA_KV_RANK, MLA_HEADS, MLA_NOPE_DIM + MLA_V_DIM)
        wuk = ukv[:, :, :MLA_NOPE_DIM].reshape(MLA_KV_RANK, MLA_HEADS * MLA_NOPE_DIM)
        wuv = ukv[:, :, MLA_NOPE_DIM:].reshape(MLA_KV_RANK, MLA_HEADS * MLA_V_DIM)

        gq = (gqa_q_gain[l][perm_head] * (LOG2E * HEAD_DIM ** -0.5))[None, :]
        gk = gqa_k_gain[l][perm_head][None, :]

        h1, qgt, kg, vgt, qmt, km, vmt = _inproj(
            x2, mod, cg, sg, cm, sm, w_attn, gq, gk, mla_q_gain[l][None, :],
            mla_kv_gain[l][None, :], wuq, wuk, wuv, batch, seq)

        y_gqa = _attention(
            qgt, kg.reshape(batch, seq, GQA_KV_COLS), vgt,
            kv_heads=GQA_KV_HEADS, dv=HEAD_DIM, tq=TQ_GQA)
        y_mla = _attention(
            qmt, km.reshape(batch, seq, MLA_HEADS * MLA_QK_PAD), vmt,
            kv_heads=MLA_HEADS, dv=MLA_V_DIM, tq=TQ_MLA)

        x2 = _merge(
            x2, mod, h1, y_gqa.reshape(t, GQA_Q_COLS), y_mla.reshape(t, MLA_HEADS * MLA_V_DIM),
            w_gate, b_gates[l][None, :], w_branch_gqa[l].astype(BF16),
            w_branch_mla[l].astype(BF16), w_out[l].astype(BF16),
            ln1_g[l][None, :], ln1_b[l][None, :], seq)

        x2 = _ffn(
            x2, mod, w_ffn_gate[l].astype(BF16), w_ffn_up[l].astype(BF16),
            w_ffn_down[l].astype(BF16), ln2_g[l][None, :], ln2_b[l][None, :], seq)

    return x2.reshape(batch, seq, d)
```

```python
import functools

import jax
import jax.numpy as jnp
import numpy as np
from jax import lax
from jax.experimental import pallas as pl
from jax.experimental.pallas import tpu as pltpu

D_MODEL = 2048
GRID_W = 64
ROPE_THETA = 10000.0
HEAD_DIM = 128
GQA_Q_HEADS = 8
GQA_KV_HEADS = 2
MLA_HEADS = 8
MLA_Q_RANK = 512
MLA_KV_RANK = 512
MLA_NOPE_DIM = 128
MLA_ROPE_DIM = 64
MLA_V_DIM = 128
D_FF = 5632
DEPTH = 1
DEEPNORM_ALPHA = (2.0 * DEPTH) ** 0.25
LN_EPS = 1e-5
RMS_EPS = 1e-6

GQA_Q_COLS = GQA_Q_HEADS * HEAD_DIM
GQA_KV_COLS = GQA_KV_HEADS * HEAD_DIM
MLA_QK_PAD = 256
LANES = 128
LOG2E = 1.4426950408889634

VMEM_LIMIT = 56 * 1024 * 1024

TM_IN = 256
TV_CHUNK = TM_IN
TK_ATT = 256
ATT_STEPS = 14
SUM_ROWS = 16
EXP_ROWS = 64
TQ_GQA = 1024
TQ_MLA = 4096
TM_MERGE = 512
TN_MERGE = 512
TM_FFN = 512
TF_FFN = 512

BF16 = jnp.bfloat16
F32 = jnp.float32


def _const_spec(shape):
    nd = len(shape)
    return pl.BlockSpec(shape, lambda *_: (0,) * nd, pipeline_mode=pl.Buffered(1))


def _ln(x):
    mu = jnp.mean(x, axis=-1, keepdims=True)
    xc = x - mu
    var = jnp.mean(xc * xc, axis=-1, keepdims=True)
    return xc * lax.rsqrt(var + LN_EPS)


def _rms(x, gain):
    ms = jnp.mean(x * x, axis=-1, keepdims=True)
    return x * lax.rsqrt(ms + RMS_EPS) * gain


def _rope(x, c, s):
    return x * c + pltpu.roll(x, LANES // 2, 1) * s


def _adaln_kernel(c_ref, w_ref, b_ref, o_ref):
    c = c_ref[...]
    act = (c * jax.nn.sigmoid(c)).astype(BF16)
    o_ref[...] = jnp.dot(act, w_ref[...].astype(BF16), preferred_element_type=F32) + b_ref[...]


def _adaln(c_pad, w_ada, b_ada):
    n = w_ada.shape[1]
    tn = 1024
    return pl.pallas_call(
        _adaln_kernel,
        grid=(n // tn,),
        in_specs=[
            pl.BlockSpec((8, D_MODEL), lambda j: (0, 0)),
            pl.BlockSpec((D_MODEL, tn), lambda j: (0, j)),
            pl.BlockSpec((1, tn), lambda j: (0, j)),
        ],
        out_specs=pl.BlockSpec((8, tn), lambda j: (0, j)),
        out_shape=jax.ShapeDtypeStruct((8, n), F32),
        compiler_params=pltpu.CompilerParams(
            dimension_semantics=("arbitrary",), vmem_limit_bytes=VMEM_LIMIT),
        name="adaln",
    )(c_pad, w_ada, b_ada)


def _inproj_kernel(x_ref, mod_ref, cg_ref, sg_ref, cm_ref, sm_ref, w_ref,
                   gq_ref, gk_ref, gql_ref, gkvl_ref, wuq_ref, wuk_ref, wuv_ref,
                   h_ref, qg_ref, kg_ref, vgt_ref, qm_ref, km_ref, vmt_ref, *, mla_scale):
    x = x_ref[...]
    shift = mod_ref[0, 0:1, :]
    scale = mod_ref[0, 1:2, :]
    h = (_ln(x) * (1.0 + scale) + shift).astype(BF16)
    h_ref[...] = h
    proj = jnp.dot(h, w_ref[...], preferred_element_type=F32)

    cg = cg_ref[...]
    sg = sg_ref[...]
    cm = cm_ref[...]
    sm = sm_ref[...]
    ones_rows = jnp.ones((SUM_ROWS, x.shape[0]), BF16)

    for hd in range(GQA_Q_HEADS):
        q = proj[:, hd * HEAD_DIM:(hd + 1) * HEAD_DIM]
        q = _rope(_rms(q, gq_ref[...]), cg, sg)
        qg_ref[0, hd, 0] = q.T.astype(BF16)
    off = GQA_Q_COLS
    for hd in range(GQA_KV_HEADS):
        k = proj[:, off + hd * HEAD_DIM: off + (hd + 1) * HEAD_DIM]
        k = _rope(_rms(k, gk_ref[...]), cg, sg)
        kg_ref[:, hd * HEAD_DIM:(hd + 1) * HEAD_DIM] = k.astype(BF16)
    off += GQA_KV_COLS
    for hd in range(GQA_KV_HEADS):
        v = proj[:, off + hd * HEAD_DIM: off + (hd + 1) * HEAD_DIM]
        vgt_ref[0, hd, 0, 0:HEAD_DIM, :] = v.T.astype(BF16)
        vgt_ref[0, hd, 0, HEAD_DIM:, :] = ones_rows
    off += GQA_KV_COLS

    q_lat = _rms(proj[:, off: off + MLA_Q_RANK], gql_ref[...]).astype(BF16)
    off += MLA_Q_RANK
    kv_lat = _rms(proj[:, off: off + MLA_KV_RANK], gkvl_ref[...]).astype(BF16)
    off += MLA_KV_RANK
    k_rope = _rope(proj[:, off: off + LANES], cm, sm).astype(BF16)

    q_m = jnp.dot(q_lat, wuq_ref[...], preferred_element_type=F32) * mla_scale
    k_n = jnp.dot(kv_lat, wuk_ref[...], preferred_element_type=F32)
    v_m = jnp.dot(kv_lat, wuv_ref[...], preferred_element_type=F32)
    for hd in range(MLA_HEADS):
        b0 = hd * MLA_QK_PAD
        qm_ref[0, hd, 0, 0:LANES, :] = q_m[:, b0: b0 + LANES].T.astype(BF16)
        qm_ref[0, hd, 0, LANES:, :] = _rope(
            q_m[:, b0 + LANES: b0 + 2 * LANES], cm, sm).T.astype(BF16)
        km_ref[:, b0: b0 + LANES] = k_n[:, hd * LANES:(hd + 1) * LANES].astype(BF16)
        km_ref[:, b0 + LANES: b0 + 2 * LANES] = k_rope
        vmt_ref[0, hd, 0, 0:MLA_V_DIM, :] = (
            v_m[:, hd * MLA_V_DIM:(hd + 1) * MLA_V_DIM].T.astype(BF16))
        vmt_ref[0, hd, 0, MLA_V_DIM:, :] = ones_rows


def _inproj(x2, mod, cg, sg, cm, sm, w_attn, gq, gk, gql, gkvl, wuq, wuk, wuv, batch, seq):
    t = x2.shape[0]
    tm = TM_IN
    nt = seq // tm
    n_attn = w_attn.shape[1]
    kern = functools.partial(
        _inproj_kernel, mla_scale=float(LOG2E * (MLA_NOPE_DIM + MLA_ROPE_DIM) ** -0.5))
    tok = lambda i: (i, 0)
    pos = lambda i: (i % nt, 0)
    in_specs = [
        pl.BlockSpec((tm, D_MODEL), tok),
        pl.BlockSpec((1, 6, D_MODEL), lambda i: (i // nt, 0, 0)),
        pl.BlockSpec((tm, LANES), pos),
        pl.BlockSpec((tm, LANES), pos),
        pl.BlockSpec((tm, LANES), pos),
        pl.BlockSpec((tm, LANES), pos),
        _const_spec((D_MODEL, n_attn)),
        _const_spec((1, HEAD_DIM)),
        _const_spec((1, HEAD_DIM)),
        _const_spec((1, MLA_Q_RANK)),
        _const_spec((1, MLA_KV_RANK)),
        _const_spec(wuq.shape),
        _const_spec(wuk.shape),
        _const_spec(wuv.shape),
    ]
    out_shape = [
        jax.ShapeDtypeStruct((t, D_MODEL), BF16),
        jax.ShapeDtypeStruct((batch, GQA_Q_HEADS, seq // tm, HEAD_DIM, tm), BF16),
        jax.ShapeDtypeStruct((t, GQA_KV_COLS), BF16),
        jax.ShapeDtypeStruct(
            (batch, GQA_KV_HEADS, seq // TV_CHUNK, HEAD_DIM + SUM_ROWS, TV_CHUNK), BF16),
        jax.ShapeDtypeStruct((batch, MLA_HEADS, seq // tm, MLA_QK_PAD, tm), BF16),
        jax.ShapeDtypeStruct((t, MLA_HEADS * MLA_QK_PAD), BF16),
        jax.ShapeDtypeStruct(
            (batch, MLA_HEADS, seq // TV_CHUNK, MLA_V_DIM + SUM_ROWS, TV_CHUNK), BF16),
    ]
    vt_map = lambda i: (i // nt, 0, i % nt, 0, 0)
    out_specs = [
        pl.BlockSpec((tm, D_MODEL), tok),
        pl.BlockSpec((1, GQA_Q_HEADS, 1, HEAD_DIM, tm), vt_map),
        pl.BlockSpec((tm, GQA_KV_COLS), tok),
        pl.BlockSpec((1, GQA_KV_HEADS, 1, HEAD_DIM + SUM_ROWS, TV_CHUNK), vt_map),
        pl.BlockSpec((1, MLA_HEADS, 1, MLA_QK_PAD, tm), vt_map),
        pl.BlockSpec((tm, MLA_HEADS * MLA_QK_PAD), tok),
        pl.BlockSpec((1, MLA_HEADS, 1, MLA_V_DIM + SUM_ROWS, TV_CHUNK), vt_map),
    ]
    return pl.pallas_call(
        kern,
        grid=(t // tm,),
        in_specs=in_specs,
        out_specs=out_specs,
        out_shape=out_shape,
        compiler_params=pltpu.CompilerParams(
            dimension_semantics=("arbitrary",), vmem_limit_bytes=VMEM_LIMIT),
        name="inproj",
    )(x2, mod, cg, sg, cm, sm, w_attn, gq, gk, gql, gkvl, wuq, wuk, wuv)


def _attn_kernel(qt_ref, k_ref, vt_ref, o_ref, st_scr, pb_scr, acc_scr,
                 *, groups, dq, dv, tq, tk, nk, steps):
    mq = groups * tq
    cb = qt_ref.shape[-1]
    nb = mq // cb
    cols = [slice(n * cb, (n + 1) * cb) for n in range(nb)]
    grp = [(n * cb) // tq for n in range(nb)]
    tok = [(n * cb) % tq for n in range(nb)]

    def scores(c, st_ref, n):
        kc = k_ref[0, pl.ds(pl.multiple_of(c * tk, tk), tk), :]
        st = jnp.dot(kc, qt_ref[0, grp[n], tok[n] // cb],
                     preferred_element_type=F32)
        st_ref[:, cols[n]] = st
        return jnp.max(st, axis=0, keepdims=True)

    def values(c, pb_ref, alpha, n, init=False):
        pv = jnp.dot(vt_ref[0, 0, c], pb_ref[:, cols[n]], preferred_element_type=F32)
        acc_scr[:, cols[n]] = pv if init else alpha * acc_scr[:, cols[n]] + pv

    def softmax(st_ref, pb_ref, cmax, m_prev, n):
        m_new = jnp.maximum(m_prev, cmax)
        alpha = jnp.exp2(m_prev - m_new)
        for r in range(tk // EXP_ROWS):
            rows = slice(r * EXP_ROWS, (r + 1) * EXP_ROWS)
            pb_ref[rows, cols[n]] = jnp.exp2((st_ref[rows, cols[n]] - m_new).astype(BF16))
        return m_new, alpha

    def step(c, cur, nxt, carry, first=False, second=False, last=False):
        cmax, m_prev, alpha_prev = carry
        cmax_next, m_new, alpha = list(cmax), [], []
        for n in range(nb):
            if not last:
                cmax_next[n] = scores(c + 1, st_scr.at[nxt], n)
        for n in range(nb):
            m_n, a_n = softmax(st_scr.at[cur], pb_scr.at[cur], cmax[n], m_prev[n], n)
            m_new.append(m_n)
            alpha.append(a_n)
            if not first:
                values(c - 1, pb_scr.at[nxt], alpha_prev[n], n, init=second)
        return cmax_next, m_new, alpha

    cmax0 = []
    for n in range(nb):
        cmax0.append(scores(0, st_scr.at[0], n))
    carry = (cmax0, [jnp.full((1, cb), -jnp.inf, F32)] * nb, [jnp.ones((1, cb), F32)] * nb)
    carry = step(0, 0, 1, carry, first=True)
    carry = step(1, 1, 0, carry, second=True)

    n_loop = (nk - 3) // steps

    def body(i, carry):
        c0 = steps * i + 2
        for s in range(steps):
            carry = step(c0 + s, s % 2, (1 + s) % 2, carry)
        return carry

    if n_loop == 1:
        carry = body(0, carry)
    elif n_loop > 1:
        carry = lax.fori_loop(0, n_loop, body, carry)
    for c in range(n_loop * steps + 2, nk - 1):
        carry = step(c, c % 2, (c + 1) % 2, carry)
    _, _, alpha = step(nk - 1, (nk - 1) % 2, nk % 2, carry, last=True)
    for n in range(nb):
        values(nk - 1, pb_scr.at[(nk - 1) % 2], alpha[n], n)
        out_t = acc_scr[0:dv, cols[n]] / acc_scr[dv:dv + 1, cols[n]]
        o_ref[0, tok[n]:tok[n] + cb, grp[n] * dv:(grp[n] + 1) * dv] = out_t.T.astype(o_ref.dtype)


def _attention(qt, k, vt, *, kv_heads, dv, tq):
    batch, q_heads, n_qc, dq, cb = qt.shape
    seq = n_qc * cb
    groups = q_heads // kv_heads
    tk = TK_ATT
    assert tk == TV_CHUNK and vt.shape[-2:] == (dv + SUM_ROWS, tk) and tq % cb == 0
    nk = seq // tk
    mq = groups * tq
    kern = functools.partial(_attn_kernel, groups=groups, dq=dq, dv=dv, tq=tq, tk=tk, nk=nk,
                             steps=ATT_STEPS)
    return pl.pallas_call(
        kern,
        grid=(batch, kv_heads, seq // tq),
        in_specs=[
            pl.BlockSpec((1, groups, tq // cb, dq, cb), lambda b, h, i: (b, h, i, 0, 0)),
            pl.BlockSpec((1, seq, dq), lambda b, h, i: (b, 0, h)),
            pl.BlockSpec((1, 1, nk, dv + SUM_ROWS, tk), lambda b, h, i: (b, h, 0, 0, 0)),
        ],
        out_specs=pl.BlockSpec((1, tq, groups * dv), lambda b, h, i: (b, i, h)),
        out_shape=jax.ShapeDtypeStruct((batch, seq, q_heads * dv), BF16),
        scratch_shapes=[
            pltpu.VMEM((2, tk, mq), F32),
            pltpu.VMEM((2, tk, mq), BF16),
            pltpu.VMEM((dv + SUM_ROWS, mq), F32),
        ],
        compiler_params=pltpu.CompilerParams(
            dimension_semantics=("arbitrary", "arbitrary", "arbitrary"),
            vmem_limit_bytes=VMEM_LIMIT),
        name=f"attn_g{groups}",
    )(qt, k, vt)


def _merge_kernel(x_ref, mod_ref, h_ref, yg_ref, ym_ref, wga_ref, wgb_ref, bga_ref, bgb_ref,
                  wbg_ref, wbm_ref, wo_ref, g_ref, b_ref, o_ref, h_scr, acc_scr):
    j = pl.program_id(1)
    last = pl.num_programs(1) - 1

    def chunk(h):
        la = jnp.dot(h, wga_ref[...], preferred_element_type=F32) + bga_ref[...]
        lb = jnp.dot(h, wgb_ref[...], preferred_element_type=F32) + bgb_ref[...]
        a = jnp.dot(yg_ref[...], wbg_ref[...], preferred_element_type=F32)
        b = jnp.dot(ym_ref[...], wbm_ref[...], preferred_element_type=F32)
        merged = jax.nn.sigmoid(la) * a + jax.nn.sigmoid(lb) * b
        return jnp.dot(merged.astype(BF16), wo_ref[...], preferred_element_type=F32)

    @pl.when(j == 0)
    def _():
        h = h_ref[...]
        h_scr[...] = h
        acc_scr[...] = chunk(h)

    @pl.when((j > 0) & (j < last))
    def _():
        acc_scr[...] += chunk(h_scr[...])

    @pl.when(j == last)
    def _():
        acc = acc_scr[...] + chunk(h_scr[...])
        r = DEEPNORM_ALPHA * x_ref[...] + mod_ref[0, 2:3, :] * acc
        o_ref[...] = _ln(r) * g_ref[...] + b_ref[...]


def _merge(x2, mod, h, yg, ym, w_gate, b_gate, wbg, wbm, wo, ln_g, ln_b, seq):
    t = x2.shape[0]
    tm, tn = TM_MERGE, TN_MERGE
    nt = seq // tm
    nj = D_MODEL // tn
    tok = lambda i, j: (i, 0)
    return pl.pallas_call(
        _merge_kernel,
        grid=(t // tm, nj),
        in_specs=[
            pl.BlockSpec((tm, D_MODEL), tok),
            pl.BlockSpec((1, 6, D_MODEL), lambda i, j: (i // nt, 0, 0)),
            pl.BlockSpec((tm, D_MODEL), tok),
            pl.BlockSpec((tm, yg.shape[1]), tok),
            pl.BlockSpec((tm, ym.shape[1]), tok),
            pl.BlockSpec((D_MODEL, tn), lambda i, j: (0, j)),
            pl.BlockSpec((D_MODEL, tn), lambda i, j: (0, j + nj)),
            pl.BlockSpec((1, tn), lambda i, j: (0, j)),
            pl.BlockSpec((1, tn), lambda i, j: (0, j + nj)),
            pl.BlockSpec((wbg.shape[0], tn), lambda i, j: (0, j)),
            pl.BlockSpec((wbm.shape[0], tn), lambda i, j: (0, j)),
            pl.BlockSpec((tn, D_MODEL), lambda i, j: (j, 0)),
            pl.BlockSpec((1, D_MODEL), lambda i, j: (0, 0)),
            pl.BlockSpec((1, D_MODEL), lambda i, j: (0, 0)),
        ],
        out_specs=pl.BlockSpec((tm, D_MODEL), tok),
        out_shape=jax.ShapeDtypeStruct((t, D_MODEL), F32),
        scratch_shapes=[pltpu.VMEM((tm, D_MODEL), BF16), pltpu.VMEM((tm, D_MODEL), F32)],
        compiler_params=pltpu.CompilerParams(
            dimension_semantics=("arbitrary", "arbitrary"), vmem_limit_bytes=VMEM_LIMIT),
        name="merge",
    )(x2, mod, h, yg, ym, w_gate, w_gate, b_gate, b_gate, wbg, wbm, wo, ln_g, ln_b)


def _ffn_kernel(x_ref, mod_ref, wg_ref, wu_ref, wd_ref, g_ref, b_ref, o_ref, h_scr, acc_scr):
    j = pl.program_id(1)
    last = pl.num_programs(1) - 1

    def chunk(h):
        gt = jnp.dot(h, wg_ref[...], preferred_element_type=F32)
        up = jnp.dot(h, wu_ref[...], preferred_element_type=F32)
        a = (gt * jax.nn.sigmoid(gt) * up).astype(BF16)
        return jnp.dot(a, wd_ref[...], preferred_element_type=F32)

    @pl.when(j == 0)
    def _():
        shift = mod_ref[0, 3:4, :]
        scale = mod_ref[0, 4:5, :]
        h = (_ln(x_ref[...]) * (1.0 + scale) + shift).astype(BF16)
        h_scr[...] = h
        acc_scr[...] = chunk(h)

    @pl.when((j > 0) & (j < last))
    def _():
        acc_scr[...] += chunk(h_scr[...])

    @pl.when(j == last)
    def _():
        acc = acc_scr[...] + chunk(h_scr[...])
        r = DEEPNORM_ALPHA * x_ref[...] + mod_ref[0, 5:6, :] * acc
        o_ref[...] = _ln(r) * g_ref[...] + b_ref[...]


def _ffn(x1, mod, wg, wu, wd, ln_g, ln_b, seq):
    t = x1.shape[0]
    tm, tf = TM_FFN, TF_FFN
    nt = seq // tm
    tok = lambda i, j: (i, 0)
    return pl.pallas_call(
        _ffn_kernel,
        grid=(t // tm, D_FF // tf),
        in_specs=[
            pl.BlockSpec((tm, D_MODEL), tok),
            pl.BlockSpec((1, 6, D_MODEL), lambda i, j: (i // nt, 0, 0)),
            pl.BlockSpec((D_MODEL, tf), lambda i, j: (0, j)),
            pl.BlockSpec((D_MODEL, tf), lambda i, j: (0, j)),
            pl.BlockSpec((tf, D_MODEL), lambda i, j: (j, 0)),
            pl.BlockSpec((1, D_MODEL), lambda i, j: (0, 0)),
            pl.BlockSpec((1, D_MODEL), lambda i, j: (0, 0)),
        ],
        out_specs=pl.BlockSpec((tm, D_MODEL), tok),
        out_shape=jax.ShapeDtypeStruct((t, D_MODEL), F32),
        scratch_shapes=[pltpu.VMEM((tm, D_MODEL), BF16), pltpu.VMEM((tm, D_MODEL), F32)],
        compiler_params=pltpu.CompilerParams(
            dimension_semantics=("arbitrary", "arbitrary"), vmem_limit_bytes=VMEM_LIMIT),
        name="ffn",
    )(x1, mod, wg, wu, wd, ln_g, ln_b)


def _rope_tables(seq, dim):
    rows = seq // GRID_W
    quarter = dim // 4
    inv_freq = ROPE_THETA ** (-jnp.arange(quarter, dtype=F32) / quarter)
    row_ang = jnp.arange(rows, dtype=F32)[:, None] * inv_freq
    col_ang = jnp.arange(GRID_W, dtype=F32)[:, None] * inv_freq

    def table(fn):
        return jnp.concatenate([
            jnp.broadcast_to(fn(row_ang)[:, None, :], (rows, GRID_W, quarter)),
            jnp.broadcast_to(fn(col_ang)[None, :, :], (rows, GRID_W, quarter)),
        ], axis=-1).reshape(seq, 2 * quarter)

    half = dim // 2
    pad = jnp.zeros((seq, LANES // 2 - half), F32)
    cos, sin = table(jnp.cos), table(jnp.sin)
    c = jnp.concatenate([cos, pad, cos, pad], axis=-1)
    s = jnp.concatenate([-sin, pad, sin, pad], axis=-1)
    return c, s


def _deinterleave(n):
    return np.concatenate([np.arange(0, n, 2), np.arange(1, n, 2)])


def kernel(x, c, w_ada, b_ada, w_in, b_gates, gqa_q_gain, gqa_k_gain, mla_q_gain, mla_kv_gain,
           w_mla_uq, w_mla_ukv, w_branch_gqa, w_branch_mla, w_out, ln1_g, ln1_b,
           w_ffn_gate, w_ffn_up, w_ffn_down, ln2_g, ln2_b):
    batch, seq, d = x.shape
    assert d == D_MODEL and w_ada.shape[0] == DEPTH
    t = batch * seq
    x2 = x.reshape(t, d)

    cg, sg = _rope_tables(seq, HEAD_DIM)
    cm, sm = _rope_tables(seq, MLA_ROPE_DIM)
    perm_head = _deinterleave(HEAD_DIM)
    zeros32 = lambda rows: jnp.zeros((rows, LANES // 2 - MLA_ROPE_DIM // 2), BF16)

    c_pad = jnp.zeros((8, d), F32).at[:batch].set(c)

    for l in range(DEPTH):
        mod = _adaln(c_pad, w_ada[l], b_ada[l][None, :])[:batch].reshape(batch, 6, d)

        wl = w_in[l]
        o0 = 0
        wq = wl[:, o0:o0 + GQA_Q_COLS].reshape(d, GQA_Q_HEADS, HEAD_DIM)[:, :, perm_head]
        o0 += GQA_Q_COLS
        wk = wl[:, o0:o0 + GQA_KV_COLS].reshape(d, GQA_KV_HEADS, HEAD_DIM)[:, :, perm_head]
        o0 += GQA_KV_COLS
        wv = wl[:, o0:o0 + GQA_KV_COLS]
        o0 += GQA_KV_COLS
        wql = wl[:, o0:o0 + MLA_Q_RANK]
        o0 += MLA_Q_RANK
        wkvl = wl[:, o0:o0 + MLA_KV_RANK]
        o0 += MLA_KV_RANK
        wkr = wl[:, o0:o0 + MLA_ROPE_DIM].astype(BF16)
        o0 += MLA_ROPE_DIM
        w_gate = wl[:, o0:].astype(BF16)
        wkr_pad = jnp.concatenate(
            [wkr[:, 0::2], zeros32(d), wkr[:, 1::2], zeros32(d)], axis=-1)
        w_attn = jnp.concatenate([
            wq.reshape(d, GQA_Q_COLS).astype(BF16), wk.reshape(d, GQA_KV_COLS).astype(BF16),
            wv.astype(BF16), wql.astype(BF16), wkvl.astype(BF16), wkr_pad], axis=-1)

        uq = w_mla_uq[l].astype(BF16).reshape(MLA_Q_RANK, MLA_HEADS, MLA_NOPE_DIM + MLA_ROPE_DIM)
        uq_r = uq[:, :, MLA_NOPE_DIM:]
        z = jnp.zeros((MLA_Q_RANK, MLA_HEADS, LANES // 2 - MLA_ROPE_DIM // 2), BF16)
        wuq = jnp.concatenate(
            [uq[:, :, :MLA_NOPE_DIM], uq_r[:, :, 0::2], z, uq_r[:, :, 1::2], z],
            axis=-1).reshape(MLA_Q_RANK, MLA_HEADS * MLA_QK_PAD)
        ukv = w_mla_ukv[l].astype(BF16).reshape(MLA_KV_RANK, MLA_HEADS, MLA_NOPE_DIM + MLA_V_DIM)
        wuk = ukv[:, :, :MLA_NOPE_DIM].reshape(MLA_KV_RANK, MLA_HEADS * MLA_NOPE_DIM)
        wuv = ukv[:, :, MLA_NOPE_DIM:].reshape(MLA_KV_RANK, MLA_HEADS * MLA_V_DIM)

        gq = (gqa_q_gain[l][perm_head] * (LOG2E * HEAD_DIM ** -0.5))[None, :]
        gk = gqa_k_gain[l][perm_head][None, :]

        h1, qgt, kg, vgt, qmt, km, vmt = _inproj(
            x2, mod, cg, sg, cm, sm, w_attn, gq, gk, mla_q_gain[l][None, :],
            mla_kv_gain[l][None, :], wuq, wuk, wuv, batch, seq)

        y_gqa = _attention(
            qgt, kg.reshape(batch, seq, GQA_KV_COLS), vgt,
            kv_heads=GQA_KV_HEADS, dv=HEAD_DIM, tq=TQ_GQA)
        y_mla = _attention(
            qmt, km.reshape(batch, seq, MLA_HEADS * MLA_QK_PAD), vmt,
            kv_heads=MLA_HEADS, dv=MLA_V_DIM, tq=TQ_MLA)

        x2 = _merge(
            x2, mod, h1, y_gqa.reshape(t, GQA_Q_COLS), y_mla.reshape(t, MLA_HEADS * MLA_V_DIM),
            w_gate, b_gates[l][None, :], w_branch_gqa[l].astype(BF16),
            w_branch_mla[l].astype(BF16), w_out[l].astype(BF16),
            ln1_g[l][None, :], ln1_b[l][None, :], seq)

        x2 = _ffn(
            x2, mod, w_ffn_gate[l].astype(BF16), w_ffn_up[l].astype(BF16),
            w_ffn_down[l].astype(BF16), ln2_g[l][None, :], ln2_b[l][None, :], seq)

    return x2.reshape(batch, seq, d)
```

```python
import functools

import jax
import jax.numpy as jnp
import numpy as np
from jax import lax
from jax.experimental import pallas as pl
from jax.experimental.pallas import tpu as pltpu

D_MODEL = 2048
GRID_W = 64
ROPE_THETA = 10000.0
HEAD_DIM = 128
GQA_Q_HEADS = 8
GQA_KV_HEADS = 2
MLA_HEADS = 8
MLA_Q_RANK = 512
MLA_KV_RANK = 512
MLA_NOPE_DIM = 128
MLA_ROPE_DIM = 64
MLA_V_DIM = 128
D_FF = 5632
DEPTH = 1
DEEPNORM_ALPHA = (2.0 * DEPTH) ** 0.25
LN_EPS = 1e-5
RMS_EPS = 1e-6

GQA_Q_COLS = GQA_Q_HEADS * HEAD_DIM
GQA_KV_COLS = GQA_KV_HEADS * HEAD_DIM
MLA_QK_PAD = 256
LANES = 128
LOG2E = 1.4426950408889634

VMEM_LIMIT = 56 * 1024 * 1024

TM_IN = 256
TV_CHUNK = TM_IN
TK_ATT = 256
ATT_STEPS = 14
SUM_ROWS = 16
EXP_ROWS = 64
TQ_GQA = 1024
TQ_MLA = 4096
TM_MERGE = 512
TN_MERGE = 512
TM_FFN = 512
TF_FFN = 512

BF16 = jnp.bfloat16
F32 = jnp.float32


def _const_spec(shape):
    nd = len(shape)
    return pl.BlockSpec(shape, lambda *_: (0,) * nd, pipeline_mode=pl.Buffered(1))


def _ln(x):
    mu = jnp.mean(x, axis=-1, keepdims=True)
    xc = x - mu
    var = jnp.mean(xc * xc, axis=-1, keepdims=True)
    return xc * lax.rsqrt(var + LN_EPS)


def _rms(x, gain):
    ms = jnp.mean(x * x, axis=-1, keepdims=True)
    return x * lax.rsqrt(ms + RMS_EPS) * gain


def _rope(x, c, s):
    return x * c + pltpu.roll(x, LANES // 2, 1) * s


def _adaln_kernel(c_ref, w_ref, b_ref, o_ref):
    c = c_ref[...]
    act = (c * jax.nn.sigmoid(c)).astype(BF16)
    o_ref[...] = jnp.dot(act, w_ref[...].astype(BF16), preferred_element_type=F32) + b_ref[...]


def _adaln(c_pad, w_ada, b_ada):
    n = w_ada.shape[1]
    tn = 1024
    return pl.pallas_call(
        _adaln_kernel,
        grid=(n // tn,),
        in_specs=[
            pl.BlockSpec((8, D_MODEL), lambda j: (0, 0)),
            pl.BlockSpec((D_MODEL, tn), lambda j: (0, j)),
            pl.BlockSpec((1, tn), lambda j: (0, j)),
        ],
        out_specs=pl.BlockSpec((8, tn), lambda j: (0, j)),
        out_shape=jax.ShapeDtypeStruct((8, n), F32),
        compiler_params=pltpu.CompilerParams(
            dimension_semantics=("arbitrary",), vmem_limit_bytes=VMEM_LIMIT),
        name="adaln",
    )(c_pad, w_ada, b_ada)


def _inproj_kernel(x_ref, mod_ref, cg_ref, sg_ref, cm_ref, sm_ref, w_ref,
                   gq_ref, gk_ref, gql_ref, gkvl_ref, wuq_ref, wuk_ref, wuv_ref,
                   h_ref, qg_ref, kg_ref, vgt_ref, qm_ref, km_ref, vmt_ref, *, mla_scale):
    x = x_ref[...]
    shift = mod_ref[0, 0:1, :]
    scale = mod_ref[0, 1:2, :]
    h = (_ln(x) * (1.0 + scale) + shift).astype(BF16)
    h_ref[...] = h
    proj = jnp.dot(h, w_ref[...], preferred_element_type=F32)

    cg = cg_ref[...]
    sg = sg_ref[...]
    cm = cm_ref[...]
    sm = sm_ref[...]
    ones_rows = jnp.ones((SUM_ROWS, x.shape[0]), BF16)

    for hd in range(GQA_Q_HEADS):
        q = proj[:, hd * HEAD_DIM:(hd + 1) * HEAD_DIM]
        q = _rope(_rms(q, gq_ref[...]), cg, sg)
        qg_ref[0, hd, 0] = q.T.astype(BF16)
    off = GQA_Q_COLS
    for hd in range(GQA_KV_HEADS):
        k = proj[:, off + hd * HEAD_DIM: off + (hd + 1) * HEAD_DIM]
        k = _rope(_rms(k, gk_ref[...]), cg, sg)
        kg_ref[:, hd * HEAD_DIM:(hd + 1) * HEAD_DIM] = k.astype(BF16)
    off += GQA_KV_COLS
    for hd in range(GQA_KV_HEADS):
        v = proj[:, off + hd * HEAD_DIM: off + (hd + 1) * HEAD_DIM]
        vgt_ref[0, hd, 0, 0:HEAD_DIM, :] = v.T.astype(BF16)
        vgt_ref[0, hd, 0, HEAD_DIM:, :] = ones_rows
    off += GQA_KV_COLS

    q_lat = _rms(proj[:, off: off + MLA_Q_RANK], gql_ref[...]).astype(BF16)
    off += MLA_Q_RANK
    kv_lat = _rms(proj[:, off: off + MLA_KV_RANK], gkvl_ref[...]).astype(BF16)
    off += MLA_KV_RANK
    k_rope = _rope(proj[:, off: off + LANES], cm, sm).astype(BF16)

    q_m = jnp.dot(q_lat, wuq_ref[...], preferred_element_type=F32) * mla_scale
    k_n = jnp.dot(kv_lat, wuk_ref[...], preferred_element_type=F32)
    v_m = jnp.dot(kv_lat, wuv_ref[...], preferred_element_type=F32)
    for hd in range(MLA_HEADS):
        b0 = hd * MLA_QK_PAD
        qm_ref[0, hd, 0, 0:LANES, :] = q_m[:, b0: b0 + LANES].T.astype(BF16)
        qm_ref[0, hd, 0, LANES:, :] = _rope(
            q_m[:, b0 + LANES: b0 + 2 * LANES], cm, sm).T.astype(BF16)
        km_ref[:, b0: b0 + LANES] = k_n[:, hd * LANES:(hd + 1) * LANES].astype(BF16)
        km_ref[:, b0 + LANES: b0 + 2 * LANES] = k_rope
        vmt_ref[0, hd, 0, 0:MLA_V_DIM, :] = (
            v_m[:, hd * MLA_V_DIM:(hd + 1) * MLA_V_DIM].T.astype(BF16))
        vmt_ref[0, hd, 0, MLA_V_DIM:, :] = ones_rows


def _inproj(x2, mod, cg, sg, cm, sm, w_attn, gq, gk, gql, gkvl, wuq, wuk, wuv, batch, seq):
    t = x2.shape[0]
    tm = TM_IN
    nt = seq // tm
    n_attn = w_attn.shape[1]
    kern = functools.partial(
        _inproj_kernel, mla_scale=float(LOG2E * (MLA_NOPE_DIM + MLA_ROPE_DIM) ** -0.5))
    tok = lambda i: (i, 0)
    pos = lambda i: (i % nt, 0)
    in_specs = [
        pl.BlockSpec((tm, D_MODEL), tok),
        pl.BlockSpec((1, 6, D_MODEL), lambda i: (i // nt, 0, 0)),
        pl.BlockSpec((tm, LANES), pos),
        pl.BlockSpec((tm, LANES), pos),
        pl.BlockSpec((tm, LANES), pos),
        pl.BlockSpec((tm, LANES), pos),
        _const_spec((D_MODEL, n_attn)),
        _const_spec((1, HEAD_DIM)),
        _const_spec((1, HEAD_DIM)),
        _const_spec((1, MLA_Q_RANK)),
        _const_spec((1, MLA_KV_RANK)),
        _const_spec(wuq.shape),
        _const_spec(wuk.shape),
        _const_spec(wuv.shape),
    ]
    out_shape = [
        jax.ShapeDtypeStruct((t, D_MODEL), BF16),
        jax.ShapeDtypeStruct((batch, GQA_Q_HEADS, seq // tm, HEAD_DIM, tm), BF16),
        jax.ShapeDtypeStruct((t, GQA_KV_COLS), BF16),
        jax.ShapeDtypeStruct(
            (batch, GQA_KV_HEADS, seq // TV_CHUNK, HEAD_DIM + SUM_ROWS, TV_CHUNK), BF16),
        jax.ShapeDtypeStruct((batch, MLA_HEADS, seq // tm, MLA_QK_PAD, tm), BF16),
        jax.ShapeDtypeStruct((t, MLA_HEADS * MLA_QK_PAD), BF16),
        jax.ShapeDtypeStruct(
            (batch, MLA_HEADS, seq // TV_CHUNK, MLA_V_DIM + SUM_ROWS, TV_CHUNK), BF16),
    ]
    vt_map = lambda i: (i // nt, 0, i % nt, 0, 0)
    out_specs = [
        pl.BlockSpec((tm, D_MODEL), tok),
        pl.BlockSpec((1, GQA_Q_HEADS, 1, HEAD_DIM, tm), vt_map),
        pl.BlockSpec((tm, GQA_KV_COLS), tok),
        pl.BlockSpec((1, GQA_KV_HEADS, 1, HEAD_DIM + SUM_ROWS, TV_CHUNK), vt_map),
        pl.BlockSpec((1, MLA_HEADS, 1, MLA_QK_PAD, tm), vt_map),
        pl.BlockSpec((tm, MLA_HEADS * MLA_QK_PAD), tok),
        pl.BlockSpec((1, MLA_HEADS, 1, MLA_V_DIM + SUM_ROWS, TV_CHUNK), vt_map),
    ]
    return pl.pallas_call(
        kern,
        grid=(t // tm,),
        in_specs=in_specs,
        out_specs=out_specs,
        out_shape=out_shape,
        compiler_params=pltpu.CompilerParams(
            dimension_semantics=("arbitrary",), vmem_limit_bytes=VMEM_LIMIT),
        name="inproj",
    )(x2, mod, cg, sg, cm, sm, w_attn, gq, gk, gql, gkvl, wuq, wuk, wuv)


def _attn_kernel(qt_ref, k_ref, vt_ref, o_ref, st_scr, pb_scr, acc_scr,
                 *, groups, dq, dv, tq, tk, nk, steps):
    mq = groups * tq
    cb = qt_ref.shape[-1]
    nb = mq // cb
    cols = [slice(n * cb, (n + 1) * cb) for n in range(nb)]
    grp = [(n * cb) // tq for n in range(nb)]
    tok = [(n * cb) % tq for n in range(nb)]

    def scores(c, st_ref, n):
        kc = k_ref[0, pl.ds(pl.multiple_of(c * tk, tk), tk), :]
        st = jnp.dot(kc, qt_ref[0, grp[n], tok[n] // cb],
                     preferred_element_type=F32)
        st_ref[:, cols[n]] = st
        return jnp.max(st, axis=0, keepdims=True)

    def values(c, pb_ref, alpha, n, init=False):
        pv = jnp.dot(vt_ref[0, 0, c], pb_ref[:, cols[n]], preferred_element_type=F32)
        acc_scr[:, cols[n]] = pv if init else alpha * acc_scr[:, cols[n]] + pv

    def softmax(st_ref, pb_ref, cmax, m_prev, n):
        m_new = jnp.maximum(m_prev, cmax)
        alpha = jnp.exp2(m_prev - m_new)
        for r in range(tk // EXP_ROWS):
            rows = slice(r * EXP_ROWS, (r + 1) * EXP_ROWS)
            pb_ref[rows, cols[n]] = jnp.exp2((st_ref[rows, cols[n]] - m_new).astype(BF16))
        return m_new, alpha

    def step(c, cur, nxt, carry, first=False, second=False, last=False):
        cmax, m_prev, alpha_prev = carry
        cmax_next, m_new, alpha = list(cmax), [], []
        for n in range(nb):
            if not last:
                cmax_next[n] = scores(c + 1, st_scr.at[nxt], n)
        for n in range(nb):
            m_n, a_n = softmax(st_scr.at[cur], pb_scr.at[cur], cmax[n], m_prev[n], n)
            m_new.append(m_n)
            alpha.append(a_n)
            if not first:
                values(c - 1, pb_scr.at[nxt], alpha_prev[n], n, init=second)
        return cmax_next, m_new, alpha

    cmax0 = []
    for n in range(nb):
        cmax0.append(scores(0, st_scr.at[0], n))
    carry = (cmax0, [jnp.full((1, cb), -jnp.inf, F32)] * nb, [jnp.ones((1, cb), F32)] * nb)
    carry = step(0, 0, 1, carry, first=True)
    carry = step(1, 1, 0, carry, second=True)

    n_loop = (nk - 3) // steps

    def body(i, carry):
        c0 = steps * i + 2
        for s in range(steps):
            carry = step(c0 + s, s % 2, (1 + s) % 2, carry)
        return carry

    if n_loop == 1:
        carry = body(0, carry)
    elif n_loop > 1:
        carry = lax.fori_loop(0, n_loop, body, carry)
    for c in range(n_loop * steps + 2, nk - 1):
        carry = step(c, c % 2, (c + 1) % 2, carry)
    _, _, alpha = step(nk - 1, (nk - 1) % 2, nk % 2, carry, last=True)
    for n in range(nb):
        values(nk - 1, pb_scr.at[(nk - 1) % 2], alpha[n], n)
        out_t = acc_scr[0:dv, cols[n]] / acc_scr[dv:dv + 1, cols[n]]
        o_ref[0, tok[n]:tok[n] + cb, grp[n] * dv:(grp[n] + 1) * dv] = out_t.T.astype(o_ref.dtype)


def _attention(qt, k, vt, *, kv_heads, dv, tq):
    batch, q_heads, n_qc, dq, cb = qt.shape
    seq = n_qc * cb
    groups = q_heads // kv_heads
    tk = TK_ATT
    assert tk == TV_CHUNK and vt.shape[-2:] == (dv + SUM_ROWS, tk) and tq % cb == 0
    nk = seq // tk
    mq = groups * tq
    kern = functools.partial(_attn_kernel, groups=groups, dq=dq, dv=dv, tq=tq, tk=tk, nk=nk,
                             steps=ATT_STEPS)
    return pl.pallas_call(
        kern,
        grid=(batch, kv_heads, seq // tq),
        in_specs=[
            pl.BlockSpec((1, groups, tq // cb, dq, cb), lambda b, h, i: (b, h, i, 0, 0)),
            pl.BlockSpec((1, seq, dq), lambda b, h, i: (b, 0, h)),
            pl.BlockSpec((1, 1, nk, dv + SUM_ROWS, tk), lambda b, h, i: (b, h, 0, 0, 0)),
        ],
        out_specs=pl.BlockSpec((1, tq, groups * dv), lambda b, h, i: (b, i, h)),
        out_shape=jax.ShapeDtypeStruct((batch, seq, q_heads * dv), BF16),
        scratch_shapes=[
            pltpu.VMEM((2, tk, mq), F32),
            pltpu.VMEM((2, tk, mq), BF16),
            pltpu.VMEM((dv + SUM_ROWS, mq), F32),
        ],
        compiler_params=pltpu.CompilerParams(
            dimension_semantics=("arbitrary", "arbitrary", "arbitrary"),
            vmem_limit_bytes=VMEM_LIMIT),
        name=f"attn_g{groups}",
    )(qt, k, vt)


def _merge_kernel(x_ref, mod_ref, h_ref, yg_ref, ym_ref, wga_ref, wgb_ref, bga_ref, bgb_ref,
                  wbg_ref, wbm_ref, wo_ref, g_ref, b_ref, o_ref, h_scr, acc_scr):
    j = pl.program_id(1)
    last = pl.num_programs(1) - 1

    def chunk(h, rows=slice(None)):
        la = jnp.dot(h, wga_ref[...], preferred_element_type=F32) + bga_ref[...]
        lb = jnp.dot(h, wgb_ref[...], preferred_element_type=F32) + bgb_ref[...]
        a = jnp.dot(yg_ref[rows, :], wbg_ref[...], preferred_element_type=F32)
        b = jnp.dot(ym_ref[rows, :], wbm_ref[...], preferred_element_type=F32)
        merged = jax.nn.sigmoid(la) * a + jax.nn.sigmoid(lb) * b
        return jnp.dot(merged.astype(BF16), wo_ref[...], preferred_element_type=F32)

    @pl.when(j == 0)
    def _():
        h = h_ref[...]
        h_scr[...] = h
        acc_scr[...] = chunk(h)

    @pl.when((j > 0) & (j < last))
    def _():
        acc_scr[...] += chunk(h_scr[...])

    @pl.when(j == last)
    def _():
        half = x_ref.shape[0] // 2
        for rows in (slice(0, half), slice(half, 2 * half)):
            acc = acc_scr[rows, :] + chunk(h_scr[rows, :], rows)
            r = DEEPNORM_ALPHA * x_ref[rows, :] + mod_ref[0, 2:3, :] * acc
            o_ref[rows, :] = _ln(r) * g_ref[...] + b_ref[...]


def _merge(x2, mod, h, yg, ym, w_gate, b_gate, wbg, wbm, wo, ln_g, ln_b, seq):
    t = x2.shape[0]
    tm, tn = TM_MERGE, TN_MERGE
    nt = seq // tm
    nj = D_MODEL // tn
    tok = lambda i, j: (i, 0)
    return pl.pallas_call(
        _merge_kernel,
        grid=(t // tm, nj),
        in_specs=[
            pl.BlockSpec((tm, D_MODEL), tok),
            pl.BlockSpec((1, 6, D_MODEL), lambda i, j: (i // nt, 0, 0)),
            pl.BlockSpec((tm, D_MODEL), tok),
            pl.BlockSpec((tm, yg.shape[1]), tok),
            pl.BlockSpec((tm, ym.shape[1]), tok),
            pl.BlockSpec((D_MODEL, tn), lambda i, j: (0, j)),
            pl.BlockSpec((D_MODEL, tn), lambda i, j: (0, j + nj)),
            pl.BlockSpec((1, tn), lambda i, j: (0, j)),
            pl.BlockSpec((1, tn), lambda i, j: (0, j + nj)),
            pl.BlockSpec((wbg.shape[0], tn), lambda i, j: (0, j)),
            pl.BlockSpec((wbm.shape[0], tn), lambda i, j: (0, j)),
            pl.BlockSpec((tn, D_MODEL), lambda i, j: (j, 0)),
            pl.BlockSpec((1, D_MODEL), lambda i, j: (0, 0)),
            pl.BlockSpec((1, D_MODEL), lambda i, j: (0, 0)),
        ],
        out_specs=pl.BlockSpec((tm, D_MODEL), tok),
        out_shape=jax.ShapeDtypeStruct((t, D_MODEL), F32),
        scratch_shapes=[pltpu.VMEM((tm, D_MODEL), BF16), pltpu.VMEM((tm, D_MODEL), F32)],
        compiler_params=pltpu.CompilerParams(
            dimension_semantics=("arbitrary", "arbitrary"), vmem_limit_bytes=VMEM_LIMIT),
        name="merge",
    )(x2, mod, h, yg, ym, w_gate, w_gate, b_gate, b_gate, wbg, wbm, wo, ln_g, ln_b)


def _ffn_kernel(x_ref, mod_ref, wg_ref, wu_ref, wd_ref, g_ref, b_ref, o_ref, h_scr, acc_scr):
    j = pl.program_id(1)
    last = pl.num_programs(1) - 1

    def chunk(h):
        gt = jnp.dot(h, wg_ref[...], preferred_element_type=F32)
        up = jnp.dot(h, wu_ref[...], preferred_element_type=F32)
        a = (gt * jax.nn.sigmoid(gt) * up).astype(BF16)
        return jnp.dot(a, wd_ref[...], preferred_element_type=F32)

    halves = [slice(r * (x_ref.shape[0] // 2), (r + 1) * (x_ref.shape[0] // 2)) for r in range(2)]

    @pl.when(j == 0)
    def _():
        shift = mod_ref[0, 3:4, :]
        scale = mod_ref[0, 4:5, :]
        for rows in halves:
            h = (_ln(x_ref[rows, :]) * (1.0 + scale) + shift).astype(BF16)
            h_scr[rows, :] = h
            acc_scr[rows, :] = chunk(h)

    @pl.when((j > 0) & (j < last))
    def _():
        acc_scr[...] += chunk(h_scr[...])

    @pl.when(j == last)
    def _():
        for rows in halves:
            acc = acc_scr[rows, :] + chunk(h_scr[rows, :])
            r = DEEPNORM_ALPHA * x_ref[rows, :] + mod_ref[0, 5:6, :] * acc
            o_ref[rows, :] = _ln(r) * g_ref[...] + b_ref[...]


def _ffn(x1, mod, wg, wu, wd, ln_g, ln_b, seq):
    t = x1.shape[0]
    tm, tf = TM_FFN, TF_FFN
    nt = seq // tm
    tok = lambda i, j: (i, 0)
    return pl.pallas_call(
        _ffn_kernel,
        grid=(t // tm, D_FF // tf),
        in_specs=[
            pl.BlockSpec((tm, D_MODEL), tok),
            pl.BlockSpec((1, 6, D_MODEL), lambda i, j: (i // nt, 0, 0)),
            pl.BlockSpec((D_MODEL, tf), lambda i, j: (0, j)),
            pl.BlockSpec((D_MODEL, tf), lambda i, j: (0, j)),
            pl.BlockSpec((tf, D_MODEL), lambda i, j: (j, 0)),
            pl.BlockSpec((1, D_MODEL), lambda i, j: (0, 0)),
            pl.BlockSpec((1, D_MODEL), lambda i, j: (0, 0)),
        ],
        out_specs=pl.BlockSpec((tm, D_MODEL), tok),
        out_shape=jax.ShapeDtypeStruct((t, D_MODEL), F32),
        scratch_shapes=[pltpu.VMEM((tm, D_MODEL), BF16), pltpu.VMEM((tm, D_MODEL), F32)],
        compiler_params=pltpu.CompilerParams(
            dimension_semantics=("arbitrary", "arbitrary"), vmem_limit_bytes=VMEM_LIMIT),
        name="ffn",
    )(x1, mod, wg, wu, wd, ln_g, ln_b)


def _rope_tables(seq, dim):
    rows = seq // GRID_W
    quarter = dim // 4
    inv_freq = ROPE_THETA ** (-jnp.arange(quarter, dtype=F32) / quarter)
    row_ang = jnp.arange(rows, dtype=F32)[:, None] * inv_freq
    col_ang = jnp.arange(GRID_W, dtype=F32)[:, None] * inv_freq

    def table(fn):
        return jnp.concatenate([
            jnp.broadcast_to(fn(row_ang)[:, None, :], (rows, GRID_W, quarter)),
            jnp.broadcast_to(fn(col_ang)[None, :, :], (rows, GRID_W, quarter)),
        ], axis=-1).reshape(seq, 2 * quarter)

    half = dim // 2
    pad = jnp.zeros((seq, LANES // 2 - half), F32)
    cos, sin = table(jnp.cos), table(jnp.sin)
    c = jnp.concatenate([cos, pad, cos, pad], axis=-1)
    s = jnp.concatenate([-sin, pad, sin, pad], axis=-1)
    return c, s


def _deinterleave(n):
    return np.concatenate([np.arange(0, n, 2), np.arange(1, n, 2)])


def kernel(x, c, w_ada, b_ada, w_in, b_gates, gqa_q_gain, gqa_k_gain, mla_q_gain, mla_kv_gain,
           w_mla_uq, w_mla_ukv, w_branch_gqa, w_branch_mla, w_out, ln1_g, ln1_b,
           w_ffn_gate, w_ffn_up, w_ffn_down, ln2_g, ln2_b):
    batch, seq, d = x.shape
    assert d == D_MODEL and w_ada.shape[0] == DEPTH
    t = batch * seq
    x2 = x.reshape(t, d)

    cg, sg = _rope_tables(seq, HEAD_DIM)
    cm, sm = _rope_tables(seq, MLA_ROPE_DIM)
    perm_head = _deinterleave(HEAD_DIM)
    zeros32 = lambda rows: jnp.zeros((rows, LANES // 2 - MLA_ROPE_DIM // 2), BF16)

    c_pad = jnp.zeros((8, d), F32).at[:batch].set(c)

    for l in range(DEPTH):
        mod = _adaln(c_pad, w_ada[l], b_ada[l][None, :])[:batch].reshape(batch, 6, d)

        wl = w_in[l]
        o0 = 0
        wq = wl[:, o0:o0 + GQA_Q_COLS].reshape(d, GQA_Q_HEADS, HEAD_DIM)[:, :, perm_head]
        o0 += GQA_Q_COLS
        wk = wl[:, o0:o0 + GQA_KV_COLS].reshape(d, GQA_KV_HEADS, HEAD_DIM)[:, :, perm_head]
        o0 += GQA_KV_COLS
        wv = wl[:, o0:o0 + GQA_KV_COLS]
        o0 += GQA_KV_COLS
        wql = wl[:, o0:o0 + MLA_Q_RANK]
        o0 += MLA_Q_RANK
        wkvl = wl[:, o0:o0 + MLA_KV_RANK]
        o0 += MLA_KV_RANK
        wkr = wl[:, o0:o0 + MLA_ROPE_DIM].astype(BF16)
        o0 += MLA_ROPE_DIM
        w_gate = wl[:, o0:].astype(BF16)
        wkr_pad = jnp.concatenate(
            [wkr[:, 0::2], zeros32(d), wkr[:, 1::2], zeros32(d)], axis=-1)
        w_attn = jnp.concatenate([
            wq.reshape(d, GQA_Q_COLS).astype(BF16), wk.reshape(d, GQA_KV_COLS).astype(BF16),
            wv.astype(BF16), wql.astype(BF16), wkvl.astype(BF16), wkr_pad], axis=-1)

        uq = w_mla_uq[l].astype(BF16).reshape(MLA_Q_RANK, MLA_HEADS, MLA_NOPE_DIM + MLA_ROPE_DIM)
        uq_r = uq[:, :, MLA_NOPE_DIM:]
        z = jnp.zeros((MLA_Q_RANK, MLA_HEADS, LANES // 2 - MLA_ROPE_DIM // 2), BF16)
        wuq = jnp.concatenate(
            [uq[:, :, :MLA_NOPE_DIM], uq_r[:, :, 0::2], z, uq_r[:, :, 1::2], z],
            axis=-1).reshape(MLA_Q_RANK, MLA_HEADS * MLA_QK_PAD)
        ukv = w_mla_ukv[l].astype(BF16).reshape(MLA_KV_RANK, MLA_HEADS, MLA_NOPE_DIM + MLA_V_DIM)
        wuk = ukv[:, :, :MLA_NOPE_DIM].reshape(MLA_KV_RANK, MLA_HEADS * MLA_NOPE_DIM)
        wuv = ukv[:, :, MLA_NOPE_DIM:].reshape(MLA_KV_RANK, MLA_HEADS * MLA_V_DIM)

        gq = (gqa_q_gain[l][perm_head] * (LOG2E * HEAD_DIM ** -0.5))[None, :]
        gk = gqa_k_gain[l][perm_head][None, :]

        h1, qgt, kg, vgt, qmt, km, vmt = _inproj(
            x2, mod, cg, sg, cm, sm, w_attn, gq, gk, mla_q_gain[l][None, :],
            mla_kv_gain[l][None, :], wuq, wuk, wuv, batch, seq)

        y_gqa = _attention(
            qgt, kg.reshape(batch, seq, GQA_KV_COLS), vgt,
            kv_heads=GQA_KV_HEADS, dv=HEAD_DIM, tq=TQ_GQA)
        y_mla = _attention(
            qmt, km.reshape(batch, seq, MLA_HEADS * MLA_QK_PAD), vmt,
            kv_heads=MLA_HEADS, dv=MLA_V_DIM, tq=TQ_MLA)

        x2 = _merge(
            x2, mod, h1, y_gqa.reshape(t, GQA_Q_COLS), y_mla.reshape(t, MLA_HEADS * MLA_V_DIM),
            w_gate, b_gates[l][None, :], w_branch_gqa[l].astype(BF16),
            w_branch_mla[l].astype(BF16), w_out[l].astype(BF16),
            ln1_g[l][None, :], ln1_b[l][None, :], seq)

        x2 = _ffn(
            x2, mod, w_ffn_gate[l].astype(BF16), w_ffn_up[l].astype(BF16),
            w_ffn_down[l].astype(BF16), ln2_g[l][None, :], ln2_b[l][None, :], seq)

    return x2.reshape(batch, seq, d)
```

```python
import functools

import jax
import jax.numpy as jnp
import numpy as np
from jax import lax
from jax.experimental import pallas as pl
from jax.experimental.pallas import tpu as pltpu

D_MODEL = 2048
GRID_W = 64
ROPE_THETA = 10000.0
HEAD_DIM = 128
GQA_Q_HEADS = 8
GQA_KV_HEADS = 2
MLA_HEADS = 8
MLA_Q_RANK = 512
MLA_KV_RANK = 512
MLA_NOPE_DIM = 128
MLA_ROPE_DIM = 64
MLA_V_DIM = 128
D_FF = 5632
DEPTH = 1
DEEPNORM_ALPHA = (2.0 * DEPTH) ** 0.25
LN_EPS = 1e-5
RMS_EPS = 1e-6

GQA_Q_COLS = GQA_Q_HEADS * HEAD_DIM
GQA_KV_COLS = GQA_KV_HEADS * HEAD_DIM
MLA_QK_PAD = 256
LANES = 128
LOG2E = 1.4426950408889634

VMEM_LIMIT = 56 * 1024 * 1024

TM_IN = 256
TV_CHUNK = TM_IN
TK_ATT = 256
ATT_STEPS = 14
SUM_ROWS = 16
EXP_ROWS = 64
TQ_GQA = 1024
TQ_MLA = 4096
TM_MERGE = 512
TN_MERGE = 512
TM_FFN = 512
TF_FFN = 512

BF16 = jnp.bfloat16
F32 = jnp.float32


def _const_spec(shape):
    nd = len(shape)
    return pl.BlockSpec(shape, lambda *_: (0,) * nd, pipeline_mode=pl.Buffered(1))


def _ln(x):
    mu = jnp.mean(x, axis=-1, keepdims=True)
    xc = x - mu
    var = jnp.mean(xc * xc, axis=-1, keepdims=True)
    return xc * lax.rsqrt(var + LN_EPS)


def _rms(x, gain):
    ms = jnp.mean(x * x, axis=-1, keepdims=True)
    return x * lax.rsqrt(ms + RMS_EPS) * gain


def _rope(x, c, s):
    return x * c + pltpu.roll(x, LANES // 2, 1) * s


def _adaln_kernel(c_ref, w_ref, b_ref, o_ref):
    c = c_ref[...]
    act = (c * jax.nn.sigmoid(c)).astype(BF16)
    o_ref[...] = jnp.dot(act, w_ref[...].astype(BF16), preferred_element_type=F32) + b_ref[...]


def _adaln(c_pad, w_ada, b_ada):
    n = w_ada.shape[1]
    tn = 1024
    return pl.pallas_call(
        _adaln_kernel,
        grid=(n // tn,),
        in_specs=[
            pl.BlockSpec((8, D_MODEL), lambda j: (0, 0)),
            pl.BlockSpec((D_MODEL, tn), lambda j: (0, j)),
            pl.BlockSpec((1, tn), lambda j: (0, j)),
        ],
        out_specs=pl.BlockSpec((8, tn), lambda j: (0, j)),
        out_shape=jax.ShapeDtypeStruct((8, n), F32),
        compiler_params=pltpu.CompilerParams(
            dimension_semantics=("arbitrary",), vmem_limit_bytes=VMEM_LIMIT),
        name="adaln",
    )(c_pad, w_ada, b_ada)


def _inproj_kernel(x_ref, mod_ref, cg_ref, sg_ref, cm_ref, sm_ref, w_ref,
                   gq_ref, gk_ref, gql_ref, gkvl_ref, wuq_ref, wuk_ref, wuv_ref,
                   h_ref, qg_ref, kg_ref, vgt_ref, qm_ref, km_ref, vmt_ref, *, mla_scale):
    x = x_ref[...]
    shift = mod_ref[0, 0:1, :]
    scale = mod_ref[0, 1:2, :]
    h = (_ln(x) * (1.0 + scale) + shift).astype(BF16)
    h_ref[...] = h
    proj = jnp.dot(h, w_ref[...], preferred_element_type=F32)

    cg = cg_ref[...]
    sg = sg_ref[...]
    cm = cm_ref[...]
    sm = sm_ref[...]
    ones_rows = jnp.ones((SUM_ROWS, x.shape[0]), BF16)

    for hd in range(GQA_Q_HEADS):
        q = proj[:, hd * HEAD_DIM:(hd + 1) * HEAD_DIM]
        q = _rope(_rms(q, gq_ref[...]), cg, sg)
        qg_ref[0, hd, 0] = q.T.astype(BF16)
    off = GQA_Q_COLS
    for hd in range(GQA_KV_HEADS):
        k = proj[:, off + hd * HEAD_DIM: off + (hd + 1) * HEAD_DIM]
        k = _rope(_rms(k, gk_ref[...]), cg, sg)
        kg_ref[:, hd * HEAD_DIM:(hd + 1) * HEAD_DIM] = k.astype(BF16)
    off += GQA_KV_COLS
    for hd in range(GQA_KV_HEADS):
        v = proj[:, off + hd * HEAD_DIM: off + (hd + 1) * HEAD_DIM]
        vgt_ref[0, hd, 0, 0:HEAD_DIM, :] = v.T.astype(BF16)
        vgt_ref[0, hd, 0, HEAD_DIM:, :] = ones_rows
    off += GQA_KV_COLS

    q_lat = _rms(proj[:, off: off + MLA_Q_RANK], gql_ref[...]).astype(BF16)
    off += MLA_Q_RANK
    kv_lat = _rms(proj[:, off: off + MLA_KV_RANK], gkvl_ref[...]).astype(BF16)
    off += MLA_KV_RANK
    k_rope = _rope(proj[:, off: off + LANES], cm, sm).astype(BF16)

    q_m = jnp.dot(q_lat, wuq_ref[...], preferred_element_type=F32) * mla_scale
    k_n = jnp.dot(kv_lat, wuk_ref[...], preferred_element_type=F32)
    v_m = jnp.dot(kv_lat, wuv_ref[...], preferred_element_type=F32)
    for hd in range(MLA_HEADS):
        b0 = hd * MLA_QK_PAD
        qm_ref[0, hd, 0, 0:LANES, :] = q_m[:, b0: b0 + LANES].T.astype(BF16)
        qm_ref[0, hd, 0, LANES:, :] = _rope(
            q_m[:, b0 + LANES: b0 + 2 * LANES], cm, sm).T.astype(BF16)
        km_ref[:, b0: b0 + LANES] = k_n[:, hd * LANES:(hd + 1) * LANES].astype(BF16)
        km_ref[:, b0 + LANES: b0 + 2 * LANES] = k_rope
        vmt_ref[0, hd, 0, 0:MLA_V_DIM, :] = (
            v_m[:, hd * MLA_V_DIM:(hd + 1) * MLA_V_DIM].T.astype(BF16))
        vmt_ref[0, hd, 0, MLA_V_DIM:, :] = ones_rows


def _inproj(x2, mod, cg, sg, cm, sm, w_attn, gq, gk, gql, gkvl, wuq, wuk, wuv, batch, seq):
    t = x2.shape[0]
    tm = TM_IN
    nt = seq // tm
    n_attn = w_attn.shape[1]
    kern = functools.partial(
        _inproj_kernel, mla_scale=float(LOG2E * (MLA_NOPE_DIM + MLA_ROPE_DIM) ** -0.5))
    tok = lambda i: (i, 0)
    pos = lambda i: (i % nt, 0)
    in_specs = [
        pl.BlockSpec((tm, D_MODEL), tok),
        pl.BlockSpec((1, 6, D_MODEL), lambda i: (i // nt, 0, 0)),
        pl.BlockSpec((tm, LANES), pos),
        pl.BlockSpec((tm, LANES), pos),
        pl.BlockSpec((tm, LANES), pos),
        pl.BlockSpec((tm, LANES), pos),
        _const_spec((D_MODEL, n_attn)),
        _const_spec((1, HEAD_DIM)),
        _const_spec((1, HEAD_DIM)),
        _const_spec((1, MLA_Q_RANK)),
        _const_spec((1, MLA_KV_RANK)),
        _const_spec(wuq.shape),
        _const_spec(wuk.shape),
        _const_spec(wuv.shape),
    ]
    out_shape = [
        jax.ShapeDtypeStruct((t, D_MODEL), BF16),
        jax.ShapeDtypeStruct((batch, GQA_Q_HEADS, seq // tm, HEAD_DIM, tm), BF16),
        jax.ShapeDtypeStruct((t, GQA_KV_COLS), BF16),
        jax.ShapeDtypeStruct(
            (batch, GQA_KV_HEADS, seq // TV_CHUNK, HEAD_DIM + SUM_ROWS, TV_CHUNK), BF16),
        jax.ShapeDtypeStruct((batch, MLA_HEADS, seq // tm, MLA_QK_PAD, tm), BF16),
        jax.ShapeDtypeStruct((t, MLA_HEADS * MLA_QK_PAD), BF16),
        jax.ShapeDtypeStruct(
            (batch, MLA_HEADS, seq // TV_CHUNK, MLA_V_DIM + SUM_ROWS, TV_CHUNK), BF16),
    ]
    vt_map = lambda i: (i // nt, 0, i % nt, 0, 0)
    out_specs = [
        pl.BlockSpec((tm, D_MODEL), tok),
        pl.BlockSpec((1, GQA_Q_HEADS, 1, HEAD_DIM, tm), vt_map),
        pl.BlockSpec((tm, GQA_KV_COLS), tok),
        pl.BlockSpec((1, GQA_KV_HEADS, 1, HEAD_DIM + SUM_ROWS, TV_CHUNK), vt_map),
        pl.BlockSpec((1, MLA_HEADS, 1, MLA_QK_PAD, tm), vt_map),
        pl.BlockSpec((tm, MLA_HEADS * MLA_QK_PAD), tok),
        pl.BlockSpec((1, MLA_HEADS, 1, MLA_V_DIM + SUM_ROWS, TV_CHUNK), vt_map),
    ]
    return pl.pallas_call(
        kern,
        grid=(t // tm,),
        in_specs=in_specs,
        out_specs=out_specs,
        out_shape=out_shape,
        compiler_params=pltpu.CompilerParams(
            dimension_semantics=("arbitrary",), vmem_limit_bytes=VMEM_LIMIT),
        name="inproj",
    )(x2, mod, cg, sg, cm, sm, w_attn, gq, gk, gql, gkvl, wuq, wuk, wuv)


def _attn_kernel(qt_ref, k_ref, vt_ref, o_ref, st_scr, pb_scr, acc_scr,
                 *, groups, dq, dv, tq, tk, nk, steps):
    mq = groups * tq
    cb = qt_ref.shape[-1]
    nb = mq // cb
    cols = [slice(n * cb, (n + 1) * cb) for n in range(nb)]
    grp = [(n * cb) // tq for n in range(nb)]
    tok = [(n * cb) % tq for n in range(nb)]

    def scores(c, st_ref, n):
        kc = k_ref[0, pl.ds(pl.multiple_of(c * tk, tk), tk), :]
        st = jnp.dot(kc, qt_ref[0, grp[n], tok[n] // cb],
                     preferred_element_type=F32)
        st_ref[:, cols[n]] = st
        return jnp.max(st, axis=0, keepdims=True)

    def values(c, pb_ref, alpha, n, init=False):
        pv = jnp.dot(vt_ref[0, 0, c], pb_ref[:, cols[n]], preferred_element_type=F32)
        acc_scr[:, cols[n]] = pv if init else alpha * acc_scr[:, cols[n]] + pv

    def softmax(st_ref, pb_ref, cmax, m_prev, n):
        m_new = jnp.maximum(m_prev, cmax)
        alpha = jnp.exp2(m_prev - m_new)
        for r in range(tk // EXP_ROWS):
            rows = slice(r * EXP_ROWS, (r + 1) * EXP_ROWS)
            pb_ref[rows, cols[n]] = jnp.exp2((st_ref[rows, cols[n]] - m_new).astype(BF16))
        return m_new, alpha

    def step(c, cur, nxt, carry, first=False, second=False, last=False):
        cmax, m_prev, alpha_prev = carry
        cmax_next, m_new, alpha = list(cmax), [], []
        for n in range(nb):
            if not last:
                cmax_next[n] = scores(c + 1, st_scr.at[nxt], n)
        for n in range(nb):
            m_n, a_n = softmax(st_scr.at[cur], pb_scr.at[cur], cmax[n], m_prev[n], n)
            m_new.append(m_n)
            alpha.append(a_n)
            if not first:
                values(c - 1, pb_scr.at[nxt], alpha_prev[n], n, init=second)
        return cmax_next, m_new, alpha

    cmax0 = []
    for n in range(nb):
        cmax0.append(scores(0, st_scr.at[0], n))
    carry = (cmax0, [jnp.full((1, cb), -jnp.inf, F32)] * nb, [jnp.ones((1, cb), F32)] * nb)
    carry = step(0, 0, 1, carry, first=True)
    carry = step(1, 1, 0, carry, second=True)

    n_loop = (nk - 3) // steps

    def body(i, carry):
        c0 = steps * i + 2
        for s in range(steps):
            carry = step(c0 + s, s % 2, (1 + s) % 2, carry)
        return carry

    if n_loop == 1:
        carry = body(0, carry)
    elif n_loop > 1:
        carry = lax.fori_loop(0, n_loop, body, carry)
    for c in range(n_loop * steps + 2, nk - 1):
        carry = step(c, c % 2, (c + 1) % 2, carry)
    _, _, alpha = step(nk - 1, (nk - 1) % 2, nk % 2, carry, last=True)
    for n in range(nb):
        acc = alpha[n] * acc_scr[:, cols[n]] + jnp.dot(
            vt_ref[0, 0, nk - 1], pb_scr[(nk - 1) % 2, :, cols[n]], preferred_element_type=F32)
        out_t = acc[0:dv] / acc[dv:dv + 1]
        o_ref[0, tok[n]:tok[n] + cb, grp[n] * dv:(grp[n] + 1) * dv] = out_t.T.astype(o_ref.dtype)


def _attention(qt, k, vt, *, kv_heads, dv, tq):
    batch, q_heads, n_qc, dq, cb = qt.shape
    seq = n_qc * cb
    groups = q_heads // kv_heads
    tk = TK_ATT
    assert tk == TV_CHUNK and vt.shape[-2:] == (dv + SUM_ROWS, tk) and tq % cb == 0
    nk = seq // tk
    mq = groups * tq
    kern = functools.partial(_attn_kernel, groups=groups, dq=dq, dv=dv, tq=tq, tk=tk, nk=nk,
                             steps=ATT_STEPS)
    return pl.pallas_call(
        kern,
        grid=(batch, kv_heads, seq // tq),
        in_specs=[
            pl.BlockSpec((1, groups, tq // cb, dq, cb), lambda b, h, i: (b, h, i, 0, 0)),
            pl.BlockSpec((1, seq, dq), lambda b, h, i: (b, 0, h)),
            pl.BlockSpec((1, 1, nk, dv + SUM_ROWS, tk), lambda b, h, i: (b, h, 0, 0, 0)),
        ],
        out_specs=pl.BlockSpec((1, tq, groups * dv), lambda b, h, i: (b, i, h)),
        out_shape=jax.ShapeDtypeStruct((batch, seq, q_heads * dv), BF16),
        scratch_shapes=[
            pltpu.VMEM((2, tk, mq), F32),
            pltpu.VMEM((2, tk, mq), BF16),
            pltpu.VMEM((dv + SUM_ROWS, mq), F32),
        ],
        compiler_params=pltpu.CompilerParams(
            dimension_semantics=("arbitrary", "arbitrary", "arbitrary"),
            vmem_limit_bytes=VMEM_LIMIT),
        name=f"attn_g{groups}",
    )(qt, k, vt)


def _merge_kernel(x_ref, mod_ref, h_ref, yg_ref, ym_ref, wga_ref, wgb_ref, bga_ref, bgb_ref,
                  wbg_ref, wbm_ref, wo_ref, g_ref, b_ref, o_ref, h_scr, acc_scr):
    j = pl.program_id(1)
    last = pl.num_programs(1) - 1

    def chunk(h):
        la = jnp.dot(h, wga_ref[...], preferred_element_type=F32) + bga_ref[...]
        lb = jnp.dot(h, wgb_ref[...], preferred_element_type=F32) + bgb_ref[...]
        a = jnp.dot(yg_ref[...], wbg_ref[...], preferred_element_type=F32)
        b = jnp.dot(ym_ref[...], wbm_ref[...], preferred_element_type=F32)
        merged = jax.nn.sigmoid(la) * a + jax.nn.sigmoid(lb) * b
        return jnp.dot(merged.astype(BF16), wo_ref[...], preferred_element_type=F32)

    @pl.when(j == 0)
    def _():
        h = h_ref[...]
        h_scr[...] = h
        acc_scr[...] = chunk(h)

    @pl.when((j > 0) & (j < last))
    def _():
        acc_scr[...] += chunk(h_scr[...])

    @pl.when(j == last)
    def _():
        acc = acc_scr[...] + chunk(h_scr[...])
        r = DEEPNORM_ALPHA * x_ref[...] + mod_ref[0, 2:3, :] * acc
        o_ref[...] = _ln(r) * g_ref[...] + b_ref[...]


def _merge(x2, mod, h, yg, ym, w_gate, b_gate, wbg, wbm, wo, ln_g, ln_b, seq):
    t = x2.shape[0]
    tm, tn = TM_MERGE, TN_MERGE
    nt = seq // tm
    nj = D_MODEL // tn
    tok = lambda i, j: (i, 0)
    return pl.pallas_call(
        _merge_kernel,
        grid=(t // tm, nj),
        in_specs=[
            pl.BlockSpec((tm, D_MODEL), tok),
            pl.BlockSpec((1, 6, D_MODEL), lambda i, j: (i // nt, 0, 0)),
            pl.BlockSpec((tm, D_MODEL), tok),
            pl.BlockSpec((tm, yg.shape[1]), tok),
            pl.BlockSpec((tm, ym.shape[1]), tok),
            pl.BlockSpec((D_MODEL, tn), lambda i, j: (0, j)),
            pl.BlockSpec((D_MODEL, tn), lambda i, j: (0, j + nj)),
            pl.BlockSpec((1, tn), lambda i, j: (0, j)),
            pl.BlockSpec((1, tn), lambda i, j: (0, j + nj)),
            pl.BlockSpec((wbg.shape[0], tn), lambda i, j: (0, j)),
            pl.BlockSpec((wbm.shape[0], tn), lambda i, j: (0, j)),
            pl.BlockSpec((tn, D_MODEL), lambda i, j: (j, 0)),
            pl.BlockSpec((1, D_MODEL), lambda i, j: (0, 0)),
            pl.BlockSpec((1, D_MODEL), lambda i, j: (0, 0)),
        ],
        out_specs=pl.BlockSpec((tm, D_MODEL), tok),
        out_shape=jax.ShapeDtypeStruct((t, D_MODEL), F32),
        scratch_shapes=[pltpu.VMEM((tm, D_MODEL), BF16), pltpu.VMEM((tm, D_MODEL), F32)],
        compiler_params=pltpu.CompilerParams(
            dimension_semantics=("arbitrary", "arbitrary"), vmem_limit_bytes=VMEM_LIMIT),
        name="merge",
    )(x2, mod, h, yg, ym, w_gate, w_gate, b_gate, b_gate, wbg, wbm, wo, ln_g, ln_b)


def _ffn_kernel(x_ref, mod_ref, wg_ref, wu_ref, wd_ref, g_ref, b_ref, o_ref, h_scr, acc_scr):
    j = pl.program_id(1)
    last = pl.num_programs(1) - 1

    def chunk(h):
        gt = jnp.dot(h, wg_ref[...], preferred_element_type=F32)
        up = jnp.dot(h, wu_ref[...], preferred_element_type=F32)
        a = (gt * jax.nn.sigmoid(gt) * up).astype(BF16)
        return jnp.dot(a, wd_ref[...], preferred_element_type=F32)

    @pl.when(j == 0)
    def _():
        shift = mod_ref[0, 3:4, :]
        scale = mod_ref[0, 4:5, :]
        h = (_ln(x_ref[...]) * (1.0 + scale) + shift).astype(BF16)
        h_scr[...] = h
        acc_scr[...] = chunk(h)

    @pl.when((j > 0) & (j < last))
    def _():
        acc_scr[...] += chunk(h_scr[...])

    @pl.when(j == last)
    def _():
        acc = acc_scr[...] + chunk(h_scr[...])
        r = DEEPNORM_ALPHA * x_ref[...] + mod_ref[0, 5:6, :] * acc
        o_ref[...] = _ln(r) * g_ref[...] + b_ref[...]


def _ffn(x1, mod, wg, wu, wd, ln_g, ln_b, seq):
    t = x1.shape[0]
    tm, tf = TM_FFN, TF_FFN
    nt = seq // tm
    tok = lambda i, j: (i, 0)
    return pl.pallas_call(
        _ffn_kernel,
        grid=(t // tm, D_FF // tf),
        in_specs=[
            pl.BlockSpec((tm, D_MODEL), tok),
            pl.BlockSpec((1, 6, D_MODEL), lambda i, j: (i // nt, 0, 0)),
            pl.BlockSpec((D_MODEL, tf), lambda i, j: (0, j)),
            pl.BlockSpec((D_MODEL, tf), lambda i, j: (0, j)),
            pl.BlockSpec((tf, D_MODEL), lambda i, j: (j, 0)),
            pl.BlockSpec((1, D_MODEL), lambda i, j: (0, 0)),
            pl.BlockSpec((1, D_MODEL), lambda i, j: (0, 0)),
        ],
        out_specs=pl.BlockSpec((tm, D_MODEL), tok),
        out_shape=jax.ShapeDtypeStruct((t, D_MODEL), F32),
        scratch_shapes=[pltpu.VMEM((tm, D_MODEL), BF16), pltpu.VMEM((tm, D_MODEL), F32)],
        compiler_params=pltpu.CompilerParams(
            dimension_semantics=("arbitrary", "arbitrary"), vmem_limit_bytes=VMEM_LIMIT),
        name="ffn",
    )(x1, mod, wg, wu, wd, ln_g, ln_b)


def _rope_tables(seq, dim):
    rows = seq // GRID_W
    quarter = dim // 4
    inv_freq = ROPE_THETA ** (-jnp.arange(quarter, dtype=F32) / quarter)
    row_ang = jnp.arange(rows, dtype=F32)[:, None] * inv_freq
    col_ang = jnp.arange(GRID_W, dtype=F32)[:, None] * inv_freq

    def table(fn):
        return jnp.concatenate([
            jnp.broadcast_to(fn(row_ang)[:, None, :], (rows, GRID_W, quarter)),
            jnp.broadcast_to(fn(col_ang)[None, :, :], (rows, GRID_W, quarter)),
        ], axis=-1).reshape(seq, 2 * quarter)

    half = dim // 2
    pad = jnp.zeros((seq, LANES // 2 - half), F32)
    cos, sin = table(jnp.cos), table(jnp.sin)
    c = jnp.concatenate([cos, pad, cos, pad], axis=-1)
    s = jnp.concatenate([-sin, pad, sin, pad], axis=-1)
    return c, s


def _deinterleave(n):
    return np.concatenate([np.arange(0, n, 2), np.arange(1, n, 2)])


def kernel(x, c, w_ada, b_ada, w_in, b_gates, gqa_q_gain, gqa_k_gain, mla_q_gain, mla_kv_gain,
           w_mla_uq, w_mla_ukv, w_branch_gqa, w_branch_mla, w_out, ln1_g, ln1_b,
           w_ffn_gate, w_ffn_up, w_ffn_down, ln2_g, ln2_b):
    batch, seq, d = x.shape
    assert d == D_MODEL and w_ada.shape[0] == DEPTH
    t = batch * seq
    x2 = x.reshape(t, d)

    cg, sg = _rope_tables(seq, HEAD_DIM)
    cm, sm = _rope_tables(seq, MLA_ROPE_DIM)
    perm_head = _deinterleave(HEAD_DIM)
    zeros32 = lambda rows: jnp.zeros((rows, LANES // 2 - MLA_ROPE_DIM // 2), BF16)

    c_pad = jnp.zeros((8, d), F32).at[:batch].set(c)

    for l in range(DEPTH):
        mod = _adaln(c_pad, w_ada[l], b_ada[l][None, :])[:batch].reshape(batch, 6, d)

        wl = w_in[l]
        o0 = 0
        wq = wl[:, o0:o0 + GQA_Q_COLS].reshape(d, GQA_Q_HEADS, HEAD_DIM)[:, :, perm_head]
        o0 += GQA_Q_COLS
        wk = wl[:, o0:o0 + GQA_KV_COLS].reshape(d, GQA_KV_HEADS, HEAD_DIM)[:, :, perm_head]
        o0 += GQA_KV_COLS
        wv = wl[:, o0:o0 + GQA_KV_COLS]
        o0 += GQA_KV_COLS
        wql = wl[:, o0:o0 + MLA_Q_RANK]
        o0 += MLA_Q_RANK
        wkvl = wl[:, o0:o0 + MLA_KV_RANK]
        o0 += MLA_KV_RANK
        wkr = wl[:, o0:o0 + MLA_ROPE_DIM].astype(BF16)
        o0 += MLA_ROPE_DIM
        w_gate = wl[:, o0:].astype(BF16)
        wkr_pad = jnp.concatenate(
            [wkr[:, 0::2], zeros32(d), wkr[:, 1::2], zeros32(d)], axis=-1)
        w_attn = jnp.concatenate([
            wq.reshape(d, GQA_Q_COLS).astype(BF16), wk.reshape(d, GQA_KV_COLS).astype(BF16),
            wv.astype(BF16), wql.astype(BF16), wkvl.astype(BF16), wkr_pad], axis=-1)

        uq = w_mla_uq[l].astype(BF16).reshape(MLA_Q_RANK, MLA_HEADS, MLA_NOPE_DIM + MLA_ROPE_DIM)
        uq_r = uq[:, :, MLA_NOPE_DIM:]
        z = jnp.zeros((MLA_Q_RANK, MLA_HEADS, LANES // 2 - MLA_ROPE_DIM // 2), BF16)
        wuq = jnp.concatenate(
            [uq[:, :, :MLA_NOPE_DIM], uq_r[:, :, 0::2], z, uq_r[:, :, 1::2], z],
            axis=-1).reshape(MLA_Q_RANK, MLA_HEADS * MLA_QK_PAD)
        ukv = w_mla_ukv[l].astype(BF16).reshape(MLA_KV_RANK, MLA_HEADS, MLA_NOPE_DIM + MLA_V_DIM)
        wuk = ukv[:, :, :MLA_NOPE_DIM].reshape(MLA_KV_RANK, MLA_HEADS * MLA_NOPE_DIM)
        wuv = ukv[:, :, MLA_NOPE_DIM:].reshape(MLA_KV_RANK, MLA_HEADS * MLA_V_DIM)

        gq = (gqa_q_gain[l][perm_head] * (LOG2E * HEAD_DIM ** -0.5))[None, :]
        gk = gqa_k_gain[l][perm_head][None, :]

        h1, qgt, kg, vgt, qmt, km, vmt = _inproj(
            x2, mod, cg, sg, cm, sm, w_attn, gq, gk, mla_q_gain[l][None, :],
            mla_kv_gain[l][None, :], wuq, wuk, wuv, batch, seq)

        y_gqa = _attention(
            qgt, kg.reshape(batch, seq, GQA_KV_COLS), vgt,
            kv_heads=GQA_KV_HEADS, dv=HEAD_DIM, tq=TQ_GQA)
        y_mla = _attention(
            qmt, km.reshape(batch, seq, MLA_HEADS * MLA_QK_PAD), vmt,
            kv_heads=MLA_HEADS, dv=MLA_V_DIM, tq=TQ_MLA)

        x2 = _merge(
            x2, mod, h1, y_gqa.reshape(t, GQA_Q_COLS), y_mla.reshape(t, MLA_HEADS * MLA_V_DIM),
            w_gate, b_gates[l][None, :], w_branch_gqa[l].astype(BF16),
            w_branch_mla[l].astype(BF16), w_out[l].astype(BF16),
            ln1_g[l][None, :], ln1_b[l][None, :], seq)

        x2 = _ffn(
            x2, mod, w_ffn_gate[l].astype(BF16), w_ffn_up[l].astype(BF16),
            w_ffn_down[l].astype(BF16), ln2_g[l][None, :], ln2_b[l][None, :], seq)

    return x2.reshape(batch, seq, d)
```

```python
import functools

import jax
import jax.numpy as jnp
import numpy as np
from jax import lax
from jax.experimental import pallas as pl
from jax.experimental.pallas import tpu as pltpu

D_MODEL = 2048
GRID_W = 64
ROPE_THETA = 10000.0
HEAD_DIM = 128
GQA_Q_HEADS = 8
GQA_KV_HEADS = 2
MLA_HEADS = 8
MLA_Q_RANK = 512
MLA_KV_RANK = 512
MLA_NOPE_DIM = 128
MLA_ROPE_DIM = 64
MLA_V_DIM = 128
D_FF = 5632
DEPTH = 1
DEEPNORM_ALPHA = (2.0 * DEPTH) ** 0.25
LN_EPS = 1e-5
RMS_EPS = 1e-6

GQA_Q_COLS = GQA_Q_HEADS * HEAD_DIM
GQA_KV_COLS = GQA_KV_HEADS * HEAD_DIM
MLA_QK_PAD = 256
LANES = 128
LOG2E = 1.4426950408889634

VMEM_LIMIT = 56 * 1024 * 1024

TM_IN = 256
TV_CHUNK = TM_IN
TK_ATT = 256
ATT_STEPS = 14
SUM_ROWS = 16
EXP_ROWS = 128
TQ_GQA = 1024
TQ_MLA = 4096
TM_MERGE = 512
TN_MERGE = 512
TM_FFN = 512
TF_FFN = 512

BF16 = jnp.bfloat16
F32 = jnp.float32


def _const_spec(shape):
    nd = len(shape)
    return pl.BlockSpec(shape, lambda *_: (0,) * nd, pipeline_mode=pl.Buffered(1))


def _ln(x):
    mu = jnp.mean(x, axis=-1, keepdims=True)
    xc = x - mu
    var = jnp.mean(xc * xc, axis=-1, keepdims=True)
    return xc * lax.rsqrt(var + LN_EPS)


def _rms(x, gain):
    ms = jnp.mean(x * x, axis=-1, keepdims=True)
    return x * lax.rsqrt(ms + RMS_EPS) * gain


def _rope(x, c, s):
    return x * c + pltpu.roll(x, LANES // 2, 1) * s


def _adaln_kernel(c_ref, w_ref, b_ref, o_ref):
    c = c_ref[...]
    act = (c * jax.nn.sigmoid(c)).astype(BF16)
    o_ref[...] = jnp.dot(act, w_ref[...].astype(BF16), preferred_element_type=F32) + b_ref[...]


def _adaln(c_pad, w_ada, b_ada):
    n = w_ada.shape[1]
    tn = 1024
    return pl.pallas_call(
        _adaln_kernel,
        grid=(n // tn,),
        in_specs=[
            pl.BlockSpec((8, D_MODEL), lambda j: (0, 0)),
            pl.BlockSpec((D_MODEL, tn), lambda j: (0, j)),
            pl.BlockSpec((1, tn), lambda j: (0, j)),
        ],
        out_specs=pl.BlockSpec((8, tn), lambda j: (0, j)),
        out_shape=jax.ShapeDtypeStruct((8, n), F32),
        compiler_params=pltpu.CompilerParams(
            dimension_semantics=("arbitrary",), vmem_limit_bytes=VMEM_LIMIT),
        name="adaln",
    )(c_pad, w_ada, b_ada)


def _inproj_kernel(x_ref, mod_ref, cg_ref, sg_ref, cm_ref, sm_ref, w_ref,
                   gq_ref, gk_ref, gql_ref, gkvl_ref, wuq_ref, wuk_ref, wuv_ref,
                   h_ref, qg_ref, kg_ref, vgt_ref, qm_ref, km_ref, vmt_ref, *, mla_scale):
    x = x_ref[...]
    shift = mod_ref[0, 0:1, :]
    scale = mod_ref[0, 1:2, :]
    h = (_ln(x) * (1.0 + scale) + shift).astype(BF16)
    h_ref[...] = h
    proj = jnp.dot(h, w_ref[...], preferred_element_type=F32)

    cg = cg_ref[...]
    sg = sg_ref[...]
    cm = cm_ref[...]
    sm = sm_ref[...]
    ones_rows = jnp.ones((SUM_ROWS, x.shape[0]), BF16)

    for hd in range(GQA_Q_HEADS):
        q = proj[:, hd * HEAD_DIM:(hd + 1) * HEAD_DIM]
        q = _rope(_rms(q, gq_ref[...]), cg, sg)
        qg_ref[0, hd, 0] = q.T.astype(BF16)
    off = GQA_Q_COLS
    for hd in range(GQA_KV_HEADS):
        k = proj[:, off + hd * HEAD_DIM: off + (hd + 1) * HEAD_DIM]
        k = _rope(_rms(k, gk_ref[...]), cg, sg)
        kg_ref[:, hd * HEAD_DIM:(hd + 1) * HEAD_DIM] = k.astype(BF16)
    off += GQA_KV_COLS
    for hd in range(GQA_KV_HEADS):
        v = proj[:, off + hd * HEAD_DIM: off + (hd + 1) * HEAD_DIM]
        vgt_ref[0, hd, 0, 0:HEAD_DIM, :] = v.T.astype(BF16)
        vgt_ref[0, hd, 0, HEAD_DIM:, :] = ones_rows
    off += GQA_KV_COLS

    q_lat = _rms(proj[:, off: off + MLA_Q_RANK], gql_ref[...]).astype(BF16)
    off += MLA_Q_RANK
    kv_lat = _rms(proj[:, off: off + MLA_KV_RANK], gkvl_ref[...]).astype(BF16)
    off += MLA_KV_RANK
    k_rope = _rope(proj[:, off: off + LANES], cm, sm).astype(BF16)

    q_m = jnp.dot(q_lat, wuq_ref[...], preferred_element_type=F32) * mla_scale
    k_n = jnp.dot(kv_lat, wuk_ref[...], preferred_element_type=F32)
    v_m = jnp.dot(kv_lat, wuv_ref[...], preferred_element_type=F32)
    for hd in range(MLA_HEADS):
        b0 = hd * MLA_QK_PAD
        qm_ref[0, hd, 0, 0:LANES, :] = q_m[:, b0: b0 + LANES].T.astype(BF16)
        qm_ref[0, hd, 0, LANES:, :] = _rope(
            q_m[:, b0 + LANES: b0 + 2 * LANES], cm, sm).T.astype(BF16)
        km_ref[:, b0: b0 + LANES] = k_n[:, hd * LANES:(hd + 1) * LANES].astype(BF16)
        km_ref[:, b0 + LANES: b0 + 2 * LANES] = k_rope
        vmt_ref[0, hd, 0, 0:MLA_V_DIM, :] = (
            v_m[:, hd * MLA_V_DIM:(hd + 1) * MLA_V_DIM].T.astype(BF16))
        vmt_ref[0, hd, 0, MLA_V_DIM:, :] = ones_rows


def _inproj(x2, mod, cg, sg, cm, sm, w_attn, gq, gk, gql, gkvl, wuq, wuk, wuv, batch, seq):
    t = x2.shape[0]
    tm = TM_IN
    nt = seq // tm
    n_attn = w_attn.shape[1]
    kern = functools.partial(
        _inproj_kernel, mla_scale=float(LOG2E * (MLA_NOPE_DIM + MLA_ROPE_DIM) ** -0.5))
    tok = lambda i: (i, 0)
    pos = lambda i: (i % nt, 0)
    in_specs = [
        pl.BlockSpec((tm, D_MODEL), tok),
        pl.BlockSpec((1, 6, D_MODEL), lambda i: (i // nt, 0, 0)),
        pl.BlockSpec((tm, LANES), pos),
        pl.BlockSpec((tm, LANES), pos),
        pl.BlockSpec((tm, LANES), pos),
        pl.BlockSpec((tm, LANES), pos),
        _const_spec((D_MODEL, n_attn)),
        _const_spec((1, HEAD_DIM)),
        _const_spec((1, HEAD_DIM)),
        _const_spec((1, MLA_Q_RANK)),
        _const_spec((1, MLA_KV_RANK)),
        _const_spec(wuq.shape),
        _const_spec(wuk.shape),
        _const_spec(wuv.shape),
    ]
    out_shape = [
        jax.ShapeDtypeStruct((t, D_MODEL), BF16),
        jax.ShapeDtypeStruct((batch, GQA_Q_HEADS, seq // tm, HEAD_DIM, tm), BF16),
        jax.ShapeDtypeStruct((t, GQA_KV_COLS), BF16),
        jax.ShapeDtypeStruct(
            (batch, GQA_KV_HEADS, seq // TV_CHUNK, HEAD_DIM + SUM_ROWS, TV_CHUNK), BF16),
        jax.ShapeDtypeStruct((batch, MLA_HEADS, seq // tm, MLA_QK_PAD, tm), BF16),
        jax.ShapeDtypeStruct((t, MLA_HEADS * MLA_QK_PAD), BF16),
        jax.ShapeDtypeStruct(
            (batch, MLA_HEADS, seq // TV_CHUNK, MLA_V_DIM + SUM_ROWS, TV_CHUNK), BF16),
    ]
    vt_map = lambda i: (i // nt, 0, i % nt, 0, 0)
    out_specs = [
        pl.BlockSpec((tm, D_MODEL), tok),
        pl.BlockSpec((1, GQA_Q_HEADS, 1, HEAD_DIM, tm), vt_map),
        pl.BlockSpec((tm, GQA_KV_COLS), tok),
        pl.BlockSpec((1, GQA_KV_HEADS, 1, HEAD_DIM + SUM_ROWS, TV_CHUNK), vt_map),
        pl.BlockSpec((1, MLA_HEADS, 1, MLA_QK_PAD, tm), vt_map),
        pl.BlockSpec((tm, MLA_HEADS * MLA_QK_PAD), tok),
        pl.BlockSpec((1, MLA_HEADS, 1, MLA_V_DIM + SUM_ROWS, TV_CHUNK), vt_map),
    ]
    return pl.pallas_call(
        kern,
        grid=(t // tm,),
        in_specs=in_specs,
        out_specs=out_specs,
        out_shape=out_shape,
        compiler_params=pltpu.CompilerParams(
            dimension_semantics=("arbitrary",), vmem_limit_bytes=VMEM_LIMIT),
        name="inproj",
    )(x2, mod, cg, sg, cm, sm, w_attn, gq, gk, gql, gkvl, wuq, wuk, wuv)


def _attn_kernel(qt_ref, k_ref, vt_ref, o_ref, st_scr, pb_scr, acc_scr,
                 *, groups, dq, dv, tq, tk, nk, steps):
    mq = groups * tq
    cb = qt_ref.shape[-1]
    nb = mq // cb
    cols = [slice(n * cb, (n + 1) * cb) for n in range(nb)]
    grp = [(n * cb) // tq for n in range(nb)]
    tok = [(n * cb) % tq for n in range(nb)]

    def scores(c, st_ref, n):
        kc = k_ref[0, pl.ds(pl.multiple_of(c * tk, tk), tk), :]
        st = jnp.dot(kc, qt_ref[0, grp[n], tok[n] // cb],
                     preferred_element_type=F32)
        st_ref[:, cols[n]] = st
        return jnp.max(st, axis=0, keepdims=True)

    def values(c, pb_ref, alpha, n, init=False):
        pv = jnp.dot(vt_ref[0, 0, c], pb_ref[:, cols[n]], preferred_element_type=F32)
        acc_scr[:, cols[n]] = pv if init else alpha * acc_scr[:, cols[n]] + pv

    def softmax(st_ref, pb_ref, cmax, m_prev, n):
        m_new = jnp.maximum(m_prev, cmax)
        alpha = jnp.exp2(m_prev - m_new)
        for r in range(tk // EXP_ROWS):
            rows = slice(r * EXP_ROWS, (r + 1) * EXP_ROWS)
            pb_ref[rows, cols[n]] = jnp.exp2((st_ref[rows, cols[n]] - m_new).astype(BF16))
        return m_new, alpha

    def step(c, cur, nxt, carry, first=False, second=False, last=False):
        cmax, m_prev, alpha_prev = carry
        cmax_next, m_new, alpha = list(cmax), [], []
        for n in range(nb):
            if not last:
                cmax_next[n] = scores(c + 1, st_scr.at[nxt], n)
        for n in range(nb):
            m_n, a_n = softmax(st_scr.at[cur], pb_scr.at[cur], cmax[n], m_prev[n], n)
            m_new.append(m_n)
            alpha.append(a_n)
            if not first:
                values(c - 1, pb_scr.at[nxt], alpha_prev[n], n, init=second)
        return cmax_next, m_new, alpha

    cmax0 = []
    for n in range(nb):
        cmax0.append(scores(0, st_scr.at[0], n))
    carry = (cmax0, [jnp.full((1, cb), -jnp.inf, F32)] * nb, [jnp.ones((1, cb), F32)] * nb)
    carry = step(0, 0, 1, carry, first=True)
    carry = step(1, 1, 0, carry, second=True)

    n_loop = (nk - 3) // steps

    def body(i, carry):
        c0 = steps * i + 2
        for s in range(steps):
            carry = step(c0 + s, s % 2, (1 + s) % 2, carry)
        return carry

    if n_loop == 1:
        carry = body(0, carry)
    elif n_loop > 1:
        carry = lax.fori_loop(0, n_loop, body, carry)
    for c in range(n_loop * steps + 2, nk - 1):
        carry = step(c, c % 2, (c + 1) % 2, carry)
    _, _, alpha = step(nk - 1, (nk - 1) % 2, nk % 2, carry, last=True)
    for n in range(nb):
        acc = alpha[n] * acc_scr[:, cols[n]] + jnp.dot(
            vt_ref[0, 0, nk - 1], pb_scr[(nk - 1) % 2, :, cols[n]], preferred_element_type=F32)
        out_t = acc[0:dv] / acc[dv:dv + 1]
        o_ref[0, tok[n]:tok[n] + cb, grp[n] * dv:(grp[n] + 1) * dv] = out_t.T.astype(o_ref.dtype)


def _attention(qt, k, vt, *, kv_heads, dv, tq):
    batch, q_heads, n_qc, dq, cb = qt.shape
    seq = n_qc * cb
    groups = q_heads // kv_heads
    tk = TK_ATT
    assert tk == TV_CHUNK and vt.shape[-2:] == (dv + SUM_ROWS, tk) and tq % cb == 0
    nk = seq // tk
    mq = groups * tq
    kern = functools.partial(_attn_kernel, groups=groups, dq=dq, dv=dv, tq=tq, tk=tk, nk=nk,
                             steps=ATT_STEPS)
    return pl.pallas_call(
        kern,
        grid=(batch, kv_heads, seq // tq),
        in_specs=[
            pl.BlockSpec((1, groups, tq // cb, dq, cb), lambda b, h, i: (b, h, i, 0, 0)),
            pl.BlockSpec((1, seq, dq), lambda b, h, i: (b, 0, h)),
            pl.BlockSpec((1, 1, nk, dv + SUM_ROWS, tk), lambda b, h, i: (b, h, 0, 0, 0)),
        ],
        out_specs=pl.BlockSpec((1, tq, groups * dv), lambda b, h, i: (b, i, h)),
        out_shape=jax.ShapeDtypeStruct((batch, seq, q_heads * dv), BF16),
        scratch_shapes=[
            pltpu.VMEM((2, tk, mq), F32),
            pltpu.VMEM((2, tk, mq), BF16),
            pltpu.VMEM((dv + SUM_ROWS, mq), F32),
        ],
        compiler_params=pltpu.CompilerParams(
            dimension_semantics=("arbitrary", "arbitrary", "arbitrary"),
            vmem_limit_bytes=VMEM_LIMIT),
        name=f"attn_g{groups}",
    )(qt, k, vt)


def _merge_kernel(x_ref, mod_ref, h_ref, yg_ref, ym_ref, wga_ref, wgb_ref, bga_ref, bgb_ref,
                  wbg_ref, wbm_ref, wo_ref, g_ref, b_ref, o_ref, h_scr, acc_scr):
    j = pl.program_id(1)
    last = pl.num_programs(1) - 1

    def chunk(h):
        la = jnp.dot(h, wga_ref[...], preferred_element_type=F32) + bga_ref[...]
        lb = jnp.dot(h, wgb_ref[...], preferred_element_type=F32) + bgb_ref[...]
        a = jnp.dot(yg_ref[...], wbg_ref[...], preferred_element_type=F32)
        b = jnp.dot(ym_ref[...], wbm_ref[...], preferred_element_type=F32)
        merged = jax.nn.sigmoid(la) * a + jax.nn.sigmoid(lb) * b
        return jnp.dot(merged.astype(BF16), wo_ref[...], preferred_element_type=F32)

    @pl.when(j == 0)
    def _():
        h = h_ref[...]
        h_scr[...] = h
        acc_scr[...] = chunk(h)

    @pl.when((j > 0) & (j < last))
    def _():
        acc_scr[...] += chunk(h_scr[...])

    @pl.when(j == last)
    def _():
        acc = acc_scr[...] + chunk(h_scr[...])
        r = DEEPNORM_ALPHA * x_ref[...] + mod_ref[0, 2:3, :] * acc
        o_ref[...] = _ln(r) * g_ref[...] + b_ref[...]


def _merge(x2, mod, h, yg, ym, w_gate, b_gate, wbg, wbm, wo, ln_g, ln_b, seq):
    t = x2.shape[0]
    tm, tn = TM_MERGE, TN_MERGE
    nt = seq // tm
    nj = D_MODEL // tn
    tok = lambda i, j: (i, 0)
    return pl.pallas_call(
        _merge_kernel,
        grid=(t // tm, nj),
        in_specs=[
            pl.BlockSpec((tm, D_MODEL), tok),
            pl.BlockSpec((1, 6, D_MODEL), lambda i, j: (i // nt, 0, 0)),
            pl.BlockSpec((tm, D_MODEL), tok),
            pl.BlockSpec((tm, yg.shape[1]), tok),
            pl.BlockSpec((tm, ym.shape[1]), tok),
            pl.BlockSpec((D_MODEL, tn), lambda i, j: (0, j)),
            pl.BlockSpec((D_MODEL, tn), lambda i, j: (0, j + nj)),
            pl.BlockSpec((1, tn), lambda i, j: (0, j)),
            pl.BlockSpec((1, tn), lambda i, j: (0, j + nj)),
            pl.BlockSpec((wbg.shape[0], tn), lambda i, j: (0, j)),
            pl.BlockSpec((wbm.shape[0], tn), lambda i, j: (0, j)),
            pl.BlockSpec((tn, D_MODEL), lambda i, j: (j, 0)),
            pl.BlockSpec((1, D_MODEL), lambda i, j: (0, 0)),
            pl.BlockSpec((1, D_MODEL), lambda i, j: (0, 0)),
        ],
        out_specs=pl.BlockSpec((tm, D_MODEL), tok),
        out_shape=jax.ShapeDtypeStruct((t, D_MODEL), F32),
        scratch_shapes=[pltpu.VMEM((tm, D_MODEL), BF16), pltpu.VMEM((tm, D_MODEL), F32)],
        compiler_params=pltpu.CompilerParams(
            dimension_semantics=("arbitrary", "arbitrary"), vmem_limit_bytes=VMEM_LIMIT),
        name="merge",
    )(x2, mod, h, yg, ym, w_gate, w_gate, b_gate, b_gate, wbg, wbm, wo, ln_g, ln_b)


def _ffn_kernel(x_ref, mod_ref, wg_ref, wu_ref, wd_ref, g_ref, b_ref, o_ref, h_scr, acc_scr):
    j = pl.program_id(1)
    last = pl.num_programs(1) - 1

    def chunk(h):
        gt = jnp.dot(h, wg_ref[...], preferred_element_type=F32)
        up = jnp.dot(h, wu_ref[...], preferred_element_type=F32)
        a = (gt * jax.nn.sigmoid(gt) * up).astype(BF16)
        return jnp.dot(a, wd_ref[...], preferred_element_type=F32)

    @pl.when(j == 0)
    def _():
        shift = mod_ref[0, 3:4, :]
        scale = mod_ref[0, 4:5, :]
        h = (_ln(x_ref[...]) * (1.0 + scale) + shift).astype(BF16)
        h_scr[...] = h
        acc_scr[...] = chunk(h)

    @pl.when((j > 0) & (j < last))
    def _():
        acc_scr[...] += chunk(h_scr[...])

    @pl.when(j == last)
    def _():
        acc = acc_scr[...] + chunk(h_scr[...])
        r = DEEPNORM_ALPHA * x_ref[...] + mod_ref[0, 5:6, :] * acc
        o_ref[...] = _ln(r) * g_ref[...] + b_ref[...]


def _ffn(x1, mod, wg, wu, wd, ln_g, ln_b, seq):
    t = x1.shape[0]
    tm, tf = TM_FFN, TF_FFN
    nt = seq // tm
    tok = lambda i, j: (i, 0)
    return pl.pallas_call(
        _ffn_kernel,
        grid=(t // tm, D_FF // tf),
        in_specs=[
            pl.BlockSpec((tm, D_MODEL), tok),
            pl.BlockSpec((1, 6, D_MODEL), lambda i, j: (i // nt, 0, 0)),
            pl.BlockSpec((D_MODEL, tf), lambda i, j: (0, j)),
            pl.BlockSpec((D_MODEL, tf), lambda i, j: (0, j)),
            pl.BlockSpec((tf, D_MODEL), lambda i, j: (j, 0)),
            pl.BlockSpec((1, D_MODEL), lambda i, j: (0, 0)),
            pl.BlockSpec((1, D_MODEL), lambda i, j: (0, 0)),
        ],
        out_specs=pl.BlockSpec((tm, D_MODEL), tok),
        out_shape=jax.ShapeDtypeStruct((t, D_MODEL), F32),
        scratch_shapes=[pltpu.VMEM((tm, D_MODEL), BF16), pltpu.VMEM((tm, D_MODEL), F32)],
        compiler_params=pltpu.CompilerParams(
            dimension_semantics=("arbitrary", "arbitrary"), vmem_limit_bytes=VMEM_LIMIT),
        name="ffn",
    )(x1, mod, wg, wu, wd, ln_g, ln_b)


def _rope_tables(seq, dim):
    rows = seq // GRID_W
    quarter = dim // 4
    inv_freq = ROPE_THETA ** (-jnp.arange(quarter, dtype=F32) / quarter)
    row_ang = jnp.arange(rows, dtype=F32)[:, None] * inv_freq
    col_ang = jnp.arange(GRID_W, dtype=F32)[:, None] * inv_freq

    def table(fn):
        return jnp.concatenate([
            jnp.broadcast_to(fn(row_ang)[:, None, :], (rows, GRID_W, quarter)),
            jnp.broadcast_to(fn(col_ang)[None, :, :], (rows, GRID_W, quarter)),
        ], axis=-1).reshape(seq, 2 * quarter)

    half = dim // 2
    pad = jnp.zeros((seq, LANES // 2 - half), F32)
    cos, sin = table(jnp.cos), table(jnp.sin)
    c = jnp.concatenate([cos, pad, cos, pad], axis=-1)
    s = jnp.concatenate([-sin, pad, sin, pad], axis=-1)
    return c, s


def _deinterleave(n):
    return np.concatenate([np.arange(0, n, 2), np.arange(1, n, 2)])


def kernel(x, c, w_ada, b_ada, w_in, b_gates, gqa_q_gain, gqa_k_gain, mla_q_gain, mla_kv_gain,
           w_mla_uq, w_mla_ukv, w_branch_gqa, w_branch_mla, w_out, ln1_g, ln1_b,
           w_ffn_gate, w_ffn_up, w_ffn_down, ln2_g, ln2_b):
    batch, seq, d = x.shape
    assert d == D_MODEL and w_ada.shape[0] == DEPTH
    t = batch * seq
    x2 = x.reshape(t, d)

    cg, sg = _rope_tables(seq, HEAD_DIM)
    cm, sm = _rope_tables(seq, MLA_ROPE_DIM)
    perm_head = _deinterleave(HEAD_DIM)
    zeros32 = lambda rows: jnp.zeros((rows, LANES // 2 - MLA_ROPE_DIM // 2), BF16)

    c_pad = jnp.zeros((8, d), F32).at[:batch].set(c)

    for l in range(DEPTH):
        mod = _adaln(c_pad, w_ada[l], b_ada[l][None, :])[:batch].reshape(batch, 6, d)

        wl = w_in[l]
        o0 = 0
        wq = wl[:, o0:o0 + GQA_Q_COLS].reshape(d, GQA_Q_HEADS, HEAD_DIM)[:, :, perm_head]
        o0 += GQA_Q_COLS
        wk = wl[:, o0:o0 + GQA_KV_COLS].reshape(d, GQA_KV_HEADS, HEAD_DIM)[:, :, perm_head]
        o0 += GQA_KV_COLS
        wv = wl[:, o0:o0 + GQA_KV_COLS]
        o0 += GQA_KV_COLS
        wql = wl[:, o0:o0 + MLA_Q_RANK]
        o0 += MLA_Q_RANK
        wkvl = wl[:, o0:o0 + MLA_KV_RANK]
        o0 += MLA_KV_RANK
        wkr = wl[:, o0:o0 + MLA_ROPE_DIM].astype(BF16)
        o0 += MLA_ROPE_DIM
        w_gate = wl[:, o0:].astype(BF16)
        wkr_pad = jnp.concatenate(
            [wkr[:, 0::2], zeros32(d), wkr[:, 1::2], zeros32(d)], axis=-1)
        w_attn = jnp.concatenate([
            wq.reshape(d, GQA_Q_COLS).astype(BF16), wk.reshape(d, GQA_KV_COLS).astype(BF16),
            wv.astype(BF16), wql.astype(BF16), wkvl.astype(BF16), wkr_pad], axis=-1)

        uq = w_mla_uq[l].astype(BF16).reshape(MLA_Q_RANK, MLA_HEADS, MLA_NOPE_DIM + MLA_ROPE_DIM)
        uq_r = uq[:, :, MLA_NOPE_DIM:]
        z = jnp.zeros((MLA_Q_RANK, MLA_HEADS, LANES // 2 - MLA_ROPE_DIM // 2), BF16)
        wuq = jnp.concatenate(
            [uq[:, :, :MLA_NOPE_DIM], uq_r[:, :, 0::2], z, uq_r[:, :, 1::2], z],
            axis=-1).reshape(MLA_Q_RANK, MLA_HEADS * MLA_QK_PAD)
        ukv = w_mla_ukv[l].astype(BF16).reshape(MLA_KV_RANK, MLA_HEADS, MLA_NOPE_DIM + MLA_V_DIM)
        wuk = ukv[:, :, :MLA_NOPE_DIM].reshape(MLA_KV_RANK, MLA_HEADS * MLA_NOPE_DIM)
        wuv = ukv[:, :, MLA_NOPE_DIM:].reshape(MLA_KV_RANK, MLA_HEADS * MLA_V_DIM)

        gq = (gqa_q_gain[l][perm_head] * (LOG2E * HEAD_DIM ** -0.5))[None, :]
        gk = gqa_k_gain[l][perm_head][None, :]

        h1, qgt, kg, vgt, qmt, km, vmt = _inproj(
            x2, mod, cg, sg, cm, sm, w_attn, gq, gk, mla_q_gain[l][None, :],
            mla_kv_gain[l][None, :], wuq, wuk, wuv, batch, seq)

        y_gqa = _attention(
            qgt, kg.reshape(batch, seq, GQA_KV_COLS), vgt,
            kv_heads=GQA_KV_HEADS, dv=HEAD_DIM, tq=TQ_GQA)
        y_mla = _attention(
            qmt, km.reshape(batch, seq, MLA_HEADS * MLA_QK_PAD), vmt,
            kv_heads=MLA_HEADS, dv=MLA_V_DIM, tq=TQ_MLA)

        x2 = _merge(
            x2, mod, h1, y_gqa.reshape(t, GQA_Q_COLS), y_mla.reshape(t, MLA_HEADS * MLA_V_DIM),
            w_gate, b_gates[l][None, :], w_branch_gqa[l].astype(BF16),
            w_branch_mla[l].astype(BF16), w_out[l].astype(BF16),
            ln1_g[l][None, :], ln1_b[l][None, :], seq)

        x2 = _ffn(
            x2, mod, w_ffn_gate[l].astype(BF16), w_ffn_up[l].astype(BF16),
            w_ffn_down[l].astype(BF16), ln2_g[l][None, :], ln2_b[l][None, :], seq)

    return x2.reshape(batch, seq, d)
```
